```python
import jax, jax.numpy as jnp
from jax import lax
import numpy as np

D_MODEL = 1024
BATCH = 8
SEQ = 4096
DEPTH = 2
DEC_BATCH = 32
DEC_SEQ = 1
PAST_LEN = 16384
PAGE_SIZE = 128

N_META = 16
N_HEADS = 8
HEAD_DIM = 64
N_KV_HEADS = 4
GROUP = N_HEADS // N_KV_HEADS
ROT_DIM = HEAD_DIM // 4
ROPE_THETA = 500000.0
ATT_W = N_HEADS * HEAD_DIM
KV_W = N_KV_HEADS * HEAD_DIM
N_IDX_HEADS = 8
IDX_DIM = 64
IDX_ROT_DIM = IDX_DIM // 4
IDX_W_SCALE = (N_IDX_HEADS * IDX_DIM) ** -0.5
TOPK_MAX = 256
QBLOCK = 128
N_REC_HEADS = 4
REC_KEY_DIM = 128
REC_VAL_DIM = 128
REC_W = N_REC_HEADS * REC_KEY_DIM
REC_V = N_REC_HEADS * REC_VAL_DIM
REC_CHUNK = 64
D_FF = -(-8 * D_MODEL // (3 * 256)) * 256
EPS = 1e-6
SPLIT_SIZES = (ATT_W, KV_W, KV_W, N_IDX_HEADS * IDX_DIM, IDX_DIM, N_IDX_HEADS,
               REC_W, REC_W, REC_V, REC_V, D_MODEL, D_MODEL)
D_IN = sum(SPLIT_SIZES)

kernel_name = "meta_dsa_hgrn2_gated_hybrid_step"


def rms_norm(x, g):
    xf = x.astype(jnp.float32)
    y = xf * lax.rsqrt(jnp.mean(xf * xf, axis=-1, keepdims=True) + EPS)
    return (y * g.astype(jnp.float32)).astype(x.dtype)


def rope(x, pos, rot_dim):
    xf = x.astype(jnp.float32)
    half = rot_dim // 2
    inv = jnp.power(ROPE_THETA, -2.0 * jnp.arange(half, dtype=jnp.float32) / rot_dim)
    ang = pos.astype(jnp.float32)[:, None] * inv[None, :]
    cos = jnp.cos(ang)[:, None, :]
    sin = jnp.sin(ang)[:, None, :]
    x1 = xf[..., :half]
    x2 = xf[..., half:rot_dim]
    out = jnp.concatenate([x1 * cos - x2 * sin, x2 * cos + x1 * sin, xf[..., rot_dim:]], axis=-1)
    return out.astype(x.dtype)


def split_columns(u):
    offs, acc = [], 0
    for s in SPLIT_SIZES[:-1]:
        acc += s
        offs.append(acc)
    return jnp.split(u, offs, axis=-1)


def indexer_scores(qi, wi, ki):
    dots = jnp.einsum('bthd,bsd->bths', qi.astype(jnp.float32), ki.astype(jnp.float32))
    return jnp.einsum('bths,bth->bts', jax.nn.relu(dots), wi.astype(jnp.float32))


def attend_selected(q, k_sel, v_sel, valid):
    B, T = q.shape[:2]
    qg = q.reshape(B, T, N_KV_HEADS, GROUP, HEAD_DIM).astype(jnp.float32)
    s = jnp.einsum('btngd,btknd->btngk', qg, k_sel.astype(jnp.float32)) * (HEAD_DIM ** -0.5)
    s = jnp.where(valid[:, :, None, None, :], s, -jnp.inf)
    p = jax.nn.softmax(s, axis=-1)
    o = jnp.einsum('btngk,btknd->btngd', p, v_sel.astype(jnp.float32))
    return o.reshape(B, T, ATT_W).astype(q.dtype)


def take_rows(a, idx):
    return jax.vmap(lambda arr, i: arr[i])(a, idx)


def prompt_attention_block(q, qi, wi, qpos, k, v, ki, top_k):
    S = k.shape[1]
    kpos = jnp.arange(S, dtype=jnp.int32)
    sc = indexer_scores(qi, wi, ki)
    sc = jnp.where(kpos[None, None, :] <= qpos[None, :, None], sc, -jnp.inf)
    _, idx = lax.top_k(sc, top_k)
    valid = idx <= qpos[None, :, None]
    return attend_selected(q, take_rows(k, idx), take_rows(v, idx), valid)


def gla_chunk(S0, q, k, v, logf):
    q = q.astype(jnp.float32)
    k = k.astype(jnp.float32)
    v = v.astype(jnp.float32)
    S0 = S0.astype(jnp.float32)
    b = jnp.cumsum(logf.astype(jnp.float32), axis=1)
    C = q.shape[1]
    causal = jnp.tril(jnp.ones((C, C), dtype=bool))
    diff = b[:, :, None] - b[:, None, :]
    decay = jnp.exp(jnp.where(causal[None, :, :, None, None], diff, -jnp.inf))
    scores = jnp.einsum('bthk,bshk,btshk->bths', q, k, decay)
    o = jnp.einsum('bths,bshv->bthv', scores, v) + jnp.einsum('bthk,bhkv->bthv', q * jnp.exp(b), S0)
    b_last = b[:, -1]
    S = jnp.exp(b_last)[..., None] * S0 + jnp.einsum('bshk,bshv->bhkv', k * jnp.exp(b_last[:, None] - b), v)
    return o, S


def mixer_front(x, pos, w_in_l, g_mix, g_q, g_k, lb):
    B, T, _ = x.shape
    h = rms_norm(x, g_mix)
    q, k, v, qi, ki, wi, qr, fr, ir, gr, ga, gb = split_columns(h @ w_in_l)
    q = rope(rms_norm(q.reshape(B, T, N_HEADS, HEAD_DIM), g_q), pos, ROT_DIM)
    k = rope(rms_norm(k.reshape(B, T, N_KV_HEADS, HEAD_DIM), g_k), pos, ROT_DIM)
    v = v.reshape(B, T, N_KV_HEADS, HEAD_DIM)
    qi = rope(qi.reshape(B, T, N_IDX_HEADS, IDX_DIM), pos, IDX_ROT_DIM)
    ki = rope(ki.reshape(B, T, 1, IDX_DIM), pos, IDX_ROT_DIM)[:, :, 0]
    wi = wi * IDX_W_SCALE
    f = lb + (1.0 - lb) * jax.nn.sigmoid(fr.astype(jnp.float32))
    rq = jax.nn.silu(qr).reshape(B, T, N_REC_HEADS, REC_KEY_DIM)
    rk = (1.0 - f).reshape(B, T, N_REC_HEADS, REC_KEY_DIM)
    rv = ir.reshape(B, T, N_REC_HEADS, REC_VAL_DIM)
    rlogf = jnp.log(f).reshape(B, T, N_REC_HEADS, REC_KEY_DIM)
    return (q, k, v, qi, ki, wi), (rq, rk, rv, rlogf), (gr, ga, gb)


def mixer_back(x, att, rec_o, gates, g_rec, w_pa_l, w_pb_l, w_o_l):
    gr, ga, gb = gates
    B, T, _ = x.shape
    rec = rms_norm(rec_o.astype(x.dtype), g_rec) * jax.nn.silu(gr.reshape(B, T, N_REC_HEADS, REC_VAL_DIM))
    a = att @ w_pa_l
    b = rec.reshape(B, T, REC_V) @ w_pb_l
    return x + (jax.nn.sigmoid(ga) * a + jax.nn.sigmoid(gb) * b) @ w_o_l


def swiglu_ffn(x, g, w_gu_l, w_down_l):
    h = rms_norm(x, g)
    gate, up = jnp.split(h @ w_gu_l, 2, axis=-1)
    return x + (jax.nn.silu(gate) * up) @ w_down_l


def prompt_mixer(x, w_in_l, g_mix, g_q, g_k, lb, g_rec, w_pa_l, w_pb_l, w_o_l):
    B, T, _ = x.shape
    n_real = T - N_META
    pos = jnp.arange(T, dtype=jnp.int32)
    (q, k, v, qi, ki, wi), (rq, rk, rv, rlogf), gates = mixer_front(x, pos, w_in_l, g_mix, g_q, g_k, lb)

    top_k = min(TOPK_MAX, n_real // 4)
    attend = lambda qb, qib, wib, pb: prompt_attention_block(qb, qib, wib, pb, k, v, ki, top_k)
    att_meta = attend(q[:, :N_META], qi[:, :N_META], wi[:, :N_META], pos[:N_META])
    n_blk = n_real // QBLOCK

    def to_qblocks(a):
        a = a[:, N_META:]
        return a.reshape((B, n_blk, QBLOCK) + a.shape[2:]).swapaxes(0, 1)

    att_real = lax.map(lambda xs: attend(*xs),
                       (to_qblocks(q), to_qblocks(qi), to_qblocks(wi), pos[N_META:].reshape(n_blk, QBLOCK)))
    att = jnp.concatenate([att_meta, att_real.swapaxes(0, 1).reshape(B, n_real, ATT_W)], axis=1)

    S0 = jnp.zeros((B, N_REC_HEADS, REC_KEY_DIM, REC_VAL_DIM), jnp.float32)
    o_meta, S = gla_chunk(S0, rq[:, :N_META], rk[:, :N_META], rv[:, :N_META], rlogf[:, :N_META])
    n_chunk = n_real // REC_CHUNK

    def to_chunks(a):
        a = a[:, N_META:]
        return a.reshape((B, n_chunk, REC_CHUNK) + a.shape[2:]).swapaxes(0, 1)

    def step(carry, xs):
        o, carry = gla_chunk(carry, *xs)
        return carry, o

    S, o_real = lax.scan(step, S, (to_chunks(rq), to_chunks(rk), to_chunks(rv), to_chunks(rlogf)))
    rec_o = jnp.concatenate([o_meta, o_real.swapaxes(0, 1).reshape(B, n_real, N_REC_HEADS, REC_VAL_DIM)], axis=1)

    y = mixer_back(x, att, rec_o, gates, g_rec, w_pa_l, w_pb_l, w_o_l)
    return y, k, v, ki, S


def sample_mixer(x, cache_k, cache_v, cache_idx_k, state_l, page_table, layer,
                 w_in_l, g_mix, g_q, g_k, lb, g_rec, w_pa_l, w_pb_l, w_o_l):
    DB, Tq, _ = x.shape
    past_len = page_table.shape[1] * PAGE_SIZE
    pos = past_len + jnp.arange(Tq, dtype=jnp.int32)
    (q, k, v, qi, ki, wi), (rq, rk, rv, rlogf), gates = mixer_front(x, pos, w_in_l, g_mix, g_q, g_k, lb)

    past_ki = cache_idx_k[layer, page_table].reshape(DB, past_len, IDX_DIM).astype(ki.dtype)
    all_ki = jnp.concatenate([past_ki, ki], axis=1)
    L = past_len + Tq
    sc = indexer_scores(qi, wi, all_ki)
    sc = jnp.where(jnp.arange(L, dtype=jnp.int32)[None, None, :] <= pos[None, :, None], sc, -jnp.inf)
    _, idx = lax.top_k(sc, min(TOPK_MAX, L // 4))

    is_past = idx < past_len
    pidx = jnp.minimum(idx, past_len - 1)
    phys = jax.vmap(lambda pt, i: pt[i])(page_table, pidx // PAGE_SIZE)
    off = pidx % PAGE_SIZE
    nidx = jnp.clip(idx - past_len, 0, Tq - 1)
    k_sel = jnp.where(is_past[..., None, None], cache_k[layer, phys, off].astype(k.dtype), take_rows(k, nidx))
    v_sel = jnp.where(is_past[..., None, None], cache_v[layer, phys, off].astype(v.dtype), take_rows(v, nidx))
    att = attend_selected(q, k_sel, v_sel, idx <= pos[None, :, None])

    rec_o, S = gla_chunk(state_l, rq, rk, rv, rlogf)

    y = mixer_back(x, att, rec_o, gates, g_rec, w_pa_l, w_pb_l, w_o_l)
    return y, k, v, ki, S


def setup_inputs(seed: int = 0) -> dict:
    key = jax.random.key(seed)
    ks = jax.random.split(key, 24)
    f32 = jnp.float32
    n_pages = PAST_LEN // PAGE_SIZE
    n_pool = (DEC_BATCH * n_pages * 5) // 4

    def nrm(k, shape, scale):
        return jax.random.normal(k, shape, f32) * scale

    def gain(k, shape):
        return 1.0 + 0.01 * jax.random.normal(k, shape, f32)

    page_table = jax.random.permutation(ks[7], n_pool)[:DEC_BATCH * n_pages].reshape(DEC_BATCH, n_pages).astype(jnp.int32)
    return {
        "x_prompt": nrm(ks[0], (BATCH, SEQ, D_MODEL), 1.0),
        "x_sample": nrm(ks[1], (DEC_BATCH, DEC_SEQ, D_MODEL), 1.0),
        "cache_k": nrm(ks[2], (DEPTH, n_pool, PAGE_SIZE, N_KV_HEADS, HEAD_DIM), 1.0),
        "cache_v": nrm(ks[3], (DEPTH, n_pool, PAGE_SIZE, N_KV_HEADS, HEAD_DIM), 1.0),
        "cache_idx_k": nrm(ks[4], (DEPTH, n_pool, PAGE_SIZE, IDX_DIM), 1.0),
        "state_rec": nrm(ks[5], (DEPTH, DEC_BATCH, N_REC_HEADS, REC_KEY_DIM, REC_VAL_DIM), 0.5),
        "page_table": page_table,
        "meta_tokens": nrm(ks[6], (N_META, D_MODEL), 1.0),
        "w_in": nrm(ks[8], (DEPTH, D_MODEL, D_IN), D_MODEL ** -0.5),
        "norm_mix": gain(ks[9], (DEPTH, D_MODEL)),
        "q_norm": gain(ks[10], (DEPTH, HEAD_DIM)),
        "k_norm": gain(ks[11], (DEPTH, HEAD_DIM)),
        "lb_raw": nrm(ks[12], (DEPTH, REC_W), 0.1),
        "rec_norm": gain(ks[13], (DEPTH, REC_VAL_DIM)),
        "w_pa": nrm(ks[14], (DEPTH, ATT_W, D_MODEL), ATT_W ** -0.5),
        "w_pb": nrm(ks[15], (DEPTH, REC_V, D_MODEL), REC_V ** -0.5),
        "w_o": nrm(ks[16], (DEPTH, D_MODEL, D_MODEL), D_MODEL ** -0.5),
        "norm_ffn": gain(ks[17], (DEPTH, D_MODEL)),
        "w_gu": nrm(ks[18], (DEPTH, D_MODEL, 2 * D_FF), D_MODEL ** -0.5),
        "w_down": nrm(ks[19], (DEPTH, D_FF, D_MODEL), D_FF ** -0.5),
    }


def reference(x_prompt, x_sample, cache_k, cache_v, cache_idx_k, state_rec, page_table, meta_tokens,
              w_in, norm_mix, q_norm, k_norm, lb_raw, rec_norm, w_pa, w_pb, w_o, norm_ffn, w_gu, w_down):
    sm = jax.nn.softmax(lb_raw.astype(jnp.float32), axis=0)
    lower_bounds = jnp.cumsum(sm, axis=0) - sm[0:1]

    B = x_prompt.shape[0]
    xp = jnp.concatenate([jnp.broadcast_to(meta_tokens[None].astype(x_prompt.dtype), (B, N_META, D_MODEL)), x_prompt], axis=1)
    xs = x_sample
    pk, pv, pki, ps = [], [], [], []
    sk, sv, ski, ss = [], [], [], []
    for l in range(DEPTH):
        front = (w_in[l], norm_mix[l], q_norm[l], k_norm[l], lower_bounds[l])
        back = (rec_norm[l], w_pa[l], w_pb[l], w_o[l])
        xp, k_, v_, ki_, S_ = prompt_mixer(xp, *front, *back)
        xp = swiglu_ffn(xp, norm_ffn[l], w_gu[l], w_down[l])
        pk.append(k_); pv.append(v_); pki.append(ki_); ps.append(S_)
        xs, k_, v_, ki_, S_ = sample_mixer(xs, cache_k, cache_v, cache_idx_k, state_rec[l], page_table, l, *front, *back)
        xs = swiglu_ffn(xs, norm_ffn[l], w_gu[l], w_down[l])
        sk.append(k_); sv.append(v_); ski.append(ki_); ss.append(S_)

    y_prompt = xp[:, N_META:]
    y_sample = xs
    return (y_prompt, y_sample, jnp.stack(pk), jnp.stack(pv), jnp.stack(pki), jnp.stack(ps),
            jnp.stack(sk), jnp.stack(sv), jnp.stack(ski), jnp.stack(ss))
```

```python
import functools

import jax
import jax.numpy as jnp
from jax import lax
from jax.experimental import pallas as pl
from jax.experimental.pallas import tpu as pltpu

F32 = jnp.float32
BF16 = jnp.bfloat16

N_META = 16
N_HEADS = 8
HEAD_DIM = 64
N_KV_HEADS = 4
ROT_DIM = 16
ROPE_THETA = 500000.0
N_IDX_HEADS = 8
IDX_DIM = 64
IDX_W_SCALE = (N_IDX_HEADS * IDX_DIM) ** -0.5
TOPK_MAX = 256
N_REC_HEADS = 4
REC_DIM = 128
EPS = 1e-6

LANES = 128
SUBLANES = 8

TOK_TILE = 384
Q_TILE = 128
REC_CHUNK = 64
PAGES_PER_STEP = 8
FFN_CHUNK = 256
VMEM_LIMIT = 56 * 1024 * 1024

NEG_SCORE = -3.0e38
NEG_HALF = -1.5e38
NEG_BIAS = -1.0e30
BIG = 3.0e38
MAX_BISECTIONS = 2200

C_Q, C_K, C_V, C_QI, C_KIWI, C_RQ, C_FR, C_IR, C_GR, C_GA, C_GB, C_END = (
    0, 512, 768, 1024, 1536, 1664, 2176, 2688, 3200, 3712, 4736, 5760)

_NT = (((1,), (1,)), ((), ()))
_TN = (((0,), (0,)), ((), ()))


def _sigmoid(x):
    return 1.0 / (1.0 + jnp.exp(-x))


def _split3(x):
    hi = x.astype(BF16)
    r1 = x - hi.astype(F32)
    mid = r1.astype(BF16)
    lo = (r1 - mid.astype(F32)).astype(BF16)
    return jnp.concatenate([hi, mid, lo], axis=1)


def _join3(r):
    return r[:, 0:LANES] + r[:, LANES:2 * LANES] + r[:, 2 * LANES:3 * LANES]


def _inproj_body(x_ref, g_ref, w_ref, rope_ref, qn_ref, kn_ref, lb_ref,
                 q_ref, kf_ref, vf_ref, kb_ref, vb_ref, qi_ref, kiwi_ref, kk_ref,
                 rq_ref, f_ref, rv_ref, gr_ref, ga_ref, gb_ref):
    x = x_ref[...]
    h = x * lax.rsqrt(jnp.mean(x * x, axis=-1, keepdims=True) + EPS) * g_ref[...]
    hb = h.astype(BF16)
    rows = x.shape[0]
    lane = lax.broadcasted_iota(jnp.int32, (rows, LANES), 1)
    low = lane < HEAD_DIM
    cosv = rope_ref[:, 0:LANES]
    sin_up = rope_ref[:, LANES:2 * LANES]
    sin_dn = rope_ref[:, 2 * LANES:3 * LANES]

    def proj(c0, c1):
        return jnp.dot(hb, w_ref[:, c0:c1], preferred_element_type=F32)

    def rope(xs):
        return (xs * cosv + pltpu.roll(xs, ROT_DIM // 2, 1) * sin_up
                + pltpu.roll(xs, LANES - ROT_DIM // 2, 1) * sin_dn)

    def headnorm(xs, gain):
        sq = xs * xs
        s_lo = jnp.sum(jnp.where(low, sq, 0.0), axis=-1, keepdims=True)
        s_hi = jnp.sum(jnp.where(low, 0.0, sq), axis=-1, keepdims=True)
        ms = jnp.where(low, s_lo, s_hi) * (1.0 / HEAD_DIM)
        return xs * lax.rsqrt(ms + EPS) * gain

    qraw = proj(C_Q, C_K)
    for p in range(N_KV_HEADS):
        qs = rope(headnorm(qraw[:, p * LANES:(p + 1) * LANES], qn_ref[...])) * (HEAD_DIM ** -0.5)
        qr = pltpu.roll(qs, HEAD_DIM, 1)
        if p % 2 == 0:
            h0, h1 = jnp.where(low, qs, 0.0), jnp.where(low, qr, 0.0)
        else:
            h0, h1 = jnp.where(low, 0.0, qr), jnp.where(low, 0.0, qs)
        q_ref[:, (2 * p) * LANES:(2 * p + 1) * LANES] = h0.astype(BF16)
        q_ref[:, (2 * p + 1) * LANES:(2 * p + 2) * LANES] = h1.astype(BF16)

    kraw = proj(C_K, C_V)
    for s in range(2):
        ks = rope(headnorm(kraw[:, s * LANES:(s + 1) * LANES], kn_ref[...]))
        kf_ref[:, s * LANES:(s + 1) * LANES] = ks
        kb_ref[:, s * LANES:(s + 1) * LANES] = ks.astype(BF16)

    vraw = proj(C_V, C_QI)
    vf_ref[...] = vraw
    vb_ref[...] = vraw.astype(BF16)

    qiraw = proj(C_QI, C_KIWI)
    for p in range(4):
        qi_ref[:, p * LANES:(p + 1) * LANES] = rope(qiraw[:, p * LANES:(p + 1) * LANES]).astype(BF16)

    kiwi = proj(C_KIWI, C_RQ)
    kir = rope(kiwi)
    kiwi_ref[...] = jnp.where(low, kir, kiwi * IDX_W_SCALE)
    kk_ref[...] = jnp.where(low, kir, pltpu.roll(kir, HEAD_DIM, 1)).astype(BF16)

    qr_ = proj(C_RQ, C_FR)
    rq_ref[...] = qr_ * _sigmoid(qr_)
    lb = lb_ref[...]
    f_ref[...] = lb + (1.0 - lb) * _sigmoid(proj(C_FR, C_IR))
    rv_ref[...] = proj(C_IR, C_GR)
    g_ = proj(C_GR, C_GA)
    gr_ref[...] = g_ * _sigmoid(g_)
    ga_ref[...] = _sigmoid(proj(C_GA, C_GB))
    gb_ref[...] = _sigmoid(proj(C_GB, C_END))


def _inproj(x_all, g, w, rope_tab, qn, kn, lb, tiles_per_seq, n_prompt_tiles):
    n_rows, d = x_all.shape
    n_tiles = n_rows // TOK_TILE
    row = lambda i: (i, 0)
    const = lambda i: (0, 0)
    rope_map = lambda i: (jnp.where(i < n_prompt_tiles, i % tiles_per_seq, tiles_per_seq), 0)
    widths = dict(q=(1024, BF16), kf=(256, F32), vf=(256, F32), kb=(256, BF16), vb=(256, BF16),
                  qi=(512, BF16), kiwi=(128, F32), kk=(128, BF16), rq=(512, F32), f=(512, F32),
                  rv=(512, F32), gr=(512, F32), ga=(1024, F32), gb=(1024, F32))
    out_shape = [jax.ShapeDtypeStruct((n_rows, wd), dt) for wd, dt in widths.values()]
    out_specs = [pl.BlockSpec((TOK_TILE, wd), row) for wd, _ in widths.values()]
    outs = pl.pallas_call(
        _inproj_body,
        grid=(n_tiles,),
        in_specs=[
            pl.BlockSpec((TOK_TILE, d), row),
            pl.BlockSpec((1, d), const),
            pl.BlockSpec((d, C_END), const, pipeline_mode=pl.Buffered(1)),
            pl.BlockSpec((TOK_TILE, 3 * LANES), rope_map),
            pl.BlockSpec((1, LANES), const),
            pl.BlockSpec((1, LANES), const),
            pl.BlockSpec((1, 512), const),
        ],
        out_specs=out_specs,
        out_shape=out_shape,
        compiler_params=pltpu.CompilerParams(
            dimension_semantics=("arbitrary",), vmem_limit_bytes=VMEM_LIMIT),
        name="inproj",
    )(x_all, g, w, rope_tab, qn, kn, lb)
    return dict(zip(widths.keys(), outs))


def _kth_largest_rows(sc_ref, nchunks, top_k):
    rows = sc_ref.shape[1]

    def minmax(c, carry):
        mn, mx = carry
        s = sc_ref[c]
        return jnp.minimum(mn, jnp.where(s > NEG_HALF, s, BIG)), jnp.maximum(mx, s)

    mn, mx = lax.fori_loop(0, nchunks, minmax,
                           (jnp.full((rows, LANES), BIG, F32), jnp.full((rows, LANES), -BIG, F32)))
    lo0 = jnp.min(mn, axis=1, keepdims=True)
    top = jnp.max(mx, axis=1, keepdims=True)
    hi0 = top + (jnp.abs(top) + 1.0)

    def cond(st):
        return (jnp.min(st[2]) < 0.5) & (st[3] < MAX_BISECTIONS)

    def body(st):
        lo, hi, _, it = st
        mid = 0.5 * (lo + hi)

        def cnt(c, acc):
            return acc + jnp.where(sc_ref[c] >= mid, 1.0, 0.0)

        acc = lax.fori_loop(0, nchunks, cnt, jnp.zeros((rows, LANES), F32))
        ge = jnp.sum(acc, axis=1, keepdims=True) >= top_k
        done = jnp.where((mid <= lo) | (mid >= hi), 1.0, 0.0)
        return jnp.where(ge, mid, lo), jnp.where(ge, hi, mid), done, it + 1

    lo, _, _, _ = lax.while_loop(cond, body, (lo0, hi0, jnp.zeros((rows, 1), F32), jnp.int32(0)))
    return lo


def _attn_body(q_ref, qi_ref, wq_ref, kb_ref, vb_ref, kk_ref, o_ref,
               sc_ref, wb_ref, qia_ref, qa_ref, m_ref, l_ref, acc_ref, *, top_k):
    j = pl.program_id(1)
    nchunks = j + 1
    T = Q_TILE
    lane = lax.broadcasted_iota(jnp.int32, (T, LANES), 1)
    sub = lax.broadcasted_iota(jnp.int32, (T, LANES), 0)
    low = lane < HEAD_DIM
    qpos = j * T + sub

    wq = wq_ref[...]
    qi = qi_ref[...]
    for h in range(N_IDX_HEADS):
        wb_ref[h] = jnp.broadcast_to(wq[:, IDX_DIM + h:IDX_DIM + h + 1], (T, LANES))
        slab = qi[:, (h // 2) * LANES:(h // 2 + 1) * LANES]
        zero = jnp.zeros_like(slab)
        qia_ref[h * T:(h + 1) * T, :] = jnp.where(low, slab, zero) if h % 2 == 0 else jnp.where(low, zero, slab)
        qa_ref[h * T:(h + 1) * T, :] = q_ref[:, h * LANES:(h + 1) * LANES]

    def phase_a(c, carry):
        off = pl.multiple_of(c * T, T)
        d = lax.dot_general(qia_ref[...], kk_ref[pl.ds(off, T), :], _NT, preferred_element_type=F32)
        s = jnp.zeros((T, LANES), F32)
        for h in range(N_IDX_HEADS):
            s = s + wb_ref[h] * jnp.maximum(d[h * T:(h + 1) * T], 0.0)
        sc_ref[c] = jnp.where(c * T + lane <= qpos, s, NEG_SCORE)
        return carry

    lax.fori_loop(0, nchunks, phase_a, 0)

    kth = _kth_largest_rows(sc_ref, nchunks, top_k)
    n_adm = j * T + lax.broadcasted_iota(jnp.int32, (T, 1), 0) + 1
    thr = jnp.where(n_adm > top_k, kth, NEG_SCORE)

    def cnt_gt(c, acc):
        return acc + jnp.where(sc_ref[c] > thr, 1.0, 0.0)

    n_gt = jnp.sum(lax.fori_loop(0, nchunks, cnt_gt, jnp.zeros((T, LANES), F32)), axis=1, keepdims=True)
    n_tie = top_k - n_gt

    upper = jnp.where(sub < lane, 1.0, 0.0).astype(BF16)
    ones = jnp.ones((LANES, LANES), BF16)

    def to_bias(c, seen):
        s = sc_ref[c]
        eq = s == thr
        eqb = jnp.where(eq, 1.0, 0.0).astype(BF16)
        rank = jnp.dot(eqb, upper, preferred_element_type=F32) + seen
        keep_tie = jnp.where(eq, jnp.where(rank < n_tie, 0.0, NEG_BIAS), NEG_BIAS)
        sc_ref[c] = jnp.where(s > NEG_HALF, jnp.where(s > thr, 0.0, keep_tie), NEG_BIAS)
        return seen + jnp.dot(eqb, ones, preferred_element_type=F32)

    lax.fori_loop(0, nchunks, to_bias, jnp.zeros((T, LANES), F32))

    m_ref[...] = jnp.full(m_ref.shape, NEG_SCORE, F32)
    l_ref[...] = jnp.zeros(l_ref.shape, F32)
    acc_ref[...] = jnp.zeros(acc_ref.shape, F32)

    def phase_c(c, carry):
        off = pl.multiple_of(c * T, T)
        bias = sc_ref[c]
        for s in range(2):
            kc = kb_ref[pl.ds(off, T), s * LANES:(s + 1) * LANES]
            vc = vb_ref[pl.ds(off, T), s * LANES:(s + 1) * LANES]
            sq = lax.dot_general(qa_ref[s * 4 * T:(s + 1) * 4 * T, :], kc, _NT, preferred_element_type=F32)
            s3 = sq.reshape(4, T, LANES) + bias[None]
            m_old = m_ref[s]
            m_new = jnp.maximum(m_old, jnp.max(s3, axis=-1, keepdims=True))
            alpha = jnp.exp(m_old - m_new)
            p = jnp.exp(s3 - m_new)
            l_ref[s] = alpha * l_ref[s] + jnp.sum(p, axis=-1, keepdims=True)
            m_ref[s] = m_new
            pv = jnp.dot(p.reshape(4 * T, LANES).astype(BF16), vc, preferred_element_type=F32)
            acc_ref[s] = alpha * acc_ref[s] + pv.reshape(4, T, LANES)
        return carry

    lax.fori_loop(0, nchunks, phase_c, 0)

    for p in range(N_KV_HEADS):
        s, hh = (2 * p) // 4, (2 * p) % 4
        a0 = acc_ref[s, hh] / l_ref[s, hh]
        a1 = acc_ref[s, hh + 1] / l_ref[s, hh + 1]
        if p % 2 == 0:
            slab = jnp.where(low, a0, pltpu.roll(a1, HEAD_DIM, 1))
        else:
            slab = jnp.where(low, pltpu.roll(a0, HEAD_DIM, 1), a1)
        o_ref[:, p * LANES:(p + 1) * LANES] = slab.astype(BF16)


def _attn_prompt(pj, batch, t_pad, top_k):
    nq = t_pad // Q_TILE
    qrow = lambda b, j: (b * nq + j, 0)
    seq = lambda b, j: (b, 0)
    return pl.pallas_call(
        functools.partial(_attn_body, top_k=top_k),
        grid=(batch, nq),
        in_specs=[
            pl.BlockSpec((Q_TILE, 1024), qrow),
            pl.BlockSpec((Q_TILE, 512), qrow),
            pl.BlockSpec((Q_TILE, LANES), qrow),
            pl.BlockSpec((t_pad, 256), seq),
            pl.BlockSpec((t_pad, 256), seq),
            pl.BlockSpec((t_pad, LANES), seq),
        ],
        out_specs=pl.BlockSpec((Q_TILE, 512), qrow),
        out_shape=jax.ShapeDtypeStruct((batch * t_pad, 512), BF16),
        scratch_shapes=[
            pltpu.VMEM((nq, Q_TILE, LANES), F32),
            pltpu.VMEM((N_IDX_HEADS, Q_TILE, LANES), F32),
            pltpu.VMEM((N_IDX_HEADS * Q_TILE, LANES), BF16),
            pltpu.VMEM((N_HEADS * Q_TILE, LANES), BF16),
            pltpu.VMEM((2, 4, Q_TILE, LANES), F32),
            pltpu.VMEM((2, 4, Q_TILE, LANES), F32),
            pltpu.VMEM((2, 4, Q_TILE, LANES), F32),
        ],
        compiler_params=pltpu.CompilerParams(
            dimension_semantics=("arbitrary", "arbitrary"), vmem_limit_bytes=VMEM_LIMIT),
        name="attn_prompt",
    )(pj["q"], pj["qi"], pj["kiwi"], pj["kb"], pj["vb"], pj["kk"])


def _gla_body(rq_ref, f_ref, rv_ref, tri_ref, gsel_ref, o_ref, sout_ref, st_ref, *, t_real):
    c = pl.program_id(1)
    n_c = pl.num_programs(1)
    C = REC_CHUNK
    n_lev = C.bit_length() - 1

    @pl.when(c == 0)
    def _():
        st_ref[...] = jnp.zeros(st_ref.shape, F32)

    trow = lax.broadcasted_iota(jnp.int32, (C, LANES), 0)
    valid = c * C + trow < t_real
    ti = lax.broadcasted_iota(jnp.int32, (C, C), 0)
    si = lax.broadcasted_iota(jnp.int32, (C, C), 1)
    tri = tri_ref[...]
    gsel = gsel_ref[...]

    for h in range(N_REC_HEADS):
        sl = slice(h * LANES, (h + 1) * LANES)
        f = f_ref[:, sl]
        q = rq_ref[:, sl]
        vb = rv_ref[:, sl].astype(BF16)
        lf = jnp.where(valid, jnp.log(f), 0.0)
        kk = jnp.where(valid, 1.0 - f, 0.0)
        b = _join3(jnp.dot(tri, _split3(lf), preferred_element_type=F32))
        bnd = _join3(jnp.dot(gsel, _split3(b), preferred_element_type=F32))

        a = jnp.where(ti == si, lax.dot_general(q.astype(BF16), kk.astype(BF16), _NT,
                                                preferred_element_type=F32), 0.0)
        for lev in range(n_lev):
            half = 1 << lev
            up = (trow & half) != 0
            bl = bnd[lev * C:(lev + 1) * C]
            e = jnp.exp(jnp.where(up, b - bl, bl - b))
            qt = jnp.where(up, q * e, 0.0).astype(BF16)
            kt = jnp.where(up, 0.0, kk * e).astype(BF16)
            pblk = lax.dot_general(qt, kt, _NT, preferred_element_type=F32)
            same = (ti >> (lev + 1)) == (si >> (lev + 1))
            a = a + jnp.where(same & ((ti & half) != 0) & ((si & half) == 0), pblk, 0.0)

        st = st_ref[h]
        qe = (q * jnp.exp(b)).astype(BF16)
        o = jnp.dot(a.astype(BF16), vb, preferred_element_type=F32)
        o = o + lax.dot_general(qe, st.astype(BF16), _NT, preferred_element_type=F32)
        o_ref[:, sl] = o

        b_last = b[C - 1:C, :]
        kd = (kk * jnp.exp(b_last - b)).astype(BF16)
        st_new = st * jnp.exp(b_last) + lax.dot_general(vb, kd, _TN, preferred_element_type=F32)
        st_ref[h] = st_new

        @pl.when(c == n_c - 1)
        def _():
            sout_ref[0, h] = st_new.T


def _gla_consts():
    C = REC_CHUNK
    n_lev = C.bit_length() - 1
    t = jnp.arange(C)
    tri = (t[None, :] <= t[:, None]).astype(BF16)
    sel = []
    for lev in range(n_lev):
        half = 1 << lev
        bnd_row = (t >> (lev + 1)) * (2 * half) + half - 1
        sel.append((t[None, :] == bnd_row[:, None]).astype(BF16))
    return tri, jnp.concatenate(sel, axis=0)


def _gla_prompt(pj, batch, t_pad, t_real):
    nch = t_pad // REC_CHUNK
    tri, gsel = _gla_consts()
    row = lambda b, c: (b * nch + c, 0)
    const = lambda b, c: (0, 0)
    return pl.pallas_call(
        functools.partial(_gla_body, t_real=t_real),
        grid=(batch, nch),
        in_specs=[
            pl.BlockSpec((REC_CHUNK, 512), row),
            pl.BlockSpec((REC_CHUNK, 512), row),
            pl.BlockSpec((REC_CHUNK, 512), row),
            pl.BlockSpec(tri.shape, const),
            pl.BlockSpec(gsel.shape, const),
        ],
        out_specs=[
            pl.BlockSpec((REC_CHUNK, 512), row),
            pl.BlockSpec((1, N_REC_HEADS, REC_DIM, REC_DIM), lambda b, c: (b, 0, 0, 0)),
        ],
        out_shape=[
            jax.ShapeDtypeStruct((batch * t_pad, 512), F32),
            jax.ShapeDtypeStruct((batch, N_REC_HEADS, REC_DIM, REC_DIM), F32),
        ],
        scratch_shapes=[pltpu.VMEM((N_REC_HEADS, REC_DIM, REC_DIM), F32)],
        compiler_params=pltpu.CompilerParams(
            dimension_semantics=("arbitrary", "arbitrary"), vmem_limit_bytes=VMEM_LIMIT),
        name="gla_prompt",
    )(pj["rq"], pj["f"], pj["rv"], tri, gsel)


def _gla_sample_body(rq_ref, f_ref, rv_ref, s0_ref, o_ref, sout_ref):
    b = pl.program_id(0)
    q = rq_ref[pl.ds(b, 1), :]
    f = f_ref[pl.ds(b, 1), :]
    v = rv_ref[pl.ds(b, 1), :]
    for h in range(N_REC_HEADS):
        sl = slice(h * LANES, (h + 1) * LANES)
        col = lambda r: jnp.broadcast_to(r[:, sl], (REC_DIM, LANES)).T
        s_new = col(f) * s0_ref[0, h] + col(1.0 - f) * v[:, sl]
        sout_ref[0, h] = s_new
        o_ref[0, :, sl] = jnp.sum(col(q) * s_new, axis=0, keepdims=True)


def _gla_sample(rq, f, rv, state):
    db = state.shape[0]
    full = lambda b: (0, 0)
    st = lambda b: (b, 0, 0, 0)
    o, s = pl.pallas_call(
        _gla_sample_body,
        grid=(db,),
        in_specs=[
            pl.BlockSpec(rq.shape, full), pl.BlockSpec(f.shape, full), pl.BlockSpec(rv.shape, full),
            pl.BlockSpec((1, N_REC_HEADS, REC_DIM, REC_DIM), st),
        ],
        out_specs=[
            pl.BlockSpec((1, 1, 512), lambda b: (b, 0, 0)),
            pl.BlockSpec((1, N_REC_HEADS, REC_DIM, REC_DIM), st),
        ],
        out_shape=[
            jax.ShapeDtypeStruct((db, 1, 512), F32),
            jax.ShapeDtypeStruct(state.shape, F32),
        ],
        compiler_params=pltpu.CompilerParams(dimension_semantics=("arbitrary",)),
        name="gla_sample",
    )(rq, f, rv, state)
    return o.reshape(db, 512), s


def _attn_sample_body(pt_ref, *refs, n_pages, top_k):
    P = PAGES_PER_STEP
    ki_refs, k_refs, v_refs = refs[0:P], refs[P:2 * P], refs[2 * P:3 * P]
    (qi_ref, w_ref, kin_ref, q_ref, kn_ref, vn_ref, exp_ref,
     o_ref, sc_ref, m_ref, l_ref, acc_ref) = refs[3 * P:]
    g = pl.program_id(1)
    n_groups = n_pages // P
    n_rows = sc_ref.shape[0]
    qi = qi_ref[0]
    w = w_ref[0]
    lane1 = lax.broadcasted_iota(jnp.int32, (1, LANES), 1)

    @pl.when(g < n_groups)
    def _():
        for i in range(P):
            d = lax.dot_general(qi, ki_refs[i][...].astype(BF16), _NT, preferred_element_type=F32)
            sc_ref[pl.ds(g * P + i, 1), :] = jnp.sum(w * jnp.maximum(d, 0.0), axis=0, keepdims=True)

    @pl.when(g == n_groups - 1)
    def _():
        d_new = jnp.sum(qi.astype(F32) * kin_ref[0].astype(F32), axis=1, keepdims=True)
        s_new = jnp.sum(w * jnp.maximum(d_new, 0.0), axis=0, keepdims=True)
        tail = lax.broadcasted_iota(jnp.int32, (n_rows - n_pages, LANES), 0) * LANES + \
            lax.broadcasted_iota(jnp.int32, (n_rows - n_pages, LANES), 1)
        sc_ref[n_pages:n_rows, :] = jnp.where(tail == 0, s_new, NEG_SCORE)

        s = sc_ref[...]
        adm = s > NEG_HALF
        lo0 = jnp.min(jnp.where(adm, s, BIG), keepdims=True)
        top = jnp.max(s, keepdims=True)
        hi0 = top + (jnp.abs(top) + 1.0)

        def cond(st):
            return (jnp.min(st[2]) < 0.5) & (st[3] < MAX_BISECTIONS)

        def body(st):
            lo, hi, _, it = st
            mid = 0.5 * (lo + hi)
            ge = jnp.sum(jnp.where(sc_ref[...] >= mid, 1.0, 0.0), keepdims=True) >= top_k
            done = jnp.where((mid <= lo) | (mid >= hi), 1.0, 0.0)
            return jnp.where(ge, mid, lo), jnp.where(ge, hi, mid), done, it + 1

        thr, _, _, _ = lax.while_loop(cond, body, (lo0, hi0, jnp.zeros((1, 1), F32), jnp.int32(0)))
        n_tie = top_k - jnp.sum(jnp.where(s > thr, 1.0, 0.0), keepdims=True)
        eqb = jnp.where(s == thr, 1.0, 0.0).astype(BF16)
        ri = lax.broadcasted_iota(jnp.int32, (LANES, LANES), 0)
        ci = lax.broadcasted_iota(jnp.int32, (LANES, LANES), 1)
        in_row = jnp.dot(eqb, jnp.where(ri < ci, 1.0, 0.0).astype(BF16), preferred_element_type=F32)
        row_tot = jnp.dot(eqb, jnp.ones((LANES, LANES), BF16), preferred_element_type=F32)
        rr = lax.broadcasted_iota(jnp.int32, (n_rows, n_rows), 0)
        rc = lax.broadcasted_iota(jnp.int32, (n_rows, n_rows), 1)
        before = jnp.dot(jnp.where(rc < rr, 1.0, 0.0).astype(BF16), row_tot.astype(BF16),
                         preferred_element_type=F32)
        rank = in_row + before
        keep = (s > thr) | ((s == thr) & (rank < n_tie))
        sc_ref[...] = jnp.where(keep & adm, 1.0, 0.0)

        q = q_ref[0].astype(F32)
        s_own = jnp.sum(q * kn_ref[0].astype(F32), axis=1, keepdims=True)
        own_sel = sc_ref[n_pages:n_pages + 1, 0:1]
        m_ref[...] = jnp.broadcast_to(s_own + jnp.where(own_sel > 0.5, 0.0, NEG_BIAS), m_ref.shape)
        l_ref[...] = jnp.ones(l_ref.shape, F32)
        acc_ref[...] = vn_ref[0].astype(F32)

    @pl.when(g >= n_groups)
    def _():
        q = q_ref[0]
        hrow = lax.broadcasted_iota(jnp.int32, (N_HEADS, 4 * LANES), 0)
        hcol = lax.broadcasted_iota(jnp.int32, (N_HEADS, 4 * LANES), 1)
        mine = (hcol & 3) == (hrow >> 1)
        for i in range(P):
            page = (g - n_groups) * P + i
            sel = jnp.broadcast_to(sc_ref[pl.ds(page, 1), :], (N_HEADS, LANES)).astype(BF16)
            sel4 = jnp.dot(sel, exp_ref[...], preferred_element_type=F32)
            s = lax.dot_general(q, k_refs[i][...].astype(BF16), _NT, preferred_element_type=F32)
            s = jnp.where(mine & (sel4 > 0.5), s, NEG_BIAS)
            m_old = m_ref[...]
            m_new = jnp.maximum(m_old, jnp.max(s, axis=1, keepdims=True))
            alpha = jnp.exp(m_old - m_new)
            p = jnp.exp(s - m_new[:, 0:1])
            l_ref[...] = alpha * l_ref[...] + jnp.sum(p, axis=1, keepdims=True)
            m_ref[...] = m_new
            acc_ref[...] = alpha[:, 0:HEAD_DIM] * acc_ref[...] + jnp.dot(
                p.astype(BF16), v_refs[i][...].astype(BF16), preferred_element_type=F32)

        @pl.when(g == 2 * n_groups - 1)
        def _():
            o_ref[0] = acc_ref[...] / l_ref[:, 0:HEAD_DIM]


def _attn_sample(page_table, layer, cache_ki, cache_k2, cache_v2, qi, w, ki_new, q, k_new, v_new):
    db, n_pages = page_table.shape
    P = PAGES_PER_STEP
    n_groups = n_pages // P
    page = cache_ki.shape[2]
    top_k = min(TOPK_MAX, (n_pages * page + 1) // 4)
    n_rows = -(-(n_pages + 1) // SUBLANES) * SUBLANES
    key = jnp.arange(page)
    expand = (jnp.arange(4 * page)[None, :] // 4 == key[:, None]).astype(BF16)

    def ki_map(i):
        return lambda b, g, pt: (layer, pt[b * n_pages + jnp.minimum(g, n_groups - 1) * P + i], 0, 0)

    def kv_map(i):
        return lambda b, g, pt: (layer, pt[b * n_pages + jnp.maximum(g - n_groups, 0) * P + i], 0, 0)

    per_b = lambda b, g, pt: (b, 0, 0)
    in_specs = (
        [pl.BlockSpec((None, None, page, IDX_DIM), ki_map(i)) for i in range(P)]
        + [pl.BlockSpec((None, None, 4 * page, HEAD_DIM), kv_map(i)) for i in range(P)]
        + [pl.BlockSpec((None, None, 4 * page, HEAD_DIM), kv_map(i)) for i in range(P)]
        + [pl.BlockSpec((1,) + a.shape[1:], per_b) for a in (qi, w, ki_new, q, k_new, v_new)]
        + [pl.BlockSpec(expand.shape, lambda b, g, pt: (0, 0))])
    out = pl.pallas_call(
        functools.partial(_attn_sample_body, n_pages=n_pages, top_k=top_k),
        grid_spec=pltpu.PrefetchScalarGridSpec(
            num_scalar_prefetch=1,
            grid=(db, 2 * n_groups),
            in_specs=in_specs,
            out_specs=pl.BlockSpec((1, N_HEADS, HEAD_DIM), per_b),
            scratch_shapes=[
                pltpu.VMEM((n_rows, LANES), F32),
                pltpu.VMEM((N_HEADS, LANES), F32),
                pltpu.VMEM((N_HEADS, LANES), F32),
                pltpu.VMEM((N_HEADS, HEAD_DIM), F32),
            ]),
        out_shape=jax.ShapeDtypeStruct((db, N_HEADS, HEAD_DIM), F32),
        compiler_params=pltpu.CompilerParams(
            dimension_semantics=("arbitrary", "arbitrary"), vmem_limit_bytes=VMEM_LIMIT),
        name="attn_sample",
    )(page_table.reshape(-1), *([cache_ki] * P), *([cache_k2] * P), *([cache_v2] * P),
      qi, w, ki_new, q, k_new, v_new, expand)
    return out.reshape(db, N_HEADS * HEAD_DIM)


def _back_body(x_ref, att_ref, ro_ref, gr_ref, ga_ref, gb_ref, grec_ref, wpa_ref, wpb_ref, wo_ref, y_ref):
    ro = ro_ref[...]
    gr = gr_ref[...]
    recs = []
    for h in range(N_REC_HEADS):
        r = ro[:, h * LANES:(h + 1) * LANES]
        r = r * lax.rsqrt(jnp.mean(r * r, axis=-1, keepdims=True) + EPS) * grec_ref[...]
        recs.append((r * gr[:, h * LANES:(h + 1) * LANES]).astype(BF16))
    rec = jnp.concatenate(recs, axis=1)
    a = jnp.dot(att_ref[...], wpa_ref[...], preferred_element_type=F32)
    b = jnp.dot(rec, wpb_ref[...], preferred_element_type=F32)
    mix = (ga_ref[...] * a + gb_ref[...] * b).astype(BF16)
    y_ref[...] = x_ref[...] + jnp.dot(mix, wo_ref[...], preferred_element_type=F32)


def _back(x_all, att, ro, gr, ga, gb, grec, wpa, wpb, wo):
    n_rows, d = x_all.shape
    row = lambda i: (i, 0)
    const = lambda i: (0, 0)
    return pl.pallas_call(
        _back_body,
        grid=(n_rows // TOK_TILE,),
        in_specs=[
            pl.BlockSpec((TOK_TILE, d), row), pl.BlockSpec((TOK_TILE, 512), row),
            pl.BlockSpec((TOK_TILE, 512), row), pl.BlockSpec((TOK_TILE, 512), row),
            pl.BlockSpec((TOK_TILE, d), row), pl.BlockSpec((TOK_TILE, d), row),
            pl.BlockSpec((1, LANES), const),
            pl.BlockSpec(wpa.shape, const), pl.BlockSpec(wpb.shape, const), pl.BlockSpec(wo.shape, const),
        ],
        out_specs=pl.BlockSpec((TOK_TILE, d), row),
        out_shape=jax.ShapeDtypeStruct((n_rows, d), F32),
        compiler_params=pltpu.CompilerParams(
            dimension_semantics=("arbitrary",), vmem_limit_bytes=VMEM_LIMIT),
        name="mixer_back",
    )(x_all, att, ro, gr, ga, gb, grec, wpa, wpb, wo)


def _ffn_body(y_ref, g_ref, wgu_ref, wd_ref, o_ref, *, d_ff):
    y = y_ref[...]
    hb = (y * lax.rsqrt(jnp.mean(y * y, axis=-1, keepdims=True) + EPS) * g_ref[...]).astype(BF16)
    acc = y
    for c in range(d_ff // FFN_CHUNK):
        c0 = c * FFN_CHUNK
        gate = jnp.dot(hb, wgu_ref[:, c0:c0 + FFN_CHUNK], preferred_element_type=F32)
        up = jnp.dot(hb, wgu_ref[:, d_ff + c0:d_ff + c0 + FFN_CHUNK], preferred_element_type=F32)
        act = (gate * _sigmoid(gate) * up).astype(BF16)
        acc = acc + jnp.dot(act, wd_ref[c0:c0 + FFN_CHUNK, :], preferred_element_type=F32)
    o_ref[...] = acc


def _ffn(y, g, wgu, wd):
    n_rows, d = y.shape
    d_ff = wd.shape[0]
    row = lambda i: (i, 0)
    const = lambda i: (0, 0)
    return pl.pallas_call(
        functools.partial(_ffn_body, d_ff=d_ff),
        grid=(n_rows // TOK_TILE,),
        in_specs=[
            pl.BlockSpec((TOK_TILE, d), row), pl.BlockSpec((1, d), const),
            pl.BlockSpec(wgu.shape, const, pipeline_mode=pl.Buffered(1)),
            pl.BlockSpec(wd.shape, const, pipeline_mode=pl.Buffered(1)),
        ],
        out_specs=pl.BlockSpec((TOK_TILE, d), row),
        out_shape=jax.ShapeDtypeStruct((n_rows, d), F32),
        compiler_params=pltpu.CompilerParams(
            dimension_semantics=("arbitrary",), vmem_limit_bytes=VMEM_LIMIT),
        name="ffn",
    )(y, g, wgu, wd)


def _rope_table(pos):
    half = ROT_DIM // 2
    inv = jnp.power(ROPE_THETA, -2.0 * jnp.arange(half, dtype=F32) / ROT_DIM)
    ang = pos.astype(F32)[:, None] * inv[None, :]
    cos, sin = jnp.cos(ang), jnp.sin(ang)
    n = pos.shape[0]
    one = jnp.ones((n, HEAD_DIM - ROT_DIM), F32)
    zero8 = jnp.zeros((n, half), F32)
    zero = jnp.zeros((n, HEAD_DIM - ROT_DIM), F32)
    c64 = jnp.concatenate([cos, cos, one], axis=1)
    up64 = jnp.concatenate([zero8, sin, zero], axis=1)
    dn64 = jnp.concatenate([-sin, zero8, zero], axis=1)
    return jnp.concatenate([c64, c64, up64, up64, dn64, dn64], axis=1)


def _pack_w_in(w):
    d = w.shape[0]
    cuts = (512, 768, 1024, 1536, 1600, 1608, 2120, 2632, 3144, 3656, 4680)
    q, k, v, qi, ki, wi, qr, fr, ir, gr, ga, gb = jnp.split(w, cuts, axis=1)
    pad = jnp.zeros((d, LANES - IDX_DIM - N_IDX_HEADS), w.dtype)
    return jnp.concatenate([q, k, v, qi, ki, wi, pad, qr, fr, ir, gr, ga, gb], axis=1).astype(BF16)


def kernel(x_prompt, x_sample, cache_k, cache_v, cache_idx_k, state_rec, page_table, meta_tokens,
           w_in, norm_mix, q_norm, k_norm, lb_raw, rec_norm, w_pa, w_pb, w_o, norm_ffn, w_gu, w_down):
    batch, seq, d = x_prompt.shape
    db = x_sample.shape[0]
    depth = w_in.shape[0]
    n_pages = page_table.shape[1]
    page = cache_k.shape[2]
    past_len = n_pages * page
    t_real = seq + N_META
    t_pad = -(-t_real // TOK_TILE) * TOK_TILE
    tiles_per_seq = t_pad // TOK_TILE
    n_prompt = batch * t_pad
    top_k = min(TOPK_MAX, seq // 4)

    sm = jax.nn.softmax(lb_raw.astype(F32), axis=0)
    lower_bounds = jnp.cumsum(sm, axis=0) - sm[0:1]

    xp = jnp.concatenate([jnp.broadcast_to(meta_tokens[None].astype(F32), (batch, N_META, d)), x_prompt], axis=1)
    xp = jnp.pad(xp, ((0, 0), (0, t_pad - t_real), (0, 0))).reshape(n_prompt, d)
    xs = jnp.pad(x_sample.reshape(db, d), ((0, TOK_TILE - db), (0, 0)))
    x_all = jnp.concatenate([xp, xs], axis=0)

    rope_tab = jnp.concatenate([
        _rope_table(jnp.arange(t_pad, dtype=jnp.int32)),
        _rope_table(jnp.full((TOK_TILE,), past_len, jnp.int32))], axis=0)

    n_pool = cache_k.shape[1]
    cache_k2 = cache_k.reshape(depth, n_pool, page * N_KV_HEADS, HEAD_DIM)
    cache_v2 = cache_v.reshape(depth, n_pool, page * N_KV_HEADS, HEAD_DIM)
    srows = slice(n_prompt, n_prompt + db)
    tile2 = lambda g_: jnp.concatenate([g_, g_]).reshape(1, LANES).astype(F32)

    pk, pv, pki, ps, sk, sv, ski, ss = [], [], [], [], [], [], [], []
    for l in range(depth):
        pj = _inproj(x_all, norm_mix[l].reshape(1, d).astype(F32), _pack_w_in(w_in[l]), rope_tab,
                     tile2(q_norm[l]), tile2(k_norm[l]), lower_bounds[l].reshape(1, -1),
                     tiles_per_seq, n_prompt // TOK_TILE)

        att_p = _attn_prompt(pj, batch, t_pad, top_k)
        ro_p, st_p = _gla_prompt(pj, batch, t_pad, t_real)

        q_s = pj["q"][srows].reshape(db, N_HEADS, LANES)
        q_s = jnp.stack([q_s[:, h, ((h // 2) % 2) * HEAD_DIM:((h // 2) % 2 + 1) * HEAD_DIM]
                         for h in range(N_HEADS)], axis=1)
        qi_s = pj["qi"][srows].reshape(db, N_IDX_HEADS, IDX_DIM)
        w_s = pj["kiwi"][srows, IDX_DIM:IDX_DIM + N_IDX_HEADS].reshape(db, N_IDX_HEADS, 1)
        ki_s = pj["kk"][srows, 0:IDX_DIM].reshape(db, 1, IDX_DIM)
        kn_s = jnp.repeat(pj["kb"][srows].reshape(db, N_KV_HEADS, HEAD_DIM), 2, axis=1)
        vn_s = jnp.repeat(pj["vb"][srows].reshape(db, N_KV_HEADS, HEAD_DIM), 2, axis=1)
        att_s = _attn_sample(page_table, l, cache_idx_k, cache_k2, cache_v2, qi_s, w_s, ki_s, q_s, kn_s, vn_s)
        ro_s, st_s = _gla_sample(pj["rq"][srows], pj["f"][srows], pj["rv"][srows], state_rec[l])

        pad_s = lambda a: jnp.pad(a, ((0, TOK_TILE - db), (0, 0)))
        att = jnp.concatenate([att_p, pad_s(att_s.astype(BF16))], axis=0)
        ro = jnp.concatenate([ro_p, pad_s(ro_s)], axis=0)
        y = _back(x_all, att, ro, pj["gr"], pj["ga"], pj["gb"], rec_norm[l].reshape(1, LANES).astype(F32),
                  w_pa[l].astype(BF16), w_pb[l].astype(BF16), w_o[l].astype(BF16))
        x_all = _ffn(y, norm_ffn[l].reshape(1, d).astype(F32), w_gu[l].astype(BF16), w_down[l].astype(BF16))

        seq_view = lambda a, wd: a[:n_prompt].reshape(batch, t_pad, wd)[:, :t_real]
        pk.append(seq_view(pj["kf"], 256).reshape(batch, t_real, N_KV_HEADS, HEAD_DIM))
        pv.append(seq_view(pj["vf"], 256).reshape(batch, t_real, N_KV_HEADS, HEAD_DIM))
        pki.append(seq_view(pj["kiwi"], LANES)[..., :IDX_DIM])
        ps.append(st_p)
        sk.append(pj["kf"][srows].reshape(db, 1, N_KV_HEADS, HEAD_DIM))
        sv.append(pj["vf"][srows].reshape(db, 1, N_KV_HEADS, HEAD_DIM))
        ski.append(pj["kiwi"][srows, :IDX_DIM].reshape(db, 1, IDX_DIM))
        ss.append(st_s)

    y_prompt = x_all[:n_prompt].reshape(batch, t_pad, d)[:, N_META:t_real]
    y_sample = x_all[srows].reshape(db, 1, d)
    return (y_prompt, y_sample, jnp.stack(pk), jnp.stack(pv), jnp.stack(pki), jnp.stack(ps),
            jnp.stack(sk), jnp.stack(sv), jnp.stack(ski), jnp.stack(ss))
```

```python
import functools

import jax
import jax.numpy as jnp
from jax import lax
from jax.experimental import pallas as pl
from jax.experimental.pallas import tpu as pltpu

F32 = jnp.float32
BF16 = jnp.bfloat16

N_META = 16
N_HEADS = 8
HEAD_DIM = 64
N_KV_HEADS = 4
ROT_DIM = 16
ROPE_THETA = 500000.0
N_IDX_HEADS = 8
IDX_DIM = 64
IDX_W_SCALE = (N_IDX_HEADS * IDX_DIM) ** -0.5
TOPK_MAX = 256
N_REC_HEADS = 4
REC_DIM = 128
EPS = 1e-6

LANES = 128
SUBLANES = 8

TOK_TILE = 384
Q_TILE = 128
KEY_GROUP = 4
REC_CHUNK = 64
PAGES_PER_STEP = 16
FFN_CHUNK = 256
VMEM_LIMIT = 56 * 1024 * 1024

NEG_SCORE = -3.0e38
NEG_HALF = -1.5e38
NEG_BIAS = -1.0e30
BIG = 3.0e38
MAX_BISECTIONS = 2200

C_Q, C_K, C_V, C_QI, C_KIWI, C_RQ, C_FR, C_IR, C_GR, C_GA, C_GB, C_END = (
    0, 512, 768, 1024, 1536, 1664, 2176, 2688, 3200, 3712, 4736, 5760)

_NT = (((1,), (1,)), ((), ()))
_TN = (((0,), (0,)), ((), ()))


def _sigmoid(x):
    return 1.0 / (1.0 + jnp.exp(-x))


def _split3(x):
    hi = x.astype(BF16)
    r1 = x - hi.astype(F32)
    mid = r1.astype(BF16)
    lo = (r1 - mid.astype(F32)).astype(BF16)
    return jnp.concatenate([hi, mid, lo], axis=1)


def _join3(r):
    return r[:, 0:LANES] + r[:, LANES:2 * LANES] + r[:, 2 * LANES:3 * LANES]


def _inproj_body(x_ref, g_ref, w_ref, rope_ref, qn_ref, kn_ref, lb_ref,
                 q_ref, kf_ref, vf_ref, kb_ref, vb_ref, qi_ref, kiwi_ref, kk_ref,
                 rq_ref, f_ref, rv_ref, gr_ref, ga_ref, gb_ref):
    x = x_ref[...]
    h = x * lax.rsqrt(jnp.mean(x * x, axis=-1, keepdims=True) + EPS) * g_ref[...]
    hb = h.astype(BF16)
    rows = x.shape[0]
    lane = lax.broadcasted_iota(jnp.int32, (rows, LANES), 1)
    low = lane < HEAD_DIM
    cosv = rope_ref[:, 0:LANES]
    sin_up = rope_ref[:, LANES:2 * LANES]
    sin_dn = rope_ref[:, 2 * LANES:3 * LANES]

    def proj(c0, c1):
        return jnp.dot(hb, w_ref[:, c0:c1], preferred_element_type=F32)

    def rope(xs):
        return (xs * cosv + pltpu.roll(xs, ROT_DIM // 2, 1) * sin_up
                + pltpu.roll(xs, LANES - ROT_DIM // 2, 1) * sin_dn)

    def headnorm(xs, gain):
        sq = xs * xs
        s_lo = jnp.sum(jnp.where(low, sq, 0.0), axis=-1, keepdims=True)
        s_hi = jnp.sum(jnp.where(low, 0.0, sq), axis=-1, keepdims=True)
        ms = jnp.where(low, s_lo, s_hi) * (1.0 / HEAD_DIM)
        return xs * lax.rsqrt(ms + EPS) * gain

    qraw = proj(C_Q, C_K)
    for p in range(N_KV_HEADS):
        qs = rope(headnorm(qraw[:, p * LANES:(p + 1) * LANES], qn_ref[...])) * (HEAD_DIM ** -0.5)
        qr = pltpu.roll(qs, HEAD_DIM, 1)
        if p % 2 == 0:
            h0, h1 = jnp.where(low, qs, 0.0), jnp.where(low, qr, 0.0)
        else:
            h0, h1 = jnp.where(low, 0.0, qr), jnp.where(low, 0.0, qs)
        q_ref[:, (2 * p) * LANES:(2 * p + 1) * LANES] = h0.astype(BF16)
        q_ref[:, (2 * p + 1) * LANES:(2 * p + 2) * LANES] = h1.astype(BF16)

    kraw = proj(C_K, C_V)
    for s in range(2):
        ks = rope(headnorm(kraw[:, s * LANES:(s + 1) * LANES], kn_ref[...]))
        kf_ref[:, s * LANES:(s + 1) * LANES] = ks
        kb_ref[:, s * LANES:(s + 1) * LANES] = ks.astype(BF16)

    vraw = proj(C_V, C_QI)
    vf_ref[...] = vraw
    vb_ref[...] = vraw.astype(BF16)

    qiraw = proj(C_QI, C_KIWI)
    for p in range(4):
        qi_ref[:, p * LANES:(p + 1) * LANES] = rope(qiraw[:, p * LANES:(p + 1) * LANES]).astype(BF16)

    kiwi = proj(C_KIWI, C_RQ)
    kir = rope(kiwi)
    kiwi_ref[...] = jnp.where(low, kir, kiwi * IDX_W_SCALE)
    kk_ref[...] = jnp.where(low, kir, pltpu.roll(kir, HEAD_DIM, 1)).astype(BF16)

    qr_ = proj(C_RQ, C_FR)
    rq_ref[...] = qr_ * _sigmoid(qr_)
    lb = lb_ref[...]
    f_ref[...] = lb + (1.0 - lb) * _sigmoid(proj(C_FR, C_IR))
    rv_ref[...] = proj(C_IR, C_GR)
    g_ = proj(C_GR, C_GA)
    gr_ref[...] = g_ * _sigmoid(g_)
    ga_ref[...] = _sigmoid(proj(C_GA, C_GB))
    gb_ref[...] = _sigmoid(proj(C_GB, C_END))


def _inproj(x_all, g, w, rope_tab, qn, kn, lb, tiles_per_seq, n_prompt_tiles):
    n_rows, d = x_all.shape
    n_tiles = n_rows // TOK_TILE
    row = lambda i: (i, 0)
    const = lambda i: (0, 0)
    rope_map = lambda i: (jnp.where(i < n_prompt_tiles, i % tiles_per_seq, tiles_per_seq), 0)
    widths = dict(q=(1024, BF16), kf=(256, F32), vf=(256, F32), kb=(256, BF16), vb=(256, BF16),
                  qi=(512, BF16), kiwi=(128, F32), kk=(128, BF16), rq=(512, F32), f=(512, F32),
                  rv=(512, F32), gr=(512, F32), ga=(1024, F32), gb=(1024, F32))
    out_shape = [jax.ShapeDtypeStruct((n_rows, wd), dt) for wd, dt in widths.values()]
    out_specs = [pl.BlockSpec((TOK_TILE, wd), row) for wd, _ in widths.values()]
    outs = pl.pallas_call(
        _inproj_body,
        grid=(n_tiles,),
        in_specs=[
            pl.BlockSpec((TOK_TILE, d), row),
            pl.BlockSpec((1, d), const),
            pl.BlockSpec((d, C_END), const, pipeline_mode=pl.Buffered(1)),
            pl.BlockSpec((TOK_TILE, 3 * LANES), rope_map),
            pl.BlockSpec((1, LANES), const),
            pl.BlockSpec((1, LANES), const),
            pl.BlockSpec((1, 512), const),
        ],
        out_specs=out_specs,
        out_shape=out_shape,
        compiler_params=pltpu.CompilerParams(
            dimension_semantics=("arbitrary",), vmem_limit_bytes=VMEM_LIMIT),
        name="inproj",
    )(x_all, g, w, rope_tab, qn, kn, lb)
    return dict(zip(widths.keys(), outs))


def _select_threshold(count_ge, bracket_minmax, lo0, hi0, n_adm, need, top_k):
    def cond(st):
        return (jnp.min(st[4]) < 0.5) & (st[5] < MAX_BISECTIONS)

    def body(st):
        lo, hi, c_lo, c_hi, _, it = st
        mid = 0.5 * (lo + hi)
        c_mid = count_ge(mid)
        ge = c_mid >= top_k
        lo2, c_lo2 = jnp.where(ge, mid, lo), jnp.where(ge, c_mid, c_lo)
        hi2, c_hi2 = jnp.where(ge, hi, mid), jnp.where(ge, c_hi, c_mid)
        r = top_k - c_hi2
        stop = (c_lo2 - c_hi2 == r) | (r == 1.0) | (mid <= lo) | (mid >= hi) | jnp.logical_not(need)
        return lo2, hi2, c_lo2, c_hi2, jnp.where(stop, 1.0, 0.0), it + 1

    zero = jnp.zeros_like(lo0)
    lo, hi, c_lo, c_hi, _, _ = lax.while_loop(cond, body, (lo0, hi0, n_adm, zero, zero, jnp.int32(0)))
    mn, mx = bracket_minmax(lo, hi)
    r = top_k - c_hi
    thr = jnp.where(need, jnp.where(r == 1.0, mx, mn), NEG_SCORE)
    n_tie = jnp.where(need & (c_lo - c_hi != r), r, BIG)
    return thr, n_tie


def _attn_body(q_ref, qi_ref, wq_ref, kb_ref, vb_ref, kk_ref, o_ref,
               sc_ref, qia_ref, qa_ref, vt_ref, m_ref, l_ref, acc_ref, *, top_k, t_real):
    j = pl.program_id(1)
    nchunks = j + 1
    T = Q_TILE
    key_i = lax.broadcasted_iota(jnp.int32, (T, LANES), 0)
    qry_i = lax.broadcasted_iota(jnp.int32, (T, LANES), 1)
    low = qry_i < HEAD_DIM
    qpos = j * T + qry_i

    @pl.when(j == 0)
    def _():
        def transpose_v(c, carry):
            blk = vb_ref[pl.ds(pl.multiple_of(c * T, T), T), :].astype(F32)
            for s in range(2):
                vt_ref[c, s] = blk[:, s * LANES:(s + 1) * LANES].T.astype(BF16)
            return carry
        lax.fori_loop(0, vt_ref.shape[0], transpose_v, 0)

    wt = wq_ref[...].T
    qi = qi_ref[...]
    for h in range(N_IDX_HEADS):
        slab = qi[:, (h // 2) * LANES:(h // 2 + 1) * LANES]
        zero = jnp.zeros_like(slab)
        qia_ref[h * T:(h + 1) * T, :] = jnp.where(low, slab, zero) if h % 2 == 0 else jnp.where(low, zero, slab)
        qa_ref[h * T:(h + 1) * T, :] = q_ref[:, h * LANES:(h + 1) * LANES]

    G = KEY_GROUP
    ngroups = (nchunks + G - 1) // G
    last_chunk = vt_ref.shape[0] - 1

    def chunk_rows(ref, g, u, lanes):
        cc = jnp.minimum(g * G + u, last_chunk)
        return ref[pl.ds(pl.multiple_of(cc * T, T), T), lanes]

    def fold(x, op):
        out = x[0:T]
        for u in range(1, G):
            out = op(out, x[u * T:(u + 1) * T])
        return out

    def phase_a(g, carry):
        mn, mx = carry
        kk = jnp.concatenate([chunk_rows(kk_ref, g, u, slice(None)) for u in range(G)], axis=0)
        d = lax.dot_general(kk, qia_ref[...], _NT, preferred_element_type=F32)
        s = jnp.zeros((G * T, LANES), F32)
        for h in range(N_IDX_HEADS):
            s = s + wt[IDX_DIM + h:IDX_DIM + h + 1, :] * jnp.maximum(d[:, h * T:(h + 1) * T], 0.0)
        kpos = g * (G * T) + lax.broadcasted_iota(jnp.int32, (G * T, LANES), 0)
        adm = kpos <= j * T + lax.broadcasted_iota(jnp.int32, (G * T, LANES), 1)
        sc_ref[pl.ds(pl.multiple_of(g * (G * T), G * T), G * T), :] = jnp.where(adm, s, NEG_SCORE)
        return (jnp.minimum(mn, fold(jnp.where(adm, s, BIG), jnp.minimum)),
                jnp.maximum(mx, fold(jnp.where(adm, s, -BIG), jnp.maximum)))

    mn, mx = lax.fori_loop(0, ngroups, phase_a,
                           (jnp.full((T, LANES), BIG, F32), jnp.full((T, LANES), -BIG, F32)))

    def sc_group(g):
        return sc_ref[pl.ds(pl.multiple_of(g * (G * T), G * T), G * T), :]

    def count_ge(x):
        def step(g, acc):
            return acc + fold(jnp.where(sc_group(g) >= x, 1.0, 0.0), jnp.add)
        return jnp.sum(lax.fori_loop(0, ngroups, step, jnp.zeros((T, LANES), F32)), axis=0, keepdims=True)

    def bracket_minmax(lo, hi):
        def step(g, carry):
            a, b = carry
            s = sc_group(g)
            inb = (s >= lo) & (s < hi)
            return (jnp.minimum(a, fold(jnp.where(inb, s, BIG), jnp.minimum)),
                    jnp.maximum(b, fold(jnp.where(inb, s, -BIG), jnp.maximum)))
        a, b = lax.fori_loop(0, ngroups, step,
                             (jnp.full((T, LANES), BIG, F32), jnp.full((T, LANES), -BIG, F32)))
        return jnp.min(a, axis=0, keepdims=True), jnp.max(b, axis=0, keepdims=True)

    qrow = j * T + lax.broadcasted_iota(jnp.int32, (1, LANES), 1)
    n_adm = (qrow + 1).astype(F32)
    need = (qrow + 1 > top_k) & (qrow < t_real)
    top = jnp.max(mx, axis=0, keepdims=True)
    thr, n_tie = _select_threshold(count_ge, bracket_minmax, jnp.min(mn, axis=0, keepdims=True),
                                   top + (jnp.abs(top) + 1.0), n_adm, need, float(top_k))

    m_ref[...] = jnp.full(m_ref.shape, NEG_SCORE, F32)
    l_ref[...] = jnp.zeros(l_ref.shape, F32)
    acc_ref[...] = jnp.zeros(acc_ref.shape, F32)
    earlier = jnp.where(qry_i < key_i, 1.0, 0.0).astype(BF16)

    def phase_c(g, seen):
        s = sc_group(g)
        eq = s == thr
        eqf = jnp.where(eq, 1.0, 0.0)
        eq_l = jnp.concatenate([eqf[u * T:(u + 1) * T] for u in range(G)], axis=1).astype(BF16)
        in_chunk = jnp.dot(earlier, eq_l, preferred_element_type=F32)
        ranks = []
        for u in range(G):
            ranks.append(in_chunk[:, u * T:(u + 1) * T] + seen)
            seen = seen + jnp.sum(eqf[u * T:(u + 1) * T], axis=0, keepdims=True)
        rank = jnp.concatenate(ranks, axis=0)
        keep_tie = jnp.where(eq, jnp.where(rank < n_tie, 0.0, NEG_BIAS), NEG_BIAS)
        bias = jnp.where(s > NEG_HALF, jnp.where(s > thr, 0.0, keep_tie), NEG_BIAS)
        bias4 = jnp.concatenate([bias] * 4, axis=1)
        sts = []
        for sl in range(2):
            lanes = slice(sl * LANES, (sl + 1) * LANES)
            kc = jnp.concatenate([chunk_rows(kb_ref, g, u, lanes) for u in range(G)], axis=0)
            sts.append(lax.dot_general(kc, qa_ref[sl * 4 * T:(sl + 1) * 4 * T, :], _NT,
                                       preferred_element_type=F32) + bias4)
        ps, alphas = [], []
        for sl in range(2):
            m_old = m_ref[sl]
            m_new = jnp.maximum(m_old, jnp.max(sts[sl], axis=0, keepdims=True))
            alpha = jnp.exp(m_old - m_new)
            p = jnp.exp(sts[sl] - m_new)
            l_ref[sl] = alpha * l_ref[sl] + jnp.sum(p, axis=0, keepdims=True)
            m_ref[sl] = m_new
            ps.append(p.astype(BF16))
            alphas.append(alpha)
        for sl in range(2):
            vt = jnp.concatenate([vt_ref[jnp.minimum(g * G + u, last_chunk), sl] for u in range(G)], axis=1)
            acc_ref[sl] = alphas[sl] * acc_ref[sl] + jnp.dot(vt, ps[sl], preferred_element_type=F32)
        return seen

    lax.fori_loop(0, ngroups, phase_c, jnp.zeros((1, LANES), F32))

    for p in range(N_KV_HEADS):
        sl, hh = (2 * p) // 4, (2 * p) % 4
        out = acc_ref[sl] / l_ref[sl]
        a0 = out[:, hh * T:(hh + 1) * T].T
        a1 = out[:, (hh + 1) * T:(hh + 2) * T].T
        if p % 2 == 0:
            slab = jnp.where(low, a0, pltpu.roll(a1, HEAD_DIM, 1))
        else:
            slab = jnp.where(low, pltpu.roll(a0, HEAD_DIM, 1), a1)
        o_ref[:, p * LANES:(p + 1) * LANES] = slab.astype(BF16)


def _attn_prompt(pj, batch, t_pad, t_real, top_k):
    nq = t_pad // Q_TILE
    qrow = lambda b, j: (b * nq + j, 0)
    seq = lambda b, j: (b, 0)
    return pl.pallas_call(
        functools.partial(_attn_body, top_k=top_k, t_real=t_real),
        grid=(batch, nq),
        in_specs=[
            pl.BlockSpec((Q_TILE, 1024), qrow),
            pl.BlockSpec((Q_TILE, 512), qrow),
            pl.BlockSpec((Q_TILE, LANES), qrow),
            pl.BlockSpec((t_pad, 256), seq),
            pl.BlockSpec((t_pad, 256), seq),
            pl.BlockSpec((t_pad, LANES), seq),
        ],
        out_specs=pl.BlockSpec((Q_TILE, 512), qrow),
        out_shape=jax.ShapeDtypeStruct((batch * t_pad, 512), BF16),
        scratch_shapes=[
            pltpu.VMEM((-(-nq // KEY_GROUP) * KEY_GROUP * Q_TILE, LANES), F32),
            pltpu.VMEM((N_IDX_HEADS * Q_TILE, LANES), BF16),
            pltpu.VMEM((N_HEADS * Q_TILE, LANES), BF16),
            pltpu.VMEM((nq, 2, LANES, Q_TILE), BF16),
            pltpu.VMEM((2, 1, 4 * Q_TILE), F32),
            pltpu.VMEM((2, 1, 4 * Q_TILE), F32),
            pltpu.VMEM((2, LANES, 4 * Q_TILE), F32),
        ],
        compiler_params=pltpu.CompilerParams(
            dimension_semantics=("arbitrary", "arbitrary"), vmem_limit_bytes=VMEM_LIMIT),
        name="attn_prompt",
    )(pj["q"], pj["qi"], pj["kiwi"], pj["kb"], pj["vb"], pj["kk"])


def _gla_body(rq_ref, f_ref, rv_ref, cum_ref, o_ref, sout_ref, st_ref, *, t_real):
    c = pl.program_id(1)
    n_c = pl.num_programs(1)
    C = REC_CHUNK
    n_lev = C.bit_length() - 1

    @pl.when(c == 0)
    def _():
        st_ref[...] = jnp.zeros(st_ref.shape, F32)

    trow = lax.broadcasted_iota(jnp.int32, (C, LANES), 0)
    valid = c * C + trow < t_real
    ti = lax.broadcasted_iota(jnp.int32, (C, C), 0)
    si = lax.broadcasted_iota(jnp.int32, (C, C), 1)
    H = range(N_REC_HEADS)
    sls = [slice(h * LANES, (h + 1) * LANES) for h in H]
    f = [f_ref[:, sl] for sl in sls]
    q = [rq_ref[:, sl] for sl in sls]
    vb = [rv_ref[:, sl].astype(BF16) for sl in sls]
    kk = [jnp.where(valid, 1.0 - f[h], 0.0) for h in H]

    lf3 = jnp.concatenate([_split3(jnp.where(valid, jnp.log(f[h]), 0.0)) for h in H], axis=1)
    cums = jnp.dot(cum_ref[...], lf3, preferred_element_type=F32)
    cum = [_join3(cums[:, h * 3 * LANES:(h + 1) * 3 * LANES]) for h in H]
    b = [cum[h][0:C] for h in H]

    a = [jnp.where(ti == si, lax.dot_general(q[h].astype(BF16), kk[h].astype(BF16), _NT,
                                             preferred_element_type=F32), 0.0) for h in H]
    for lev in range(n_lev):
        half = 1 << lev
        up = (trow & half) != 0
        blk_mask = ((ti >> (lev + 1)) == (si >> (lev + 1))) & ((ti & half) != 0) & ((si & half) == 0)
        for h in H:
            bl = cum[h][(1 + lev) * C:(2 + lev) * C]
            e = jnp.exp(jnp.where(up, b[h] - bl, bl - b[h]))
            qt = jnp.where(up, q[h] * e, 0.0).astype(BF16)
            kt = jnp.where(up, 0.0, kk[h] * e).astype(BF16)
            a[h] = a[h] + jnp.where(blk_mask, lax.dot_general(qt, kt, _NT, preferred_element_type=F32), 0.0)

    for h in H:
        st = st_ref[h]
        qe = (q[h] * jnp.exp(b[h])).astype(BF16)
        o = jnp.dot(a[h].astype(BF16), vb[h], preferred_element_type=F32)
        o_ref[:, sls[h]] = o + lax.dot_general(qe, st.astype(BF16), _NT, preferred_element_type=F32)
        b_last = b[h][C - 1:C, :]
        kd = (kk[h] * jnp.exp(b_last - b[h])).astype(BF16)
        st_new = st * jnp.exp(b_last) + lax.dot_general(vb[h], kd, _TN, preferred_element_type=F32)
        st_ref[h] = st_new

        @pl.when(c == n_c - 1)
        def _(h=h, st_new=st_new):
            sout_ref[0, h] = st_new.T


def _gla_consts():
    C = REC_CHUNK
    n_lev = C.bit_length() - 1
    t = jnp.arange(C)
    tri = t[None, :] <= t[:, None]
    blocks = [tri]
    for lev in range(n_lev):
        half = 1 << lev
        mid_row = (t >> (lev + 1)) * (2 * half) + half - 1
        blocks.append(t[None, :] <= mid_row[:, None])
    return jnp.concatenate(blocks, axis=0).astype(BF16)


def _gla_prompt(pj, batch, t_pad, t_real):
    nch = t_pad // REC_CHUNK
    cum = _gla_consts()
    row = lambda b, c: (b * nch + c, 0)
    const = lambda b, c: (0, 0)
    return pl.pallas_call(
        functools.partial(_gla_body, t_real=t_real),
        grid=(batch, nch),
        in_specs=[
            pl.BlockSpec((REC_CHUNK, 512), row),
            pl.BlockSpec((REC_CHUNK, 512), row),
            pl.BlockSpec((REC_CHUNK, 512), row),
            pl.BlockSpec(cum.shape, const),
        ],
        out_specs=[
            pl.BlockSpec((REC_CHUNK, 512), row),
            pl.BlockSpec((1, N_REC_HEADS, REC_DIM, REC_DIM), lambda b, c: (b, 0, 0, 0)),
        ],
        out_shape=[
            jax.ShapeDtypeStruct((batch * t_pad, 512), F32),
            jax.ShapeDtypeStruct((batch, N_REC_HEADS, REC_DIM, REC_DIM), F32),
        ],
        scratch_shapes=[pltpu.VMEM((N_REC_HEADS, REC_DIM, REC_DIM), F32)],
        compiler_params=pltpu.CompilerParams(
            dimension_semantics=("arbitrary", "arbitrary"), vmem_limit_bytes=VMEM_LIMIT),
        name="gla_prompt",
    )(pj["rq"], pj["f"], pj["rv"], cum)


def _gla_sample_body(rq_ref, f_ref, rv_ref, s0_ref, o_ref, sout_ref):
    b = pl.program_id(0)
    q = rq_ref[pl.ds(b, 1), :]
    f = f_ref[pl.ds(b, 1), :]
    v = rv_ref[pl.ds(b, 1), :]
    for h in range(N_REC_HEADS):
        sl = slice(h * LANES, (h + 1) * LANES)
        col = lambda r: jnp.broadcast_to(r[:, sl], (REC_DIM, LANES)).T
        s_new = col(f) * s0_ref[0, h] + col(1.0 - f) * v[:, sl]
        sout_ref[0, h] = s_new
        o_ref[0, :, sl] = jnp.sum(col(q) * s_new, axis=0, keepdims=True)


def _gla_sample(rq, f, rv, state):
    db = state.shape[0]
    full = lambda b: (0, 0)
    st = lambda b: (b, 0, 0, 0)
    o, s = pl.pallas_call(
        _gla_sample_body,
        grid=(db,),
        in_specs=[
            pl.BlockSpec(rq.shape, full), pl.BlockSpec(f.shape, full), pl.BlockSpec(rv.shape, full),
            pl.BlockSpec((1, N_REC_HEADS, REC_DIM, REC_DIM), st),
        ],
        out_specs=[
            pl.BlockSpec((1, 1, 512), lambda b: (b, 0, 0)),
            pl.BlockSpec((1, N_REC_HEADS, REC_DIM, REC_DIM), st),
        ],
        out_shape=[
            jax.ShapeDtypeStruct((db, 1, 512), F32),
            jax.ShapeDtypeStruct(state.shape, F32),
        ],
        compiler_params=pltpu.CompilerParams(dimension_semantics=("arbitrary",)),
        name="gla_sample",
    )(rq, f, rv, state)
    return o.reshape(db, 512), s


def _attn_sample_body(pt_ref, *refs, n_pages, top_k):
    P = PAGES_PER_STEP
    ki_refs, k_refs, v_refs = refs[0:P], refs[P:2 * P], refs[2 * P:3 * P]
    (qi_ref, w_ref, kin_ref, q_ref, kn_ref, vn_ref,
     o_ref, sc_ref, m_ref, l_ref, acc_ref) = refs[3 * P:]
    g = pl.program_id(1)
    n_groups = n_pages // P
    n_rows = sc_ref.shape[0]
    qi = qi_ref[0]
    w = w_ref[0]

    @pl.when(g < n_groups)
    def _():
        rows = []
        for i in range(P):
            d = jnp.dot(qi, ki_refs[i][...].astype(BF16), preferred_element_type=F32)
            rows.append(jnp.sum(w * jnp.maximum(d, 0.0), axis=0, keepdims=True))
        sc_ref[pl.ds(pl.multiple_of(g * P, P), P), :] = jnp.concatenate(rows, axis=0)

    @pl.when(g == n_groups - 1)
    def _():
        d_new = jnp.sum(qi.astype(F32) * kin_ref[0].astype(F32), axis=1, keepdims=True)
        s_new = jnp.sum(w * jnp.maximum(d_new, 0.0), axis=0, keepdims=True)
        tail = lax.broadcasted_iota(jnp.int32, (n_rows - n_pages, LANES), 0) * LANES + \
            lax.broadcasted_iota(jnp.int32, (n_rows - n_pages, LANES), 1)
        sc_ref[n_pages:n_rows, :] = jnp.where(tail == 0, s_new, NEG_SCORE)

        s = sc_ref[...]
        adm = s > NEG_HALF
        top = jnp.max(s, keepdims=True)

        def count_ge(x):
            return jnp.sum(jnp.where(sc_ref[...] >= x, 1.0, 0.0), keepdims=True)

        def bracket_minmax(lo, hi):
            v = sc_ref[...]
            inb = (v >= lo) & (v < hi)
            return (jnp.min(jnp.where(inb, v, BIG), keepdims=True),
                    jnp.max(jnp.where(inb, v, -BIG), keepdims=True))

        n_adm = jnp.full((1, 1), float(n_pages * LANES + 1), F32)
        thr, n_tie = _select_threshold(count_ge, bracket_minmax, jnp.min(jnp.where(adm, s, BIG), keepdims=True),
                                       top + (jnp.abs(top) + 1.0), n_adm, n_adm > top_k, float(top_k))
        eqb = jnp.where(s == thr, 1.0, 0.0).astype(BF16)
        ri = lax.broadcasted_iota(jnp.int32, (LANES, LANES), 0)
        ci = lax.broadcasted_iota(jnp.int32, (LANES, LANES), 1)
        in_row = jnp.dot(eqb, jnp.where(ri < ci, 1.0, 0.0).astype(BF16), preferred_element_type=F32)
        row_tot = jnp.dot(eqb, jnp.ones((LANES, LANES), BF16), preferred_element_type=F32)
        rr = lax.broadcasted_iota(jnp.int32, (n_rows, n_rows), 0)
        rc = lax.broadcasted_iota(jnp.int32, (n_rows, n_rows), 1)
        before = jnp.dot(jnp.where(rc < rr, 1.0, 0.0).astype(BF16), row_tot.astype(BF16),
                         preferred_element_type=F32)
        rank = in_row + before
        keep = (s > thr) | ((s == thr) & (rank < n_tie))
        sc_ref[...] = jnp.where(keep & adm, 1.0, 0.0)

        q = q_ref[0].astype(F32)
        s_own = jnp.sum(q * kn_ref[0].astype(F32), axis=1, keepdims=True)
        own_sel = sc_ref[n_pages:n_pages + 1, 0:1]
        m_ref[...] = jnp.broadcast_to(s_own + jnp.where(own_sel > 0.5, 0.0, NEG_BIAS), m_ref.shape)
        l_ref[...] = jnp.ones(l_ref.shape, F32)
        acc_ref[...] = jnp.broadcast_to(vn_ref[0].astype(F32), acc_ref.shape)

    @pl.when(g >= n_groups)
    def _():
        q = q_ref[0]
        sel = sc_ref[pl.ds(pl.multiple_of((g - n_groups) * P, P), P), :]
        scores = []
        for i in range(P):
            s = jnp.dot(q, k_refs[i][...].astype(BF16), preferred_element_type=F32)
            scores.append(jnp.where(sel[i:i + 1, :] > 0.5, s, NEG_BIAS))
        s_all = jnp.concatenate(scores, axis=1)
        m_old = m_ref[...]
        m_new = jnp.maximum(m_old, jnp.max(s_all, axis=1, keepdims=True))
        alpha = jnp.exp(m_old - m_new)
        p32 = jnp.exp(s_all - m_new[:, 0:1])
        l_ref[...] = alpha * l_ref[...] + jnp.sum(p32, axis=1, keepdims=True)
        m_ref[...] = m_new
        p = p32.astype(BF16)
        pv = jnp.zeros(acc_ref.shape, F32)
        for i in range(P):
            pv = pv + lax.dot_general(p[:, i * LANES:(i + 1) * LANES], v_refs[i][...].astype(BF16), _NT,
                                      preferred_element_type=F32)
        acc_ref[...] = alpha[:, 0:1] * acc_ref[...] + pv

        @pl.when(g == 2 * n_groups - 1)
        def _():
            o_ref[0] = acc_ref[...] / l_ref[:, 0:1]


def _attn_sample(page_table, layer, cache_kit, cache_kt, cache_vt, qi, w, ki_new, q, k_new, v_new):
    db, n_pages = page_table.shape
    P = PAGES_PER_STEP
    n_groups = n_pages // P
    page = cache_kit.shape[3]
    kv_w = N_KV_HEADS * HEAD_DIM
    top_k = min(TOPK_MAX, (n_pages * page + 1) // 4)
    n_rows = -(-(n_pages + 1) // SUBLANES) * SUBLANES

    def ki_map(i):
        return lambda b, g, pt: (layer, pt[b * n_pages + jnp.minimum(g, n_groups - 1) * P + i], 0, 0)

    def kv_map(i):
        return lambda b, g, pt: (layer, pt[b * n_pages + jnp.maximum(g - n_groups, 0) * P + i], 0, 0)

    per_b = lambda b, g, pt: (b, 0, 0)
    in_specs = (
        [pl.BlockSpec((None, None, IDX_DIM, page), ki_map(i)) for i in range(P)]
        + [pl.BlockSpec((None, None, kv_w, page), kv_map(i)) for i in range(P)]
        + [pl.BlockSpec((None, None, kv_w, page), kv_map(i)) for i in range(P)]
        + [pl.BlockSpec((1,) + a.shape[1:], per_b) for a in (qi, w, ki_new, q, k_new, v_new)])
    out = pl.pallas_call(
        functools.partial(_attn_sample_body, n_pages=n_pages, top_k=top_k),
        grid_spec=pltpu.PrefetchScalarGridSpec(
            num_scalar_prefetch=1,
            grid=(db, 2 * n_groups),
            in_specs=in_specs,
            out_specs=pl.BlockSpec((1, N_HEADS, kv_w), per_b),
            scratch_shapes=[
                pltpu.VMEM((n_rows, LANES), F32),
                pltpu.VMEM((N_HEADS, LANES), F32),
                pltpu.VMEM((N_HEADS, LANES), F32),
                pltpu.VMEM((N_HEADS, kv_w), F32),
            ]),
        out_shape=jax.ShapeDtypeStruct((db, N_HEADS, kv_w), F32),
        compiler_params=pltpu.CompilerParams(
            dimension_semantics=("arbitrary", "arbitrary"), vmem_limit_bytes=VMEM_LIMIT),
        name="attn_sample",
    )(page_table.reshape(-1), *([cache_kit] * P), *([cache_kt] * P), *([cache_vt] * P),
      qi, w, ki_new, q, k_new, v_new)
    return jnp.concatenate([out[:, h, (h // 2) * HEAD_DIM:(h // 2 + 1) * HEAD_DIM] for h in range(N_HEADS)], axis=1)


def _back_body(x_ref, att_ref, ro_ref, gr_ref, ga_ref, gb_ref, grec_ref, wpa_ref, wpb_ref, wo_ref, y_ref):
    ro = ro_ref[...]
    gr = gr_ref[...]
    recs = []
    for h in range(N_REC_HEADS):
        r = ro[:, h * LANES:(h + 1) * LANES]
        r = r * lax.rsqrt(jnp.mean(r * r, axis=-1, keepdims=True) + EPS) * grec_ref[...]
        recs.append((r * gr[:, h * LANES:(h + 1) * LANES]).astype(BF16))
    rec = jnp.concatenate(recs, axis=1)
    a = jnp.dot(att_ref[...], wpa_ref[...], preferred_element_type=F32)
    b = jnp.dot(rec, wpb_ref[...], preferred_element_type=F32)
    mix = (ga_ref[...] * a + gb_ref[...] * b).astype(BF16)
    y_ref[...] = x_ref[...] + jnp.dot(mix, wo_ref[...], preferred_element_type=F32)


def _back(x_all, att, ro, gr, ga, gb, grec, wpa, wpb, wo):
    n_rows, d = x_all.shape
    row = lambda i: (i, 0)
    const = lambda i: (0, 0)
    return pl.pallas_call(
        _back_body,
        grid=(n_rows // TOK_TILE,),
        in_specs=[
            pl.BlockSpec((TOK_TILE, d), row), pl.BlockSpec((TOK_TILE, 512), row),
            pl.BlockSpec((TOK_TILE, 512), row), pl.BlockSpec((TOK_TILE, 512), row),
            pl.BlockSpec((TOK_TILE, d), row), pl.BlockSpec((TOK_TILE, d), row),
            pl.BlockSpec((1, LANES), const),
            pl.BlockSpec(wpa.shape, const), pl.BlockSpec(wpb.shape, const), pl.BlockSpec(wo.shape, const),
        ],
        out_specs=pl.BlockSpec((TOK_TILE, d), row),
        out_shape=jax.ShapeDtypeStruct((n_rows, d), F32),
        compiler_params=pltpu.CompilerParams(
            dimension_semantics=("arbitrary",), vmem_limit_bytes=VMEM_LIMIT),
        name="mixer_back",
    )(x_all, att, ro, gr, ga, gb, grec, wpa, wpb, wo)


def _ffn_body(y_ref, g_ref, wgu_ref, wd_ref, o_ref, *, d_ff):
    y = y_ref[...]
    hb = (y * lax.rsqrt(jnp.mean(y * y, axis=-1, keepdims=True) + EPS) * g_ref[...]).astype(BF16)
    acc = y
    for c in range(d_ff // FFN_CHUNK):
        c0 = c * FFN_CHUNK
        gate = jnp.dot(hb, wgu_ref[:, c0:c0 + FFN_CHUNK], preferred_element_type=F32)
        up = jnp.dot(hb, wgu_ref[:, d_ff + c0:d_ff + c0 + FFN_CHUNK], preferred_element_type=F32)
        act = (gate * _sigmoid(gate) * up).astype(BF16)
        acc = acc + jnp.dot(act, wd_ref[c0:c0 + FFN_CHUNK, :], preferred_element_type=F32)
    o_ref[...] = acc


def _ffn(y, g, wgu, wd):
    n_rows, d = y.shape
    d_ff = wd.shape[0]
    row = lambda i: (i, 0)
    const = lambda i: (0, 0)
    return pl.pallas_call(
        functools.partial(_ffn_body, d_ff=d_ff),
        grid=(n_rows // TOK_TILE,),
        in_specs=[
            pl.BlockSpec((TOK_TILE, d), row), pl.BlockSpec((1, d), const),
            pl.BlockSpec(wgu.shape, const, pipeline_mode=pl.Buffered(1)),
            pl.BlockSpec(wd.shape, const, pipeline_mode=pl.Buffered(1)),
        ],
        out_specs=pl.BlockSpec((TOK_TILE, d), row),
        out_shape=jax.ShapeDtypeStruct((n_rows, d), F32),
        compiler_params=pltpu.CompilerParams(
            dimension_semantics=("arbitrary",), vmem_limit_bytes=VMEM_LIMIT),
        name="ffn",
    )(y, g, wgu, wd)


def _rope_table(pos):
    half = ROT_DIM // 2
    inv = jnp.power(ROPE_THETA, -2.0 * jnp.arange(half, dtype=F32) / ROT_DIM)
    ang = pos.astype(F32)[:, None] * inv[None, :]
    cos, sin = jnp.cos(ang), jnp.sin(ang)
    n = pos.shape[0]
    one = jnp.ones((n, HEAD_DIM - ROT_DIM), F32)
    zero8 = jnp.zeros((n, half), F32)
    zero = jnp.zeros((n, HEAD_DIM - ROT_DIM), F32)
    c64 = jnp.concatenate([cos, cos, one], axis=1)
    up64 = jnp.concatenate([zero8, sin, zero], axis=1)
    dn64 = jnp.concatenate([-sin, zero8, zero], axis=1)
    return jnp.concatenate([c64, c64, up64, up64, dn64, dn64], axis=1)


def _pack_w_in(w):
    d = w.shape[0]
    cuts = (512, 768, 1024, 1536, 1600, 1608, 2120, 2632, 3144, 3656, 4680)
    q, k, v, qi, ki, wi, qr, fr, ir, gr, ga, gb = jnp.split(w, cuts, axis=1)
    pad = jnp.zeros((d, LANES - IDX_DIM - N_IDX_HEADS), w.dtype)
    return jnp.concatenate([q, k, v, qi, ki, wi, pad, qr, fr, ir, gr, ga, gb], axis=1).astype(BF16)


def kernel(x_prompt, x_sample, cache_k, cache_v, cache_idx_k, state_rec, page_table, meta_tokens,
           w_in, norm_mix, q_norm, k_norm, lb_raw, rec_norm, w_pa, w_pb, w_o, norm_ffn, w_gu, w_down):
    batch, seq, d = x_prompt.shape
    db = x_sample.shape[0]
    depth = w_in.shape[0]
    n_pages = page_table.shape[1]
    page = cache_k.shape[2]
    past_len = n_pages * page
    t_real = seq + N_META
    t_pad = -(-t_real // TOK_TILE) * TOK_TILE
    tiles_per_seq = t_pad // TOK_TILE
    n_prompt = batch * t_pad
    top_k = min(TOPK_MAX, seq // 4)

    sm = jax.nn.softmax(lb_raw.astype(F32), axis=0)
    lower_bounds = jnp.cumsum(sm, axis=0) - sm[0:1]

    xp = jnp.concatenate([jnp.broadcast_to(meta_tokens[None].astype(F32), (batch, N_META, d)), x_prompt], axis=1)
    xp = jnp.pad(xp, ((0, 0), (0, t_pad - t_real), (0, 0))).reshape(n_prompt, d)
    xs = jnp.pad(x_sample.reshape(db, d), ((0, TOK_TILE - db), (0, 0)))
    x_all = jnp.concatenate([xp, xs], axis=0)

    rope_tab = jnp.concatenate([
        _rope_table(jnp.arange(t_pad, dtype=jnp.int32)),
        _rope_table(jnp.full((TOK_TILE,), past_len, jnp.int32))], axis=0)

    n_pool = cache_k.shape[1]
    cache_kt = jnp.transpose(cache_k, (0, 1, 3, 4, 2)).reshape(depth, n_pool, N_KV_HEADS * HEAD_DIM, page)
    cache_vt = jnp.transpose(cache_v, (0, 1, 3, 4, 2)).reshape(depth, n_pool, N_KV_HEADS * HEAD_DIM, page)
    cache_kit = jnp.transpose(cache_idx_k, (0, 1, 3, 2))
    srows = slice(n_prompt, n_prompt + db)
    tile2 = lambda g_: jnp.concatenate([g_, g_]).reshape(1, LANES).astype(F32)

    pk, pv, pki, ps, sk, sv, ski, ss = [], [], [], [], [], [], [], []
    for l in range(depth):
        pj = _inproj(x_all, norm_mix[l].reshape(1, d).astype(F32), _pack_w_in(w_in[l]), rope_tab,
                     tile2(q_norm[l]), tile2(k_norm[l]), lower_bounds[l].reshape(1, -1),
                     tiles_per_seq, n_prompt // TOK_TILE)

        att_p = _attn_prompt(pj, batch, t_pad, t_real, top_k)
        ro_p, st_p = _gla_prompt(pj, batch, t_pad, t_real)

        q_s = pj["q"][srows].reshape(db, N_HEADS, LANES)
        zero_s = jnp.zeros_like(q_s)
        q_s = jnp.stack([jnp.concatenate([q_s[:, h] if h // 4 == s else zero_s[:, h] for s in range(2)], axis=1)
                         for h in range(N_HEADS)], axis=1)
        qi_s = pj["qi"][srows].reshape(db, N_IDX_HEADS, IDX_DIM)
        w_s = pj["kiwi"][srows, IDX_DIM:IDX_DIM + N_IDX_HEADS].reshape(db, N_IDX_HEADS, 1)
        ki_s = pj["kk"][srows, 0:IDX_DIM].reshape(db, 1, IDX_DIM)
        kn_s = pj["kb"][srows].reshape(db, 1, N_KV_HEADS * HEAD_DIM)
        vn_s = pj["vb"][srows].reshape(db, 1, N_KV_HEADS * HEAD_DIM)
        att_s = _attn_sample(page_table, l, cache_kit, cache_kt, cache_vt, qi_s, w_s, ki_s, q_s, kn_s, vn_s)
        ro_s, st_s = _gla_sample(pj["rq"][srows], pj["f"][srows], pj["rv"][srows], state_rec[l])

        pad_s = lambda a: jnp.pad(a, ((0, TOK_TILE - db), (0, 0)))
        att = jnp.concatenate([att_p, pad_s(att_s.astype(BF16))], axis=0)
        ro = jnp.concatenate([ro_p, pad_s(ro_s)], axis=0)
        y = _back(x_all, att, ro, pj["gr"], pj["ga"], pj["gb"], rec_norm[l].reshape(1, LANES).astype(F32),
                  w_pa[l].astype(BF16), w_pb[l].astype(BF16), w_o[l].astype(BF16))
        x_all = _ffn(y, norm_ffn[l].reshape(1, d).astype(F32), w_gu[l].astype(BF16), w_down[l].astype(BF16))

        seq_view = lambda a, wd: a[:n_prompt].reshape(batch, t_pad, wd)[:, :t_real]
        pk.append(seq_view(pj["kf"], 256).reshape(batch, t_real, N_KV_HEADS, HEAD_DIM))
        pv.append(seq_view(pj["vf"], 256).reshape(batch, t_real, N_KV_HEADS, HEAD_DIM))
        pki.append(seq_view(pj["kiwi"], LANES)[..., :IDX_DIM])
        ps.append(st_p)
        sk.append(pj["kf"][srows].reshape(db, 1, N_KV_HEADS, HEAD_DIM))
        sv.append(pj["vf"][srows].reshape(db, 1, N_KV_HEADS, HEAD_DIM))
        ski.append(pj["kiwi"][srows, :IDX_DIM].reshape(db, 1, IDX_DIM))
        ss.append(st_s)

    y_prompt = x_all[:n_prompt].reshape(batch, t_pad, d)[:, N_META:t_real]
    y_sample = x_all[srows].reshape(db, 1, d)
    return (y_prompt, y_sample, jnp.stack(pk), jnp.stack(pv), jnp.stack(pki), jnp.stack(ps),
            jnp.stack(sk), jnp.stack(sv), jnp.stack(ski), jnp.stack(ss))
```

```python
import functools

import jax
import jax.numpy as jnp
from jax import lax
from jax.experimental import pallas as pl
from jax.experimental.pallas import tpu as pltpu

F32 = jnp.float32
BF16 = jnp.bfloat16

N_META = 16
N_HEADS = 8
HEAD_DIM = 64
N_KV_HEADS = 4
ROT_DIM = 16
ROPE_THETA = 500000.0
N_IDX_HEADS = 8
IDX_DIM = 64
IDX_W_SCALE = (N_IDX_HEADS * IDX_DIM) ** -0.5
TOPK_MAX = 256
N_REC_HEADS = 4
REC_DIM = 128
EPS = 1e-6
Q_SCALE = HEAD_DIM ** -0.5 * 1.4426950408889634

LANES = 128
SUBLANES = 8

TOK_TILE = 384
Q_TILE = 128
KEY_GROUP = 4
REC_CHUNK = 64
PAGES_PER_STEP = 16
FFN_CHUNK = 256
VMEM_LIMIT = 56 * 1024 * 1024

NEG_SCORE = -3.0e38
NEG_HALF = -1.5e38
NEG_BIAS = -1.0e30
BIG = 3.0e38
BISECTIONS_PER_ROUND = 16
MAX_ROUNDS = 160

C_Q, C_K, C_V, C_QI, C_KIWI, C_RQ, C_FR, C_IR, C_GR, C_GA, C_GB, C_END = (
    0, 512, 768, 1024, 1536, 1664, 2176, 2688, 3200, 3712, 4736, 5760)

_NT = (((1,), (1,)), ((), ()))
_TN = (((0,), (0,)), ((), ()))


def _sigmoid(x):
    return 1.0 / (1.0 + jnp.exp(-x))


def _split3(x):
    hi = x.astype(BF16)
    r1 = x - hi.astype(F32)
    mid = r1.astype(BF16)
    lo = (r1 - mid.astype(F32)).astype(BF16)
    return jnp.concatenate([hi, mid, lo], axis=1)


def _join3(r):
    return r[:, 0:LANES] + r[:, LANES:2 * LANES] + r[:, 2 * LANES:3 * LANES]


def _inproj_body(x_ref, g_ref, w_ref, rope_ref, qn_ref, kn_ref, lb_ref,
                 q_ref, kf_ref, vf_ref, kb_ref, vb_ref, qi_ref, kiwi_ref, kk_ref,
                 rq_ref, f_ref, rv_ref, gr_ref, ga_ref, gb_ref):
    x = x_ref[...]
    h = x * lax.rsqrt(jnp.mean(x * x, axis=-1, keepdims=True) + EPS) * g_ref[...]
    hb = h.astype(BF16)
    rows = x.shape[0]
    lane = lax.broadcasted_iota(jnp.int32, (rows, LANES), 1)
    low = lane < HEAD_DIM
    cosv = rope_ref[:, 0:LANES]
    sin_up = rope_ref[:, LANES:2 * LANES]
    sin_dn = rope_ref[:, 2 * LANES:3 * LANES]

    def proj(c0, c1):
        return jnp.dot(hb, w_ref[:, c0:c1], preferred_element_type=F32)

    def rope(xs):
        return (xs * cosv + pltpu.roll(xs, ROT_DIM // 2, 1) * sin_up
                + pltpu.roll(xs, LANES - ROT_DIM // 2, 1) * sin_dn)

    def headnorm(xs, gain):
        sq = xs * xs
        s_lo = jnp.sum(jnp.where(low, sq, 0.0), axis=-1, keepdims=True)
        s_hi = jnp.sum(jnp.where(low, 0.0, sq), axis=-1, keepdims=True)
        ms = jnp.where(low, s_lo, s_hi) * (1.0 / HEAD_DIM)
        return xs * lax.rsqrt(ms + EPS) * gain

    qraw = proj(C_Q, C_K)
    for p in range(N_KV_HEADS):
        qs = rope(headnorm(qraw[:, p * LANES:(p + 1) * LANES], qn_ref[...])) * Q_SCALE
        qr = pltpu.roll(qs, HEAD_DIM, 1)
        if p % 2 == 0:
            h0, h1 = jnp.where(low, qs, 0.0), jnp.where(low, qr, 0.0)
        else:
            h0, h1 = jnp.where(low, 0.0, qr), jnp.where(low, 0.0, qs)
        q_ref[:, (2 * p) * LANES:(2 * p + 1) * LANES] = h0.astype(BF16)
        q_ref[:, (2 * p + 1) * LANES:(2 * p + 2) * LANES] = h1.astype(BF16)

    kraw = proj(C_K, C_V)
    for s in range(2):
        ks = rope(headnorm(kraw[:, s * LANES:(s + 1) * LANES], kn_ref[...]))
        kf_ref[:, s * LANES:(s + 1) * LANES] = ks
        kb_ref[:, s * LANES:(s + 1) * LANES] = ks.astype(BF16)

    vraw = proj(C_V, C_QI)
    vf_ref[...] = vraw
    vb_ref[...] = vraw.astype(BF16)

    qiraw = proj(C_QI, C_KIWI)
    for p in range(4):
        qi_ref[:, p * LANES:(p + 1) * LANES] = rope(qiraw[:, p * LANES:(p + 1) * LANES]).astype(BF16)

    kiwi = proj(C_KIWI, C_RQ)
    kir = rope(kiwi)
    kiwi_ref[...] = jnp.where(low, kir, kiwi * IDX_W_SCALE)
    kk_ref[...] = jnp.where(low, kir, pltpu.roll(kir, HEAD_DIM, 1)).astype(BF16)

    qr_ = proj(C_RQ, C_FR)
    rq_ref[...] = qr_ * _sigmoid(qr_)
    lb = lb_ref[...]
    f_ref[...] = lb + (1.0 - lb) * _sigmoid(proj(C_FR, C_IR))
    rv_ref[...] = proj(C_IR, C_GR)
    g_ = proj(C_GR, C_GA)
    gr_ref[...] = g_ * _sigmoid(g_)
    ga_ref[...] = _sigmoid(proj(C_GA, C_GB))
    gb_ref[...] = _sigmoid(proj(C_GB, C_END))


def _inproj(x_all, g, w, rope_tab, qn, kn, lb, tiles_per_seq, n_prompt_tiles):
    n_rows, d = x_all.shape
    n_tiles = n_rows // TOK_TILE
    row = lambda i: (i, 0)
    const = lambda i: (0, 0)
    rope_map = lambda i: (jnp.where(i < n_prompt_tiles, i % tiles_per_seq, tiles_per_seq), 0)
    widths = dict(q=(1024, BF16), kf=(256, F32), vf=(256, F32), kb=(256, BF16), vb=(256, BF16),
                  qi=(512, BF16), kiwi=(128, F32), kk=(128, BF16), rq=(512, F32), f=(512, F32),
                  rv=(512, F32), gr=(512, F32), ga=(1024, F32), gb=(1024, F32))
    out_shape = [jax.ShapeDtypeStruct((n_rows, wd), dt) for wd, dt in widths.values()]
    out_specs = [pl.BlockSpec((TOK_TILE, wd), row) for wd, _ in widths.values()]
    outs = pl.pallas_call(
        _inproj_body,
        grid=(n_tiles,),
        in_specs=[
            pl.BlockSpec((TOK_TILE, d), row),
            pl.BlockSpec((1, d), const),
            pl.BlockSpec((d, C_END), const, pipeline_mode=pl.Buffered(1)),
            pl.BlockSpec((TOK_TILE, 3 * LANES), rope_map),
            pl.BlockSpec((1, LANES), const),
            pl.BlockSpec((1, LANES), const),
            pl.BlockSpec((1, 512), const),
        ],
        out_specs=out_specs,
        out_shape=out_shape,
        compiler_params=pltpu.CompilerParams(
            dimension_semantics=("arbitrary",), vmem_limit_bytes=VMEM_LIMIT),
        name="inproj",
    )(x_all, g, w, rope_tab, qn, kn, lb)
    return dict(zip(widths.keys(), outs))


def _select_threshold(count_ge, bracket_minmax, lo0, hi0, n_adm, need, top_k):
    def halve(st):
        lo, hi, c_lo, c_hi, _ = st
        mid = 0.5 * (lo + hi)
        c_mid = count_ge(mid)
        ge = c_mid >= top_k
        lo2, c_lo2 = jnp.where(ge, mid, lo), jnp.where(ge, c_mid, c_lo)
        hi2, c_hi2 = jnp.where(ge, hi, mid), jnp.where(ge, c_hi, c_mid)
        r = top_k - c_hi2
        stop = (c_lo2 - c_hi2 == r) | (r == 1.0) | (mid <= lo) | (mid >= hi) | jnp.logical_not(need)
        return lo2, hi2, c_lo2, c_hi2, jnp.where(stop, 1.0, 0.0)

    def inner_cond(st):
        return (jnp.min(st[4]) < 0.5) & (st[5] < BISECTIONS_PER_ROUND)

    def inner_body(st):
        return halve(halve(st[:5])) + (st[5] + 2,)

    def outer_cond(st):
        return (jnp.min(st[4]) < 0.5) & (st[7] < MAX_ROUNDS)

    def outer_body(st):
        lo, hi, c_lo, c_hi, done = lax.while_loop(inner_cond, inner_body, st[:5] + (jnp.int32(0),))[:5]
        mn, mx = bracket_minmax(lo, hi)
        open_ = need & (done < 0.5)
        done = jnp.where(open_ & (mn == mx), 1.0, done)
        return jnp.where(open_, mn, lo), hi, c_lo, c_hi, done, mn, mx, st[7] + 1

    zero = jnp.zeros_like(lo0)
    lo, hi, c_lo, c_hi, _, mn, mx, _ = lax.while_loop(
        outer_cond, outer_body, (lo0, hi0, n_adm, zero, zero, zero, zero, jnp.int32(0)))
    r = top_k - c_hi
    thr = jnp.where(need, jnp.where(r == 1.0, mx, mn), NEG_SCORE)
    n_tie = jnp.where(need & (c_lo - c_hi != r), r, BIG)
    return thr, n_tie


def _attn_body(q_ref, qi_ref, wq_ref, kb_ref, vb_ref, kk_ref, o_ref,
               sc_ref, qia_ref, qa_ref, vt_ref, m_ref, l_ref, acc_ref,
               da_ref, db_ref, sa_ref, sb_ref, *, top_k, t_real):
    j = pl.program_id(1)
    nchunks = j + 1
    T = Q_TILE
    key_i = lax.broadcasted_iota(jnp.int32, (T, LANES), 0)
    qry_i = lax.broadcasted_iota(jnp.int32, (T, LANES), 1)
    low = qry_i < HEAD_DIM
    qpos = j * T + qry_i

    @pl.when(j == 0)
    def _():
        def transpose_v(c, carry):
            blk = vb_ref[pl.ds(pl.multiple_of(c * T, T), T), :].astype(F32)
            for s in range(2):
                vt_ref[c, s] = blk[:, s * LANES:(s + 1) * LANES].T.astype(BF16)
            return carry
        lax.fori_loop(0, vt_ref.shape[0], transpose_v, 0)

    wt = wq_ref[...].T
    qi = qi_ref[...]
    for h in range(N_IDX_HEADS):
        slab = qi[:, (h // 2) * LANES:(h // 2 + 1) * LANES]
        zero = jnp.zeros_like(slab)
        qia_ref[h * T:(h + 1) * T, :] = jnp.where(low, slab, zero) if h % 2 == 0 else jnp.where(low, zero, slab)
        qa_ref[h * T:(h + 1) * T, :] = q_ref[:, h * LANES:(h + 1) * LANES]

    G = KEY_GROUP
    ngroups = (nchunks + G - 1) // G
    last_chunk = vt_ref.shape[0] - 1

    def chunk_rows(ref, g, u, lanes):
        cc = jnp.minimum(g * G + u, last_chunk)
        return ref[pl.ds(pl.multiple_of(cc * T, T), T), lanes]

    def fold(x, op):
        out = x[0:T]
        for u in range(1, G):
            out = op(out, x[u * T:(u + 1) * T])
        return out

    npairs = ngroups // 2
    odd_tail = ngroups % 2 == 1

    def idx_dots(dst, g):
        kk = jnp.concatenate([chunk_rows(kk_ref, g, u, slice(None)) for u in range(G)], axis=0)
        dst[...] = lax.dot_general(kk, qia_ref[...], _NT, preferred_element_type=F32)

    def idx_scores(src, g, carry):
        mn, mx = carry
        s = jnp.zeros((G * T, LANES), F32)
        for h in range(N_IDX_HEADS):
            s = s + wt[IDX_DIM + h:IDX_DIM + h + 1, :] * jnp.maximum(src[:, h * T:(h + 1) * T], 0.0)
        kpos = g * (G * T) + lax.broadcasted_iota(jnp.int32, (G * T, LANES), 0)
        adm = kpos <= j * T + lax.broadcasted_iota(jnp.int32, (G * T, LANES), 1)
        sc_ref[pl.ds(pl.multiple_of(g * (G * T), G * T), G * T), :] = jnp.where(adm, s, NEG_SCORE)
        return (jnp.minimum(mn, fold(jnp.where(adm, s, BIG), jnp.minimum)),
                jnp.maximum(mx, fold(jnp.where(adm, s, -BIG), jnp.maximum)))

    def phase_a(i, carry):
        idx_dots(db_ref, 2 * i + 1)
        carry = idx_scores(da_ref, 2 * i, carry)
        idx_dots(da_ref, 2 * i + 2)
        return idx_scores(db_ref, 2 * i + 1, carry)

    idx_dots(da_ref, 0)
    rng = lax.fori_loop(0, npairs, phase_a,
                        (jnp.full((T, LANES), BIG, F32), jnp.full((T, LANES), -BIG, F32)))
    mn, mx = lax.cond(odd_tail, lambda c: idx_scores(da_ref, ngroups - 1, c), lambda c: c, rng)

    def sc_group(g):
        return sc_ref[pl.ds(pl.multiple_of(g * (G * T), G * T), G * T), :]

    def count_ge(x):
        def step(g, acc):
            return acc + fold(jnp.where(sc_group(g) >= x, 1.0, 0.0), jnp.add)
        return jnp.sum(lax.fori_loop(0, ngroups, step, jnp.zeros((T, LANES), F32)), axis=0, keepdims=True)

    def bracket_minmax(lo, hi):
        def step(g, carry):
            a, b = carry
            s = sc_group(g)
            inb = (s >= lo) & (s < hi)
            return (jnp.minimum(a, fold(jnp.where(inb, s, BIG), jnp.minimum)),
                    jnp.maximum(b, fold(jnp.where(inb, s, -BIG), jnp.maximum)))
        a, b = lax.fori_loop(0, ngroups, step,
                             (jnp.full((T, LANES), BIG, F32), jnp.full((T, LANES), -BIG, F32)))
        return jnp.min(a, axis=0, keepdims=True), jnp.max(b, axis=0, keepdims=True)

    qrow = j * T + lax.broadcasted_iota(jnp.int32, (1, LANES), 1)
    n_adm = (qrow + 1).astype(F32)
    need = (qrow + 1 > top_k) & (qrow < t_real)
    top = jnp.max(mx, axis=0, keepdims=True)
    thr, n_tie = _select_threshold(count_ge, bracket_minmax, jnp.min(mn, axis=0, keepdims=True),
                                   top + (jnp.abs(top) + 1.0), n_adm, need, float(top_k))

    m_ref[...] = jnp.full(m_ref.shape, NEG_SCORE, F32)
    l_ref[...] = jnp.zeros(l_ref.shape, F32)
    acc_ref[...] = jnp.zeros(acc_ref.shape, F32)
    earlier = jnp.where(qry_i < key_i, 1.0, 0.0).astype(BF16)

    def qk_dots(dst, g):
        for sl in range(2):
            lanes = slice(sl * LANES, (sl + 1) * LANES)
            kc = jnp.concatenate([chunk_rows(kb_ref, g, u, lanes) for u in range(G)], axis=0)
            dst[sl] = lax.dot_general(kc, qa_ref[sl * 4 * T:(sl + 1) * 4 * T, :], _NT,
                                      preferred_element_type=F32)

    def softmax_pv(src, g, seen, issue_next):
        s = sc_group(g)
        eq = s == thr
        eqf = jnp.where(eq, 1.0, 0.0)
        eq_l = jnp.concatenate([eqf[u * T:(u + 1) * T] for u in range(G)], axis=1).astype(BF16)
        in_chunk = jnp.dot(earlier, eq_l, preferred_element_type=F32)
        ranks = []
        for u in range(G):
            ranks.append(in_chunk[:, u * T:(u + 1) * T] + seen)
            seen = seen + jnp.sum(eqf[u * T:(u + 1) * T], axis=0, keepdims=True)
        rank = jnp.concatenate(ranks, axis=0)
        keep_tie = jnp.where(eq, jnp.where(rank < n_tie, 0.0, NEG_BIAS), NEG_BIAS)
        bias = jnp.where(s > NEG_HALF, jnp.where(s > thr, 0.0, keep_tie), NEG_BIAS)
        bias4 = jnp.concatenate([bias] * 4, axis=1)
        sts, m_news, alphas = [], [], []
        for sl in range(2):
            st = src[sl] + bias4
            m_old = m_ref[sl]
            m_new = jnp.maximum(m_old, jnp.max(st, axis=0, keepdims=True))
            m_ref[sl] = m_new
            sts.append(st)
            m_news.append(m_new)
            alphas.append(jnp.exp2(m_old - m_new))
        issue_next()
        for sl in range(2):
            p = jnp.exp2(sts[sl] - m_news[sl])
            l_ref[sl] = alphas[sl] * l_ref[sl] + jnp.sum(p, axis=0, keepdims=True)
            vt = jnp.concatenate([vt_ref[jnp.minimum(g * G + u, last_chunk), sl] for u in range(G)], axis=1)
            acc_ref[sl] = alphas[sl] * acc_ref[sl] + jnp.dot(vt, p.astype(BF16), preferred_element_type=F32)
        return seen

    def phase_c(i, seen):
        seen = softmax_pv(sa_ref, 2 * i, seen, lambda: qk_dots(sb_ref, 2 * i + 1))
        return softmax_pv(sb_ref, 2 * i + 1, seen, lambda: qk_dots(sa_ref, 2 * i + 2))

    qk_dots(sa_ref, 0)
    seen = lax.fori_loop(0, npairs, phase_c, jnp.zeros((1, LANES), F32))

    @pl.when(odd_tail)
    def _():
        softmax_pv(sa_ref, ngroups - 1, seen, lambda: None)

    for p in range(N_KV_HEADS):
        sl, hh = (2 * p) // 4, (2 * p) % 4
        out = acc_ref[sl] / l_ref[sl]
        a0 = out[:, hh * T:(hh + 1) * T].T
        a1 = out[:, (hh + 1) * T:(hh + 2) * T].T
        if p % 2 == 0:
            slab = jnp.where(low, a0, pltpu.roll(a1, HEAD_DIM, 1))
        else:
            slab = jnp.where(low, pltpu.roll(a0, HEAD_DIM, 1), a1)
        o_ref[:, p * LANES:(p + 1) * LANES] = slab.astype(BF16)


def _attn_prompt(pj, batch, t_pad, t_real, top_k):
    nq = t_pad // Q_TILE
    grp_rows = KEY_GROUP * Q_TILE
    n_grp = 2 * -(-(-(-nq // KEY_GROUP)) // 2)
    qrow = lambda b, j: (b * nq + j, 0)
    seq = lambda b, j: (b, 0)
    return pl.pallas_call(
        functools.partial(_attn_body, top_k=top_k, t_real=t_real),
        grid=(batch, nq),
        in_specs=[
            pl.BlockSpec((Q_TILE, 1024), qrow),
            pl.BlockSpec((Q_TILE, 512), qrow),
            pl.BlockSpec((Q_TILE, LANES), qrow),
            pl.BlockSpec((t_pad, 256), seq),
            pl.BlockSpec((t_pad, 256), seq),
            pl.BlockSpec((t_pad, LANES), seq),
        ],
        out_specs=pl.BlockSpec((Q_TILE, 512), qrow),
        out_shape=jax.ShapeDtypeStruct((batch * t_pad, 512), BF16),
        scratch_shapes=[
            pltpu.VMEM((n_grp * grp_rows, LANES), F32),
            pltpu.VMEM((N_IDX_HEADS * Q_TILE, LANES), BF16),
            pltpu.VMEM((N_HEADS * Q_TILE, LANES), BF16),
            pltpu.VMEM((nq, 2, LANES, Q_TILE), BF16),
            pltpu.VMEM((2, 1, 4 * Q_TILE), F32),
            pltpu.VMEM((2, 1, 4 * Q_TILE), F32),
            pltpu.VMEM((2, LANES, 4 * Q_TILE), F32),
            pltpu.VMEM((grp_rows, N_IDX_HEADS * Q_TILE), F32),
            pltpu.VMEM((grp_rows, N_IDX_HEADS * Q_TILE), F32),
            pltpu.VMEM((2, grp_rows, 4 * Q_TILE), F32),
            pltpu.VMEM((2, grp_rows, 4 * Q_TILE), F32),
        ],
        compiler_params=pltpu.CompilerParams(
            dimension_semantics=("arbitrary", "arbitrary"), vmem_limit_bytes=VMEM_LIMIT),
        name="attn_prompt",
    )(pj["q"], pj["qi"], pj["kiwi"], pj["kb"], pj["vb"], pj["kk"])


def _gla_body(rq_ref, f_ref, rv_ref, cum_ref, o_ref, sout_ref, st_ref, *, t_real):
    c = pl.program_id(1)
    n_c = pl.num_programs(1)
    C = REC_CHUNK
    n_lev = C.bit_length() - 1

    @pl.when(c == 0)
    def _():
        st_ref[...] = jnp.zeros(st_ref.shape, F32)

    trow = lax.broadcasted_iota(jnp.int32, (C, LANES), 0)
    valid = c * C + trow < t_real
    ti = lax.broadcasted_iota(jnp.int32, (C, C), 0)
    si = lax.broadcasted_iota(jnp.int32, (C, C), 1)
    H = range(N_REC_HEADS)
    sls = [slice(h * LANES, (h + 1) * LANES) for h in H]
    f = [f_ref[:, sl] for sl in sls]
    q = [rq_ref[:, sl] for sl in sls]
    vb = [rv_ref[:, sl].astype(BF16) for sl in sls]
    kk = [jnp.where(valid, 1.0 - f[h], 0.0) for h in H]

    lf3 = jnp.concatenate([_split3(jnp.where(valid, jnp.log(f[h]), 0.0)) for h in H], axis=1)
    cums = jnp.dot(cum_ref[...], lf3, preferred_element_type=F32)
    cum = [_join3(cums[:, h * 3 * LANES:(h + 1) * 3 * LANES]) for h in H]
    b = [cum[h][0:C] for h in H]

    a = [jnp.where(ti == si, lax.dot_general(q[h].astype(BF16), kk[h].astype(BF16), _NT,
                                             preferred_element_type=F32), 0.0) for h in H]
    for lev in range(n_lev):
        half = 1 << lev
        up = (trow & half) != 0
        blk_mask = ((ti >> (lev + 1)) == (si >> (lev + 1))) & ((ti & half) != 0) & ((si & half) == 0)
        for h in H:
            bl = cum[h][(1 + lev) * C:(2 + lev) * C]
            e = jnp.exp(jnp.where(up, b[h] - bl, bl - b[h]))
            qt = jnp.where(up, q[h] * e, 0.0).astype(BF16)
            kt = jnp.where(up, 0.0, kk[h] * e).astype(BF16)
            a[h] = a[h] + jnp.where(blk_mask, lax.dot_general(qt, kt, _NT, preferred_element_type=F32), 0.0)

    for h in H:
        st = st_ref[h]
        qe = (q[h] * jnp.exp(b[h])).astype(BF16)
        o = jnp.dot(a[h].astype(BF16), vb[h], preferred_element_type=F32)
        o_ref[:, sls[h]] = o + lax.dot_general(qe, st.astype(BF16), _NT, preferred_element_type=F32)
        b_last = b[h][C - 1:C, :]
        kd = (kk[h] * jnp.exp(b_last - b[h])).astype(BF16)
        st_new = st * jnp.exp(b_last) + lax.dot_general(vb[h], kd, _TN, preferred_element_type=F32)
        st_ref[h] = st_new

        @pl.when(c == n_c - 1)
        def _(h=h, st_new=st_new):
            sout_ref[0, h] = st_new.T


def _gla_consts():
    C = REC_CHUNK
    n_lev = C.bit_length() - 1
    t = jnp.arange(C)
    tri = t[None, :] <= t[:, None]
    blocks = [tri]
    for lev in range(n_lev):
        half = 1 << lev
        mid_row = (t >> (lev + 1)) * (2 * half) + half - 1
        blocks.append(t[None, :] <= mid_row[:, None])
    return jnp.concatenate(blocks, axis=0).astype(BF16)


def _gla_prompt(pj, batch, t_pad, t_real):
    nch = t_pad // REC_CHUNK
    cum = _gla_consts()
    row = lambda b, c: (b * nch + c, 0)
    const = lambda b, c: (0, 0)
    return pl.pallas_call(
        functools.partial(_gla_body, t_real=t_real),
        grid=(batch, nch),
        in_specs=[
            pl.BlockSpec((REC_CHUNK, 512), row),
            pl.BlockSpec((REC_CHUNK, 512), row),
            pl.BlockSpec((REC_CHUNK, 512), row),
            pl.BlockSpec(cum.shape, const),
        ],
        out_specs=[
            pl.BlockSpec((REC_CHUNK, 512), row),
            pl.BlockSpec((1, N_REC_HEADS, REC_DIM, REC_DIM), lambda b, c: (b, 0, 0, 0)),
        ],
        out_shape=[
            jax.ShapeDtypeStruct((batch * t_pad, 512), F32),
            jax.ShapeDtypeStruct((batch, N_REC_HEADS, REC_DIM, REC_DIM), F32),
        ],
        scratch_shapes=[pltpu.VMEM((N_REC_HEADS, REC_DIM, REC_DIM), F32)],
        compiler_params=pltpu.CompilerParams(
            dimension_semantics=("arbitrary", "arbitrary"), vmem_limit_bytes=VMEM_LIMIT),
        name="gla_prompt",
    )(pj["rq"], pj["f"], pj["rv"], cum)


def _gla_sample_body(rq_ref, f_ref, rv_ref, s0_ref, o_ref, sout_ref):
    b = pl.program_id(0)
    q = rq_ref[pl.ds(b, 1), :]
    f = f_ref[pl.ds(b, 1), :]
    v = rv_ref[pl.ds(b, 1), :]
    for h in range(N_REC_HEADS):
        sl = slice(h * LANES, (h + 1) * LANES)
        col = lambda r: jnp.broadcast_to(r[:, sl], (REC_DIM, LANES)).T
        s_new = col(f) * s0_ref[0, h] + col(1.0 - f) * v[:, sl]
        sout_ref[0, h] = s_new
        o_ref[0, :, sl] = jnp.sum(col(q) * s_new, axis=0, keepdims=True)


def _gla_sample(rq, f, rv, state):
    db = state.shape[0]
    full = lambda b: (0, 0)
    st = lambda b: (b, 0, 0, 0)
    o, s = pl.pallas_call(
        _gla_sample_body,
        grid=(db,),
        in_specs=[
            pl.BlockSpec(rq.shape, full), pl.BlockSpec(f.shape, full), pl.BlockSpec(rv.shape, full),
            pl.BlockSpec((1, N_REC_HEADS, REC_DIM, REC_DIM), st),
        ],
        out_specs=[
            pl.BlockSpec((1, 1, 512), lambda b: (b, 0, 0)),
            pl.BlockSpec((1, N_REC_HEADS, REC_DIM, REC_DIM), st),
        ],
        out_shape=[
            jax.ShapeDtypeStruct((db, 1, 512), F32),
            jax.ShapeDtypeStruct(state.shape, F32),
        ],
        compiler_params=pltpu.CompilerParams(dimension_semantics=("arbitrary",)),
        name="gla_sample",
    )(rq, f, rv, state)
    return o.reshape(db, 512), s


def _attn_sample_body(pt_ref, *refs, n_pages, top_k):
    P = PAGES_PER_STEP
    ki_refs, k_refs, v_refs = refs[0:P], refs[P:2 * P], refs[2 * P:3 * P]
    (qi_ref, w_ref, kin_ref, q_ref, kn_ref, vn_ref,
     o_ref, sc_ref, m_ref, l_ref, acc_ref) = refs[3 * P:]
    g = pl.program_id(1)
    n_groups = n_pages // P
    n_rows = sc_ref.shape[0]
    qi = qi_ref[0]
    w = w_ref[0]

    @pl.when(g < n_groups)
    def _():
        rows = []
        for i in range(P):
            d = jnp.dot(qi, ki_refs[i][...].astype(BF16), preferred_element_type=F32)
            rows.append(jnp.sum(w * jnp.maximum(d, 0.0), axis=0, keepdims=True))
        sc_ref[pl.ds(pl.multiple_of(g * P, P), P), :] = jnp.concatenate(rows, axis=0)

    @pl.when(g == n_groups - 1)
    def _():
        d_new = jnp.sum(qi.astype(F32) * kin_ref[0].astype(F32), axis=1, keepdims=True)
        s_new = jnp.sum(w * jnp.maximum(d_new, 0.0), axis=0, keepdims=True)
        tail = lax.broadcasted_iota(jnp.int32, (n_rows - n_pages, LANES), 0) * LANES + \
            lax.broadcasted_iota(jnp.int32, (n_rows - n_pages, LANES), 1)
        sc_ref[n_pages:n_rows, :] = jnp.where(tail == 0, s_new, NEG_SCORE)

        s = sc_ref[...]
        adm = s > NEG_HALF
        top = jnp.max(s, keepdims=True)

        def count_ge(x):
            return jnp.sum(jnp.where(sc_ref[...] >= x, 1.0, 0.0), keepdims=True)

        def bracket_minmax(lo, hi):
            v = sc_ref[...]
            inb = (v >= lo) & (v < hi)
            return (jnp.min(jnp.where(inb, v, BIG), keepdims=True),
                    jnp.max(jnp.where(inb, v, -BIG), keepdims=True))

        n_adm = jnp.full((1, 1), float(n_pages * LANES + 1), F32)
        thr, n_tie = _select_threshold(count_ge, bracket_minmax, jnp.min(jnp.where(adm, s, BIG), keepdims=True),
                                       top + (jnp.abs(top) + 1.0), n_adm, n_adm > top_k, float(top_k))
        eqb = jnp.where(s == thr, 1.0, 0.0).astype(BF16)
        ri = lax.broadcasted_iota(jnp.int32, (LANES, LANES), 0)
        ci = lax.broadcasted_iota(jnp.int32, (LANES, LANES), 1)
        in_row = jnp.dot(eqb, jnp.where(ri < ci, 1.0, 0.0).astype(BF16), preferred_element_type=F32)
        row_tot = jnp.dot(eqb, jnp.ones((LANES, LANES), BF16), preferred_element_type=F32)
        rr = lax.broadcasted_iota(jnp.int32, (n_rows, n_rows), 0)
        rc = lax.broadcasted_iota(jnp.int32, (n_rows, n_rows), 1)
        before = jnp.dot(jnp.where(rc < rr, 1.0, 0.0).astype(BF16), row_tot.astype(BF16),
                         preferred_element_type=F32)
        rank = in_row + before
        keep = (s > thr) | ((s == thr) & (rank < n_tie))
        sc_ref[...] = jnp.where(keep & adm, 1.0, 0.0)

        q = q_ref[0].astype(F32)
        s_own = jnp.sum(q * kn_ref[0].astype(F32), axis=1, keepdims=True)
        own_sel = sc_ref[n_pages:n_pages + 1, 0:1]
        m_ref[...] = jnp.broadcast_to(s_own + jnp.where(own_sel > 0.5, 0.0, NEG_BIAS), m_ref.shape)
        l_ref[...] = jnp.ones(l_ref.shape, F32)
        acc_ref[...] = jnp.broadcast_to(vn_ref[0].astype(F32), acc_ref.shape)

    @pl.when(g >= n_groups)
    def _():
        q = q_ref[0]
        sel = sc_ref[pl.ds(pl.multiple_of((g - n_groups) * P, P), P), :]
        scores = []
        for i in range(P):
            s = jnp.dot(q, k_refs[i][...].astype(BF16), preferred_element_type=F32)
            scores.append(jnp.where(sel[i:i + 1, :] > 0.5, s, NEG_BIAS))
        s_all = jnp.concatenate(scores, axis=1)
        m_old = m_ref[...]
        m_new = jnp.maximum(m_old, jnp.max(s_all, axis=1, keepdims=True))
        alpha = jnp.exp2(m_old - m_new)
        p32 = jnp.exp2(s_all - m_new[:, 0:1])
        l_ref[...] = alpha * l_ref[...] + jnp.sum(p32, axis=1, keepdims=True)
        m_ref[...] = m_new
        p = p32.astype(BF16)
        pv = jnp.zeros(acc_ref.shape, F32)
        for i in range(P):
            pv = pv + lax.dot_general(p[:, i * LANES:(i + 1) * LANES], v_refs[i][...].astype(BF16), _NT,
                                      preferred_element_type=F32)
        acc_ref[...] = alpha[:, 0:1] * acc_ref[...] + pv

        @pl.when(g == 2 * n_groups - 1)
        def _():
            o_ref[0] = acc_ref[...] / l_ref[:, 0:1]


def _attn_sample(page_table, layer, cache_kit, cache_kt, cache_vt, qi, w, ki_new, q, k_new, v_new):
    db, n_pages = page_table.shape
    P = PAGES_PER_STEP
    n_groups = n_pages // P
    page = cache_kit.shape[3]
    kv_w = N_KV_HEADS * HEAD_DIM
    top_k = min(TOPK_MAX, (n_pages * page + 1) // 4)
    n_rows = -(-(n_pages + 1) // SUBLANES) * SUBLANES

    def ki_map(i):
        return lambda b, g, pt: (layer, pt[b * n_pages + jnp.minimum(g, n_groups - 1) * P + i], 0, 0)

    def kv_map(i):
        return lambda b, g, pt: (layer, pt[b * n_pages + jnp.maximum(g - n_groups, 0) * P + i], 0, 0)

    per_b = lambda b, g, pt: (b, 0, 0)
    in_specs = (
        [pl.BlockSpec((None, None, IDX_DIM, page), ki_map(i)) for i in range(P)]
        + [pl.BlockSpec((None, None, kv_w, page), kv_map(i)) for i in range(P)]
        + [pl.BlockSpec((None, None, kv_w, page), kv_map(i)) for i in range(P)]
        + [pl.BlockSpec((1,) + a.shape[1:], per_b) for a in (qi, w, ki_new, q, k_new, v_new)])
    out = pl.pallas_call(
        functools.partial(_attn_sample_body, n_pages=n_pages, top_k=top_k),
        grid_spec=pltpu.PrefetchScalarGridSpec(
            num_scalar_prefetch=1,
            grid=(db, 2 * n_groups),
            in_specs=in_specs,
            out_specs=pl.BlockSpec((1, N_HEADS, kv_w), per_b),
            scratch_shapes=[
                pltpu.VMEM((n_rows, LANES), F32),
                pltpu.VMEM((N_HEADS, LANES), F32),
                pltpu.VMEM((N_HEADS, LANES), F32),
                pltpu.VMEM((N_HEADS, kv_w), F32),
            ]),
        out_shape=jax.ShapeDtypeStruct((db, N_HEADS, kv_w), F32),
        compiler_params=pltpu.CompilerParams(
            dimension_semantics=("arbitrary", "arbitrary"), vmem_limit_bytes=VMEM_LIMIT),
        name="attn_sample",
    )(page_table.reshape(-1), *([cache_kit] * P), *([cache_kt] * P), *([cache_vt] * P),
      qi, w, ki_new, q, k_new, v_new)
    return jnp.concatenate([out[:, h, (h // 2) * HEAD_DIM:(h // 2 + 1) * HEAD_DIM] for h in range(N_HEADS)], axis=1)


def _back_body(x_ref, att_ref, ro_ref, gr_ref, ga_ref, gb_ref, grec_ref, wpa_ref, wpb_ref, wo_ref, y_ref):
    ro = ro_ref[...]
    gr = gr_ref[...]
    recs = []
    for h in range(N_REC_HEADS):
        r = ro[:, h * LANES:(h + 1) * LANES]
        r = r * lax.rsqrt(jnp.mean(r * r, axis=-1, keepdims=True) + EPS) * grec_ref[...]
        recs.append((r * gr[:, h * LANES:(h + 1) * LANES]).astype(BF16))
    rec = jnp.concatenate(recs, axis=1)
    a = jnp.dot(att_ref[...], wpa_ref[...], preferred_element_type=F32)
    b = jnp.dot(rec, wpb_ref[...], preferred_element_type=F32)
    mix = (ga_ref[...] * a + gb_ref[...] * b).astype(BF16)
    y_ref[...] = x_ref[...] + jnp.dot(mix, wo_ref[...], preferred_element_type=F32)


def _back(x_all, att, ro, gr, ga, gb, grec, wpa, wpb, wo):
    n_rows, d = x_all.shape
    row = lambda i: (i, 0)
    const = lambda i: (0, 0)
    return pl.pallas_call(
        _back_body,
        grid=(n_rows // TOK_TILE,),
        in_specs=[
            pl.BlockSpec((TOK_TILE, d), row), pl.BlockSpec((TOK_TILE, 512), row),
            pl.BlockSpec((TOK_TILE, 512), row), pl.BlockSpec((TOK_TILE, 512), row),
            pl.BlockSpec((TOK_TILE, d), row), pl.BlockSpec((TOK_TILE, d), row),
            pl.BlockSpec((1, LANES), const),
            pl.BlockSpec(wpa.shape, const), pl.BlockSpec(wpb.shape, const), pl.BlockSpec(wo.shape, const),
        ],
        out_specs=pl.BlockSpec((TOK_TILE, d), row),
        out_shape=jax.ShapeDtypeStruct((n_rows, d), F32),
        compiler_params=pltpu.CompilerParams(
            dimension_semantics=("arbitrary",), vmem_limit_bytes=VMEM_LIMIT),
        name="mixer_back",
    )(x_all, att, ro, gr, ga, gb, grec, wpa, wpb, wo)


def _ffn_body(y_ref, g_ref, wgu_ref, wd_ref, o_ref, *, d_ff):
    y = y_ref[...]
    hb = (y * lax.rsqrt(jnp.mean(y * y, axis=-1, keepdims=True) + EPS) * g_ref[...]).astype(BF16)
    acc = y
    for c in range(d_ff // FFN_CHUNK):
        c0 = c * FFN_CHUNK
        gate = jnp.dot(hb, wgu_ref[:, c0:c0 + FFN_CHUNK], preferred_element_type=F32)
        up = jnp.dot(hb, wgu_ref[:, d_ff + c0:d_ff + c0 + FFN_CHUNK], preferred_element_type=F32)
        act = (gate * _sigmoid(gate) * up).astype(BF16)
        acc = acc + jnp.dot(act, wd_ref[c0:c0 + FFN_CHUNK, :], preferred_element_type=F32)
    o_ref[...] = acc


def _ffn(y, g, wgu, wd):
    n_rows, d = y.shape
    d_ff = wd.shape[0]
    row = lambda i: (i, 0)
    const = lambda i: (0, 0)
    return pl.pallas_call(
        functools.partial(_ffn_body, d_ff=d_ff),
        grid=(n_rows // TOK_TILE,),
        in_specs=[
            pl.BlockSpec((TOK_TILE, d), row), pl.BlockSpec((1, d), const),
            pl.BlockSpec(wgu.shape, const, pipeline_mode=pl.Buffered(1)),
            pl.BlockSpec(wd.shape, const, pipeline_mode=pl.Buffered(1)),
        ],
        out_specs=pl.BlockSpec((TOK_TILE, d), row),
        out_shape=jax.ShapeDtypeStruct((n_rows, d), F32),
        compiler_params=pltpu.CompilerParams(
            dimension_semantics=("arbitrary",), vmem_limit_bytes=VMEM_LIMIT),
        name="ffn",
    )(y, g, wgu, wd)


def _rope_table(pos):
    half = ROT_DIM // 2
    inv = jnp.power(ROPE_THETA, -2.0 * jnp.arange(half, dtype=F32) / ROT_DIM)
    ang = pos.astype(F32)[:, None] * inv[None, :]
    cos, sin = jnp.cos(ang), jnp.sin(ang)
    n = pos.shape[0]
    one = jnp.ones((n, HEAD_DIM - ROT_DIM), F32)
    zero8 = jnp.zeros((n, half), F32)
    zero = jnp.zeros((n, HEAD_DIM - ROT_DIM), F32)
    c64 = jnp.concatenate([cos, cos, one], axis=1)
    up64 = jnp.concatenate([zero8, sin, zero], axis=1)
    dn64 = jnp.concatenate([-sin, zero8, zero], axis=1)
    return jnp.concatenate([c64, c64, up64, up64, dn64, dn64], axis=1)


def _pack_w_in(w):
    d = w.shape[0]
    cuts = (512, 768, 1024, 1536, 1600, 1608, 2120, 2632, 3144, 3656, 4680)
    q, k, v, qi, ki, wi, qr, fr, ir, gr, ga, gb = jnp.split(w, cuts, axis=1)
    pad = jnp.zeros((d, LANES - IDX_DIM - N_IDX_HEADS), w.dtype)
    return jnp.concatenate([q, k, v, qi, ki, wi, pad, qr, fr, ir, gr, ga, gb], axis=1).astype(BF16)


def kernel(x_prompt, x_sample, cache_k, cache_v, cache_idx_k, state_rec, page_table, meta_tokens,
           w_in, norm_mix, q_norm, k_norm, lb_raw, rec_norm, w_pa, w_pb, w_o, norm_ffn, w_gu, w_down):
    batch, seq, d = x_prompt.shape
    db = x_sample.shape[0]
    depth = w_in.shape[0]
    n_pages = page_table.shape[1]
    page = cache_k.shape[2]
    past_len = n_pages * page
    t_real = seq + N_META
    t_pad = -(-t_real // TOK_TILE) * TOK_TILE
    tiles_per_seq = t_pad // TOK_TILE
    n_prompt = batch * t_pad
    top_k = min(TOPK_MAX, seq // 4)

    sm = jax.nn.softmax(lb_raw.astype(F32), axis=0)
    lower_bounds = jnp.cumsum(sm, axis=0) - sm[0:1]

    xp = jnp.concatenate([jnp.broadcast_to(meta_tokens[None].astype(F32), (batch, N_META, d)), x_prompt], axis=1)
    xp = jnp.pad(xp, ((0, 0), (0, t_pad - t_real), (0, 0))).reshape(n_prompt, d)
    xs = jnp.pad(x_sample.reshape(db, d), ((0, TOK_TILE - db), (0, 0)))
    x_all = jnp.concatenate([xp, xs], axis=0)

    rope_tab = jnp.concatenate([
        _rope_table(jnp.arange(t_pad, dtype=jnp.int32)),
        _rope_table(jnp.full((TOK_TILE,), past_len, jnp.int32))], axis=0)

    n_pool = cache_k.shape[1]
    cache_kt = jnp.transpose(cache_k, (0, 1, 3, 4, 2)).reshape(depth, n_pool, N_KV_HEADS * HEAD_DIM, page)
    cache_vt = jnp.transpose(cache_v, (0, 1, 3, 4, 2)).reshape(depth, n_pool, N_KV_HEADS * HEAD_DIM, page)
    cache_kit = jnp.transpose(cache_idx_k, (0, 1, 3, 2))
    srows = slice(n_prompt, n_prompt + db)
    tile2 = lambda g_: jnp.concatenate([g_, g_]).reshape(1, LANES).astype(F32)

    pk, pv, pki, ps, sk, sv, ski, ss = [], [], [], [], [], [], [], []
    for l in range(depth):
        pj = _inproj(x_all, norm_mix[l].reshape(1, d).astype(F32), _pack_w_in(w_in[l]), rope_tab,
                     tile2(q_norm[l]), tile2(k_norm[l]), lower_bounds[l].reshape(1, -1),
                     tiles_per_seq, n_prompt // TOK_TILE)

        att_p = _attn_prompt(pj, batch, t_pad, t_real, top_k)
        ro_p, st_p = _gla_prompt(pj, batch, t_pad, t_real)

        q_s = pj["q"][srows].reshape(db, N_HEADS, LANES)
        zero_s = jnp.zeros_like(q_s)
        q_s = jnp.stack([jnp.concatenate([q_s[:, h] if h // 4 == s else zero_s[:, h] for s in range(2)], axis=1)
                         for h in range(N_HEADS)], axis=1)
        qi_s = pj["qi"][srows].reshape(db, N_IDX_HEADS, IDX_DIM)
        w_s = pj["kiwi"][srows, IDX_DIM:IDX_DIM + N_IDX_HEADS].reshape(db, N_IDX_HEADS, 1)
        ki_s = pj["kk"][srows, 0:IDX_DIM].reshape(db, 1, IDX_DIM)
        kn_s = pj["kb"][srows].reshape(db, 1, N_KV_HEADS * HEAD_DIM)
        vn_s = pj["vb"][srows].reshape(db, 1, N_KV_HEADS * HEAD_DIM)
        att_s = _attn_sample(page_table, l, cache_kit, cache_kt, cache_vt, qi_s, w_s, ki_s, q_s, kn_s, vn_s)
        ro_s, st_s = _gla_sample(pj["rq"][srows], pj["f"][srows], pj["rv"][srows], state_rec[l])

        pad_s = lambda a: jnp.pad(a, ((0, TOK_TILE - db), (0, 0)))
        att = jnp.concatenate([att_p, pad_s(att_s.astype(BF16))], axis=0)
        ro = jnp.concatenate([ro_p, pad_s(ro_s)], axis=0)
        y = _back(x_all, att, ro, pj["gr"], pj["ga"], pj["gb"], rec_norm[l].reshape(1, LANES).astype(F32),
                  w_pa[l].astype(BF16), w_pb[l].astype(BF16), w_o[l].astype(BF16))
        x_all = _ffn(y, norm_ffn[l].reshape(1, d).astype(F32), w_gu[l].astype(BF16), w_down[l].astype(BF16))

        seq_view = lambda a, wd: a[:n_prompt].reshape(batch, t_pad, wd)[:, :t_real]
        pk.append(seq_view(pj["kf"], 256).reshape(batch, t_real, N_KV_HEADS, HEAD_DIM))
        pv.append(seq_view(pj["vf"], 256).reshape(batch, t_real, N_KV_HEADS, HEAD_DIM))
        pki.append(seq_view(pj["kiwi"], LANES)[..., :IDX_DIM])
        ps.append(st_p)
        sk.append(pj["kf"][srows].reshape(db, 1, N_KV_HEADS, HEAD_DIM))
        sv.append(pj["vf"][srows].reshape(db, 1, N_KV_HEADS, HEAD_DIM))
        ski.append(pj["kiwi"][srows, :IDX_DIM].reshape(db, 1, IDX_DIM))
        ss.append(st_s)

    y_prompt = x_all[:n_prompt].reshape(batch, t_pad, d)[:, N_META:t_real]
    y_sample = x_all[srows].reshape(db, 1, d)
    return (y_prompt, y_sample, jnp.stack(pk), jnp.stack(pv), jnp.stack(pki), jnp.stack(ps),
            jnp.stack(sk), jnp.stack(sv), jnp.stack(ski), jnp.stack(ss))
```

```python
import functools

import jax
import jax.numpy as jnp
from jax import lax
from jax.experimental import pallas as pl
from jax.experimental.pallas import tpu as pltpu

F32 = jnp.float32
BF16 = jnp.bfloat16

N_META = 16
N_HEADS = 8
HEAD_DIM = 64
N_KV_HEADS = 4
ROT_DIM = 16
ROPE_THETA = 500000.0
N_IDX_HEADS = 8
IDX_DIM = 64
IDX_W_SCALE = (N_IDX_HEADS * IDX_DIM) ** -0.5
TOPK_MAX = 256
N_REC_HEADS = 4
REC_DIM = 128
EPS = 1e-6
Q_SCALE = HEAD_DIM ** -0.5 * 1.4426950408889634

LANES = 128
SUBLANES = 8

TOK_TILE = 384
Q_TILE = 128
KEY_GROUP = 4
REC_CHUNK = 64
REC_CHUNKS_PER_STEP = 3
MATMUL_MID_LEVELS = 3
PAGES_PER_STEP = 16
FFN_CHUNK = 256
VMEM_LIMIT = 56 * 1024 * 1024

NEG_SCORE = -3.0e38
NEG_HALF = -1.5e38
NEG_BIAS = -1.0e30
BIG = 3.0e38
BISECTIONS_PER_ROUND = 16
MAX_ROUNDS = 160

C_Q, C_K, C_V, C_QI, C_KIWI, C_RQ, C_FR, C_IR, C_GR, C_GA, C_GB, C_END = (
    0, 512, 768, 1024, 1536, 1664, 2176, 2688, 3200, 3712, 4736, 5760)

_NT = (((1,), (1,)), ((), ()))
_TN = (((0,), (0,)), ((), ()))


def _sigmoid(x):
    return 1.0 / (1.0 + jnp.exp(-x))


def _split3(x):
    hi = x.astype(BF16)
    r1 = x - hi.astype(F32)
    mid = r1.astype(BF16)
    lo = (r1 - mid.astype(F32)).astype(BF16)
    return jnp.concatenate([hi, mid, lo], axis=1)


def _join3(r):
    return r[:, 0:LANES] + r[:, LANES:2 * LANES] + r[:, 2 * LANES:3 * LANES]


def _inproj_body(x_ref, g_ref, w_ref, rope_ref, qn_ref, kn_ref, lb_ref,
                 q_ref, kf_ref, vf_ref, kb_ref, vb_ref, qi_ref, kiwi_ref, kk_ref,
                 rq_ref, f_ref, rv_ref, gr_ref, ga_ref, gb_ref):
    x = x_ref[...]
    h = x * lax.rsqrt(jnp.mean(x * x, axis=-1, keepdims=True) + EPS) * g_ref[...]
    hb = h.astype(BF16)
    rows = x.shape[0]
    lane = lax.broadcasted_iota(jnp.int32, (rows, LANES), 1)
    low = lane < HEAD_DIM
    cosv = rope_ref[:, 0:LANES]
    sin_up = rope_ref[:, LANES:2 * LANES]
    sin_dn = rope_ref[:, 2 * LANES:3 * LANES]

    def proj(c0, c1):
        return jnp.dot(hb, w_ref[:, c0:c1], preferred_element_type=F32)

    def rope(xs):
        return (xs * cosv + pltpu.roll(xs, ROT_DIM // 2, 1) * sin_up
                + pltpu.roll(xs, LANES - ROT_DIM // 2, 1) * sin_dn)

    def headnorm(xs, gain):
        sq = xs * xs
        s_lo = jnp.sum(jnp.where(low, sq, 0.0), axis=-1, keepdims=True)
        s_hi = jnp.sum(jnp.where(low, 0.0, sq), axis=-1, keepdims=True)
        ms = jnp.where(low, s_lo, s_hi) * (1.0 / HEAD_DIM)
        return xs * lax.rsqrt(ms + EPS) * gain

    qraw = proj(C_Q, C_K)
    for p in range(N_KV_HEADS):
        qs = rope(headnorm(qraw[:, p * LANES:(p + 1) * LANES], qn_ref[...])) * Q_SCALE
        qr = pltpu.roll(qs, HEAD_DIM, 1)
        if p % 2 == 0:
            h0, h1 = jnp.where(low, qs, 0.0), jnp.where(low, qr, 0.0)
        else:
            h0, h1 = jnp.where(low, 0.0, qr), jnp.where(low, 0.0, qs)
        q_ref[:, (2 * p) * LANES:(2 * p + 1) * LANES] = h0.astype(BF16)
        q_ref[:, (2 * p + 1) * LANES:(2 * p + 2) * LANES] = h1.astype(BF16)

    kraw = proj(C_K, C_V)
    for s in range(2):
        ks = rope(headnorm(kraw[:, s * LANES:(s + 1) * LANES], kn_ref[...]))
        kf_ref[:, s * LANES:(s + 1) * LANES] = ks
        kb_ref[:, s * LANES:(s + 1) * LANES] = ks.astype(BF16)

    vraw = proj(C_V, C_QI)
    vf_ref[...] = vraw
    vb_ref[...] = vraw.astype(BF16)

    qiraw = proj(C_QI, C_KIWI)
    for p in range(4):
        qi_ref[:, p * LANES:(p + 1) * LANES] = rope(qiraw[:, p * LANES:(p + 1) * LANES]).astype(BF16)

    kiwi = proj(C_KIWI, C_RQ)
    kir = rope(kiwi)
    kiwi_ref[...] = jnp.where(low, kir, kiwi * IDX_W_SCALE)
    kk_ref[...] = jnp.where(low, kir, pltpu.roll(kir, HEAD_DIM, 1)).astype(BF16)

    qr_ = proj(C_RQ, C_FR)
    rq_ref[...] = qr_ * _sigmoid(qr_)
    lb = lb_ref[...]
    f_ref[...] = lb + (1.0 - lb) * _sigmoid(proj(C_FR, C_IR))
    rv_ref[...] = proj(C_IR, C_GR)
    g_ = proj(C_GR, C_GA)
    gr_ref[...] = g_ * _sigmoid(g_)
    ga_ref[...] = _sigmoid(proj(C_GA, C_GB))
    gb_ref[...] = _sigmoid(proj(C_GB, C_END))


def _inproj(x_all, g, w, rope_tab, qn, kn, lb, tiles_per_seq, n_prompt_tiles):
    n_rows, d = x_all.shape
    n_tiles = n_rows // TOK_TILE
    row = lambda i: (i, 0)
    const = lambda i: (0, 0)
    rope_map = lambda i: (jnp.where(i < n_prompt_tiles, i % tiles_per_seq, tiles_per_seq), 0)
    widths = dict(q=(1024, BF16), kf=(256, F32), vf=(256, F32), kb=(256, BF16), vb=(256, BF16),
                  qi=(512, BF16), kiwi=(128, F32), kk=(128, BF16), rq=(512, F32), f=(512, F32),
                  rv=(512, F32), gr=(512, F32), ga=(1024, F32), gb=(1024, F32))
    out_shape = [jax.ShapeDtypeStruct((n_rows, wd), dt) for wd, dt in widths.values()]
    out_specs = [pl.BlockSpec((TOK_TILE, wd), row) for wd, _ in widths.values()]
    outs = pl.pallas_call(
        _inproj_body,
        grid=(n_tiles,),
        in_specs=[
            pl.BlockSpec((TOK_TILE, d), row),
            pl.BlockSpec((1, d), const),
            pl.BlockSpec((d, C_END), const, pipeline_mode=pl.Buffered(1)),
            pl.BlockSpec((TOK_TILE, 3 * LANES), rope_map),
            pl.BlockSpec((1, LANES), const),
            pl.BlockSpec((1, LANES), const),
            pl.BlockSpec((1, 512), const),
        ],
        out_specs=out_specs,
        out_shape=out_shape,
        compiler_params=pltpu.CompilerParams(
            dimension_semantics=("arbitrary",), vmem_limit_bytes=VMEM_LIMIT),
        name="inproj",
    )(x_all, g, w, rope_tab, qn, kn, lb)
    return dict(zip(widths.keys(), outs))


def _select_threshold(count_ge, bracket_minmax, lo0, hi0, n_adm, need, top_k):
    def halve(st):
        lo, hi, c_lo, c_hi, _ = st
        mid = 0.5 * (lo + hi)
        c_mid = count_ge(mid)
        ge = c_mid >= top_k
        lo2, c_lo2 = jnp.where(ge, mid, lo), jnp.where(ge, c_mid, c_lo)
        hi2, c_hi2 = jnp.where(ge, hi, mid), jnp.where(ge, c_hi, c_mid)
        r = top_k - c_hi2
        stop = (c_lo2 - c_hi2 == r) | (r == 1.0) | (mid <= lo) | (mid >= hi) | jnp.logical_not(need)
        return lo2, hi2, c_lo2, c_hi2, jnp.where(stop, 1.0, 0.0)

    def inner_cond(st):
        return (jnp.min(st[4]) < 0.5) & (st[5] < BISECTIONS_PER_ROUND)

    def inner_body(st):
        return halve(halve(st[:5])) + (st[5] + 2,)

    def outer_cond(st):
        return (jnp.min(st[4]) < 0.5) & (st[7] < MAX_ROUNDS)

    def outer_body(st):
        lo, hi, c_lo, c_hi, done = lax.while_loop(inner_cond, inner_body, st[:5] + (jnp.int32(0),))[:5]
        mn, mx = bracket_minmax(lo, hi)
        open_ = need & (done < 0.5)
        done = jnp.where(open_ & (mn == mx), 1.0, done)
        return jnp.where(open_, mn, lo), hi, c_lo, c_hi, done, mn, mx, st[7] + 1

    zero = jnp.zeros_like(lo0)
    lo, hi, c_lo, c_hi, _, mn, mx, _ = lax.while_loop(
        outer_cond, outer_body, (lo0, hi0, n_adm, zero, zero, zero, zero, jnp.int32(0)))
    r = top_k - c_hi
    thr = jnp.where(need, jnp.where(r == 1.0, mx, mn), NEG_SCORE)
    n_tie = jnp.where(need & (c_lo - c_hi != r), r, BIG)
    return thr, n_tie


def _attn_body(q_ref, qi_ref, wq_ref, kb_ref, vb_ref, kk_ref, o_ref,
               sc_ref, qia_ref, qa_ref, vt_ref, m_ref, l_ref, acc_ref,
               da_ref, db_ref, sa_ref, sb_ref, *, top_k, t_real):
    j = pl.program_id(1)
    nchunks = j + 1
    T = Q_TILE
    key_i = lax.broadcasted_iota(jnp.int32, (T, LANES), 0)
    qry_i = lax.broadcasted_iota(jnp.int32, (T, LANES), 1)
    low = qry_i < HEAD_DIM
    qpos = j * T + qry_i

    @pl.when(j == 0)
    def _():
        def transpose_v(c, carry):
            blk = vb_ref[pl.ds(pl.multiple_of(c * T, T), T), :].astype(F32)
            for s in range(2):
                vt_ref[c, s] = blk[:, s * LANES:(s + 1) * LANES].T.astype(BF16)
            return carry
        lax.fori_loop(0, vt_ref.shape[0], transpose_v, 0)

    wt = wq_ref[...].T
    qi = qi_ref[...]
    for h in range(N_IDX_HEADS):
        slab = qi[:, (h // 2) * LANES:(h // 2 + 1) * LANES]
        zero = jnp.zeros_like(slab)
        qia_ref[h * T:(h + 1) * T, :] = jnp.where(low, slab, zero) if h % 2 == 0 else jnp.where(low, zero, slab)
        qa_ref[h * T:(h + 1) * T, :] = q_ref[:, h * LANES:(h + 1) * LANES]

    G = KEY_GROUP
    ngroups = (nchunks + G - 1) // G
    last_chunk = vt_ref.shape[0] - 1

    def chunk_rows(ref, g, u, lanes):
        cc = jnp.minimum(g * G + u, last_chunk)
        return ref[pl.ds(pl.multiple_of(cc * T, T), T), lanes]

    def fold(x, op):
        out = x[0:T]
        for u in range(1, G):
            out = op(out, x[u * T:(u + 1) * T])
        return out

    npairs = ngroups // 2
    odd_tail = ngroups % 2 == 1

    def idx_dots(dst, g):
        kk = jnp.concatenate([chunk_rows(kk_ref, g, u, slice(None)) for u in range(G)], axis=0)
        dst[...] = lax.dot_general(kk, qia_ref[...], _NT, preferred_element_type=F32)

    def idx_scores(src, g, carry):
        mn, mx = carry
        s = jnp.zeros((G * T, LANES), F32)
        for h in range(N_IDX_HEADS):
            s = s + wt[IDX_DIM + h:IDX_DIM + h + 1, :] * jnp.maximum(src[:, h * T:(h + 1) * T], 0.0)
        kpos = g * (G * T) + lax.broadcasted_iota(jnp.int32, (G * T, LANES), 0)
        adm = kpos <= j * T + lax.broadcasted_iota(jnp.int32, (G * T, LANES), 1)
        sc_ref[pl.ds(pl.multiple_of(g * (G * T), G * T), G * T), :] = jnp.where(adm, s, NEG_SCORE)
        return (jnp.minimum(mn, fold(jnp.where(adm, s, BIG), jnp.minimum)),
                jnp.maximum(mx, fold(jnp.where(adm, s, -BIG), jnp.maximum)))

    def phase_a(i, carry):
        idx_dots(db_ref, 2 * i + 1)
        carry = idx_scores(da_ref, 2 * i, carry)
        idx_dots(da_ref, 2 * i + 2)
        return idx_scores(db_ref, 2 * i + 1, carry)

    idx_dots(da_ref, 0)
    rng = lax.fori_loop(0, npairs, phase_a,
                        (jnp.full((T, LANES), BIG, F32), jnp.full((T, LANES), -BIG, F32)))
    mn, mx = lax.cond(odd_tail, lambda c: idx_scores(da_ref, ngroups - 1, c), lambda c: c, rng)

    def sc_group(g):
        return sc_ref[pl.ds(pl.multiple_of(g * (G * T), G * T), G * T), :]

    def count_ge(x):
        def step(g, acc):
            return acc + fold(jnp.where(sc_group(g) >= x, 1.0, 0.0), jnp.add)
        return jnp.sum(lax.fori_loop(0, ngroups, step, jnp.zeros((T, LANES), F32)), axis=0, keepdims=True)

    def bracket_minmax(lo, hi):
        def step(g, carry):
            a, b = carry
            s = sc_group(g)
            inb = (s >= lo) & (s < hi)
            return (jnp.minimum(a, fold(jnp.where(inb, s, BIG), jnp.minimum)),
                    jnp.maximum(b, fold(jnp.where(inb, s, -BIG), jnp.maximum)))
        a, b = lax.fori_loop(0, ngroups, step,
                             (jnp.full((T, LANES), BIG, F32), jnp.full((T, LANES), -BIG, F32)))
        return jnp.min(a, axis=0, keepdims=True), jnp.max(b, axis=0, keepdims=True)

    qrow = j * T + lax.broadcasted_iota(jnp.int32, (1, LANES), 1)
    n_adm = (qrow + 1).astype(F32)
    need = (qrow + 1 > top_k) & (qrow < t_real)
    top = jnp.max(mx, axis=0, keepdims=True)
    thr, n_tie = _select_threshold(count_ge, bracket_minmax, jnp.min(mn, axis=0, keepdims=True),
                                   top + (jnp.abs(top) + 1.0), n_adm, need, float(top_k))

    m_ref[...] = jnp.full(m_ref.shape, NEG_SCORE, F32)
    l_ref[...] = jnp.zeros(l_ref.shape, F32)
    acc_ref[...] = jnp.zeros(acc_ref.shape, F32)
    earlier = jnp.where(qry_i < key_i, 1.0, 0.0).astype(BF16)

    def qk_dots(dst, g):
        for sl in range(2):
            lanes = slice(sl * LANES, (sl + 1) * LANES)
            kc = jnp.concatenate([chunk_rows(kb_ref, g, u, lanes) for u in range(G)], axis=0)
            dst[sl] = lax.dot_general(kc, qa_ref[sl * 4 * T:(sl + 1) * 4 * T, :], _NT,
                                      preferred_element_type=F32)

    def softmax_pv(src, g, seen, issue_next):
        s = sc_group(g)
        eq = s == thr
        eqf = jnp.where(eq, 1.0, 0.0)
        eq_l = jnp.concatenate([eqf[u * T:(u + 1) * T] for u in range(G)], axis=1).astype(BF16)
        in_chunk = jnp.dot(earlier, eq_l, preferred_element_type=F32)
        ranks = []
        for u in range(G):
            ranks.append(in_chunk[:, u * T:(u + 1) * T] + seen)
            seen = seen + jnp.sum(eqf[u * T:(u + 1) * T], axis=0, keepdims=True)
        rank = jnp.concatenate(ranks, axis=0)
        keep_tie = jnp.where(eq, jnp.where(rank < n_tie, 0.0, NEG_BIAS), NEG_BIAS)
        bias = jnp.where(s > NEG_HALF, jnp.where(s > thr, 0.0, keep_tie), NEG_BIAS)
        bias4 = jnp.concatenate([bias] * 4, axis=1)
        sts, m_news, alphas = [], [], []
        for sl in range(2):
            st = src[sl] + bias4
            m_old = m_ref[sl]
            m_new = jnp.maximum(m_old, jnp.max(st, axis=0, keepdims=True))
            m_ref[sl] = m_new
            sts.append(st)
            m_news.append(m_new)
            alphas.append(jnp.exp2(m_old - m_new))
        issue_next()
        for sl in range(2):
            p = jnp.exp2(sts[sl] - m_news[sl])
            l_ref[sl] = alphas[sl] * l_ref[sl] + jnp.sum(p, axis=0, keepdims=True)
            vt = jnp.concatenate([vt_ref[jnp.minimum(g * G + u, last_chunk), sl] for u in range(G)], axis=1)
            acc_ref[sl] = alphas[sl] * acc_ref[sl] + jnp.dot(vt, p.astype(BF16), preferred_element_type=F32)
        return seen

    def phase_c(i, seen):
        seen = softmax_pv(sa_ref, 2 * i, seen, lambda: qk_dots(sb_ref, 2 * i + 1))
        return softmax_pv(sb_ref, 2 * i + 1, seen, lambda: qk_dots(sa_ref, 2 * i + 2))

    qk_dots(sa_ref, 0)
    seen = lax.fori_loop(0, npairs, phase_c, jnp.zeros((1, LANES), F32))

    @pl.when(odd_tail)
    def _():
        softmax_pv(sa_ref, ngroups - 1, seen, lambda: None)

    for p in range(N_KV_HEADS):
        sl, hh = (2 * p) // 4, (2 * p) % 4
        out = acc_ref[sl] / l_ref[sl]
        a0 = out[:, hh * T:(hh + 1) * T].T
        a1 = out[:, (hh + 1) * T:(hh + 2) * T].T
        if p % 2 == 0:
            slab = jnp.where(low, a0, pltpu.roll(a1, HEAD_DIM, 1))
        else:
            slab = jnp.where(low, pltpu.roll(a0, HEAD_DIM, 1), a1)
        o_ref[:, p * LANES:(p + 1) * LANES] = slab.astype(BF16)


def _attn_prompt(pj, batch, t_pad, t_real, top_k):
    nq = t_pad // Q_TILE
    grp_rows = KEY_GROUP * Q_TILE
    n_grp = 2 * -(-(-(-nq // KEY_GROUP)) // 2)
    qrow = lambda b, j: (b * nq + j, 0)
    seq = lambda b, j: (b, 0)
    return pl.pallas_call(
        functools.partial(_attn_body, top_k=top_k, t_real=t_real),
        grid=(batch, nq),
        in_specs=[
            pl.BlockSpec((Q_TILE, 1024), qrow),
            pl.BlockSpec((Q_TILE, 512), qrow),
            pl.BlockSpec((Q_TILE, LANES), qrow),
            pl.BlockSpec((t_pad, 256), seq),
            pl.BlockSpec((t_pad, 256), seq),
            pl.BlockSpec((t_pad, LANES), seq),
        ],
        out_specs=pl.BlockSpec((Q_TILE, 512), qrow),
        out_shape=jax.ShapeDtypeStruct((batch * t_pad, 512), BF16),
        scratch_shapes=[
            pltpu.VMEM((n_grp * grp_rows, LANES), F32),
            pltpu.VMEM((N_IDX_HEADS * Q_TILE, LANES), BF16),
            pltpu.VMEM((N_HEADS * Q_TILE, LANES), BF16),
            pltpu.VMEM((nq, 2, LANES, Q_TILE), BF16),
            pltpu.VMEM((2, 1, 4 * Q_TILE), F32),
            pltpu.VMEM((2, 1, 4 * Q_TILE), F32),
            pltpu.VMEM((2, LANES, 4 * Q_TILE), F32),
            pltpu.VMEM((grp_rows, N_IDX_HEADS * Q_TILE), F32),
            pltpu.VMEM((grp_rows, N_IDX_HEADS * Q_TILE), F32),
            pltpu.VMEM((2, grp_rows, 4 * Q_TILE), F32),
            pltpu.VMEM((2, grp_rows, 4 * Q_TILE), F32),
        ],
        compiler_params=pltpu.CompilerParams(
            dimension_semantics=("arbitrary", "arbitrary"), vmem_limit_bytes=VMEM_LIMIT),
        name="attn_prompt",
    )(pj["q"], pj["qi"], pj["kiwi"], pj["kb"], pj["vb"], pj["kk"])


def _gla_body(rq_ref, f_ref, rv_ref, cum_ref, o_ref, sout_ref, st_ref, *, t_real):
    step = pl.program_id(1)
    n_steps = pl.num_programs(1)
    C = REC_CHUNK
    CH = o_ref.shape[0] // C
    n_lev = C.bit_length() - 1

    @pl.when(step == 0)
    def _():
        st_ref[...] = jnp.zeros(st_ref.shape, F32)

    trow = lax.broadcasted_iota(jnp.int32, (C, LANES), 0)
    ti = lax.broadcasted_iota(jnp.int32, (C, C), 0)
    si = lax.broadcasted_iota(jnp.int32, (C, C), 1)
    H = range(N_REC_HEADS)
    units = [(cc, h) for cc in range(CH) for h in H]
    valid = [(step * CH + cc) * C + trow < t_real for cc in range(CH)]

    def tile(ref, u):
        return ref[u[0] * C:(u[0] + 1) * C, u[1] * LANES:(u[1] + 1) * LANES]

    f = {u: tile(f_ref, u) for u in units}
    q = {u: tile(rq_ref, u) for u in units}
    vb = {u: tile(rv_ref, u).astype(BF16) for u in units}
    kk = {u: jnp.where(valid[u[0]], 1.0 - f[u], 0.0) for u in units}

    lf3 = jnp.concatenate([_split3(jnp.where(valid[u[0]], jnp.log(f[u]), 0.0)) for u in units], axis=1)
    cums = jnp.dot(cum_ref[...], lf3, preferred_element_type=F32)
    cum = {u: _join3(cums[:, i * 3 * LANES:(i + 1) * 3 * LANES]) for i, u in enumerate(units)}
    b = {u: cum[u][0:C] for u in units}

    def midpoint(u, lev):
        if lev < MATMUL_MID_LEVELS:
            return cum[u][(1 + lev) * C:(2 + lev) * C]
        half = 1 << lev
        return jnp.concatenate(
            [jnp.broadcast_to(b[u][s + half - 1:s + half, :], (2 * half, LANES)) for s in range(0, C, 2 * half)],
            axis=0)

    a = {u: jnp.where(ti == si, lax.dot_general(q[u].astype(BF16), kk[u].astype(BF16), _NT,
                                                preferred_element_type=F32), 0.0) for u in units}
    for lev in range(n_lev):
        half = 1 << lev
        up = (trow & half) != 0
        blk_mask = ((ti >> (lev + 1)) == (si >> (lev + 1))) & ((ti & half) != 0) & ((si & half) == 0)
        for u in units:
            bl = midpoint(u, lev)
            e = jnp.exp(jnp.where(up, b[u] - bl, bl - b[u]))
            qt = jnp.where(up, q[u] * e, 0.0).astype(BF16)
            kt = jnp.where(up, 0.0, kk[u] * e).astype(BF16)
            a[u] = a[u] + jnp.where(blk_mask, lax.dot_general(qt, kt, _NT, preferred_element_type=F32), 0.0)

    st = {h: st_ref[h] for h in H}
    for u in units:
        cc, h = u
        qe = (q[u] * jnp.exp(b[u])).astype(BF16)
        o = jnp.dot(a[u].astype(BF16), vb[u], preferred_element_type=F32)
        o_ref[cc * C:(cc + 1) * C, h * LANES:(h + 1) * LANES] = o + lax.dot_general(
            qe, st[h].astype(BF16), _NT, preferred_element_type=F32)
        b_last = b[u][C - 1:C, :]
        kd = (kk[u] * jnp.exp(b_last - b[u])).astype(BF16)
        st[h] = st[h] * jnp.exp(b_last) + lax.dot_general(vb[u], kd, _TN, preferred_element_type=F32)
    for h in H:
        st_ref[h] = st[h]

        @pl.when(step == n_steps - 1)
        def _(h=h):
            sout_ref[0, h] = st[h].T


def _gla_consts():
    C = REC_CHUNK
    t = jnp.arange(C)
    tri = t[None, :] <= t[:, None]
    blocks = [tri]
    for lev in range(MATMUL_MID_LEVELS):
        half = 1 << lev
        mid_row = (t >> (lev + 1)) * (2 * half) + half - 1
        blocks.append(t[None, :] <= mid_row[:, None])
    return jnp.concatenate(blocks, axis=0).astype(BF16)


def _gla_prompt(pj, batch, t_pad, t_real):
    rows = REC_CHUNK * REC_CHUNKS_PER_STEP
    nch = t_pad // rows
    cum = _gla_consts()
    row = lambda b, c: (b * nch + c, 0)
    const = lambda b, c: (0, 0)
    return pl.pallas_call(
        functools.partial(_gla_body, t_real=t_real),
        grid=(batch, nch),
        in_specs=[
            pl.BlockSpec((rows, 512), row),
            pl.BlockSpec((rows, 512), row),
            pl.BlockSpec((rows, 512), row),
            pl.BlockSpec(cum.shape, const),
        ],
        out_specs=[
            pl.BlockSpec((rows, 512), row),
            pl.BlockSpec((1, N_REC_HEADS, REC_DIM, REC_DIM), lambda b, c: (b, 0, 0, 0)),
        ],
        out_shape=[
            jax.ShapeDtypeStruct((batch * t_pad, 512), F32),
            jax.ShapeDtypeStruct((batch, N_REC_HEADS, REC_DIM, REC_DIM), F32),
        ],
        scratch_shapes=[pltpu.VMEM((N_REC_HEADS, REC_DIM, REC_DIM), F32)],
        compiler_params=pltpu.CompilerParams(
            dimension_semantics=("arbitrary", "arbitrary"), vmem_limit_bytes=VMEM_LIMIT),
        name="gla_prompt",
    )(pj["rq"], pj["f"], pj["rv"], cum)


def _gla_sample_body(rq_ref, f_ref, rv_ref, s0_ref, o_ref, sout_ref):
    b = pl.program_id(0)
    q = rq_ref[pl.ds(b, 1), :]
    f = f_ref[pl.ds(b, 1), :]
    v = rv_ref[pl.ds(b, 1), :]
    for h in range(N_REC_HEADS):
        sl = slice(h * LANES, (h + 1) * LANES)
        col = lambda r: jnp.broadcast_to(r[:, sl], (REC_DIM, LANES)).T
        s_new = col(f) * s0_ref[0, h] + col(1.0 - f) * v[:, sl]
        sout_ref[0, h] = s_new
        o_ref[0, :, sl] = jnp.sum(col(q) * s_new, axis=0, keepdims=True)


def _gla_sample(rq, f, rv, state):
    db = state.shape[0]
    full = lambda b: (0, 0)
    st = lambda b: (b, 0, 0, 0)
    o, s = pl.pallas_call(
        _gla_sample_body,
        grid=(db,),
        in_specs=[
            pl.BlockSpec(rq.shape, full), pl.BlockSpec(f.shape, full), pl.BlockSpec(rv.shape, full),
            pl.BlockSpec((1, N_REC_HEADS, REC_DIM, REC_DIM), st),
        ],
        out_specs=[
            pl.BlockSpec((1, 1, 512), lambda b: (b, 0, 0)),
            pl.BlockSpec((1, N_REC_HEADS, REC_DIM, REC_DIM), st),
        ],
        out_shape=[
            jax.ShapeDtypeStruct((db, 1, 512), F32),
            jax.ShapeDtypeStruct(state.shape, F32),
        ],
        compiler_params=pltpu.CompilerParams(dimension_semantics=("arbitrary",)),
        name="gla_sample",
    )(rq, f, rv, state)
    return o.reshape(db, 512), s


def _sample_select_body(pt_ref, *refs, n_pages, top_k):
    P = PAGES_PER_STEP
    ki_refs = refs[0:P]
    qi_ref, w_ref, kin_ref, sel_ref, sc_ref = refs[P:]
    g = pl.program_id(1)
    n_groups = n_pages // P
    n_rows = sc_ref.shape[0]
    qi = qi_ref[0]
    w = w_ref[0]

    rows = []
    for i in range(P):
        d = jnp.dot(qi, ki_refs[i][...].astype(BF16), preferred_element_type=F32)
        rows.append(jnp.sum(w * jnp.maximum(d, 0.0), axis=0, keepdims=True))
    sc_ref[pl.ds(pl.multiple_of(g * P, P), P), :] = jnp.concatenate(rows, axis=0)

    @pl.when(g == n_groups - 1)
    def _():
        d_new = jnp.sum(qi.astype(F32) * kin_ref[0].astype(F32), axis=1, keepdims=True)
        s_new = jnp.sum(w * jnp.maximum(d_new, 0.0), axis=0, keepdims=True)
        tail = lax.broadcasted_iota(jnp.int32, (n_rows - n_pages, LANES), 0) * LANES + \
            lax.broadcasted_iota(jnp.int32, (n_rows - n_pages, LANES), 1)
        sc_ref[n_pages:n_rows, :] = jnp.where(tail == 0, s_new, NEG_SCORE)

        s = sc_ref[...]
        adm = s > NEG_HALF
        top = jnp.max(s, keepdims=True)

        def count_ge(x):
            return jnp.sum(jnp.where(sc_ref[...] >= x, 1.0, 0.0), keepdims=True)

        def bracket_minmax(lo, hi):
            v = sc_ref[...]
            inb = (v >= lo) & (v < hi)
            return (jnp.min(jnp.where(inb, v, BIG), keepdims=True),
                    jnp.max(jnp.where(inb, v, -BIG), keepdims=True))

        n_adm = jnp.full((1, 1), float(n_pages * LANES + 1), F32)
        thr, n_tie = _select_threshold(count_ge, bracket_minmax, jnp.min(jnp.where(adm, s, BIG), keepdims=True),
                                       top + (jnp.abs(top) + 1.0), n_adm, n_adm > top_k, float(top_k))
        eqb = jnp.where(s == thr, 1.0, 0.0).astype(BF16)
        ri = lax.broadcasted_iota(jnp.int32, (LANES, LANES), 0)
        ci = lax.broadcasted_iota(jnp.int32, (LANES, LANES), 1)
        in_row = jnp.dot(eqb, jnp.where(ri < ci, 1.0, 0.0).astype(BF16), preferred_element_type=F32)
        row_tot = jnp.dot(eqb, jnp.ones((LANES, LANES), BF16), preferred_element_type=F32)
        rr = lax.broadcasted_iota(jnp.int32, (n_rows, n_rows), 0)
        rc = lax.broadcasted_iota(jnp.int32, (n_rows, n_rows), 1)
        before = jnp.dot(jnp.where(rc < rr, 1.0, 0.0).astype(BF16), row_tot.astype(BF16),
                         preferred_element_type=F32)
        rank = in_row + before
        keep = (s > thr) | ((s == thr) & (rank < n_tie))
        sel_ref[0] = jnp.where(keep & adm, 1.0, 0.0)


def _sample_attend_body(pt_ref, *refs, n_pages):
    P = PAGES_PER_STEP
    k_refs, v_refs = refs[0:P], refs[P:2 * P]
    sel_ref, q_ref, kn_ref, vn_ref, o_ref, m_ref, l_ref, acc_ref = refs[2 * P:]
    g = pl.program_id(1)
    n_groups = n_pages // P

    @pl.when(g == 0)
    def _():
        q = q_ref[0].astype(F32)
        s_own = jnp.sum(q * kn_ref[0].astype(F32), axis=1, keepdims=True)
        own_sel = sel_ref[0, n_pages:n_pages + 1, 0:1]
        m_ref[...] = jnp.broadcast_to(s_own + jnp.where(own_sel > 0.5, 0.0, NEG_BIAS), m_ref.shape)
        l_ref[...] = jnp.ones(l_ref.shape, F32)
        acc_ref[...] = jnp.broadcast_to(vn_ref[0].astype(F32), acc_ref.shape)

    q = q_ref[0]
    sel = sel_ref[0, pl.ds(pl.multiple_of(g * P, P), P), :]
    scores = []
    for i in range(P):
        s = jnp.dot(q, k_refs[i][...].astype(BF16), preferred_element_type=F32)
        scores.append(jnp.where(sel[i:i + 1, :] > 0.5, s, NEG_BIAS))
    s_all = jnp.concatenate(scores, axis=1)
    m_old = m_ref[...]
    m_new = jnp.maximum(m_old, jnp.max(s_all, axis=1, keepdims=True))
    alpha = jnp.exp2(m_old - m_new)
    p32 = jnp.exp2(s_all - m_new[:, 0:1])
    l_ref[...] = alpha * l_ref[...] + jnp.sum(p32, axis=1, keepdims=True)
    m_ref[...] = m_new
    p = p32.astype(BF16)
    pv = jnp.zeros(acc_ref.shape, F32)
    for i in range(P):
        pv = pv + lax.dot_general(p[:, i * LANES:(i + 1) * LANES], v_refs[i][...].astype(BF16), _NT,
                                  preferred_element_type=F32)
    acc_ref[...] = alpha[:, 0:1] * acc_ref[...] + pv

    @pl.when(g == n_groups - 1)
    def _():
        o_ref[0] = acc_ref[...] / l_ref[:, 0:1]


def _attn_sample(page_table, layer, cache_kit, cache_kt, cache_vt, qi, w, ki_new, q, k_new, v_new):
    db, n_pages = page_table.shape
    P = PAGES_PER_STEP
    n_groups = n_pages // P
    page = cache_kit.shape[3]
    kv_w = N_KV_HEADS * HEAD_DIM
    top_k = min(TOPK_MAX, (n_pages * page + 1) // 4)
    n_rows = -(-(n_pages + 1) // SUBLANES) * SUBLANES
    pt = page_table.reshape(-1)
    params = pltpu.CompilerParams(dimension_semantics=("arbitrary", "arbitrary"), vmem_limit_bytes=VMEM_LIMIT)

    def page_map(i):
        return lambda b, g, pt: (layer, pt[b * n_pages + g * P + i], 0, 0)

    per_b = lambda b, g, pt: (b, 0, 0)
    sel = pl.pallas_call(
        functools.partial(_sample_select_body, n_pages=n_pages, top_k=top_k),
        grid_spec=pltpu.PrefetchScalarGridSpec(
            num_scalar_prefetch=1,
            grid=(db, n_groups),
            in_specs=([pl.BlockSpec((None, None, IDX_DIM, page), page_map(i)) for i in range(P)]
                      + [pl.BlockSpec((1,) + a.shape[1:], per_b) for a in (qi, w, ki_new)]),
            out_specs=pl.BlockSpec((1, n_rows, LANES), per_b),
            scratch_shapes=[pltpu.VMEM((n_rows, LANES), F32)]),
        out_shape=jax.ShapeDtypeStruct((db, n_rows, LANES), F32),
        compiler_params=params,
        name="sample_select",
    )(pt, *([cache_kit] * P), qi, w, ki_new)

    out = pl.pallas_call(
        functools.partial(_sample_attend_body, n_pages=n_pages),
        grid_spec=pltpu.PrefetchScalarGridSpec(
            num_scalar_prefetch=1,
            grid=(db, n_groups),
            in_specs=([pl.BlockSpec((None, None, kv_w, page), page_map(i)) for i in range(P)]
                      + [pl.BlockSpec((None, None, kv_w, page), page_map(i)) for i in range(P)]
                      + [pl.BlockSpec((1,) + a.shape[1:], per_b) for a in (sel, q, k_new, v_new)]),
            out_specs=pl.BlockSpec((1, N_HEADS, kv_w), per_b),
            scratch_shapes=[
                pltpu.VMEM((N_HEADS, LANES), F32),
                pltpu.VMEM((N_HEADS, LANES), F32),
                pltpu.VMEM((N_HEADS, kv_w), F32),
            ]),
        out_shape=jax.ShapeDtypeStruct((db, N_HEADS, kv_w), F32),
        compiler_params=params,
        name="sample_attend",
    )(pt, *([cache_kt] * P), *([cache_vt] * P), sel, q, k_new, v_new)
    return jnp.concatenate([out[:, h, (h // 2) * HEAD_DIM:(h // 2 + 1) * HEAD_DIM] for h in range(N_HEADS)], axis=1)


def _back_body(x_ref, att_ref, atts_ref, ro_ref, ros_ref, gr_ref, ga_ref, gb_ref, grec_ref,
               wpa_ref, wpb_ref, wo_ref, y_ref, *, n_prompt_tiles):
    is_sample = pl.program_id(0) >= n_prompt_tiles
    ro = jnp.where(is_sample, ros_ref[...], ro_ref[...])
    att = jnp.where(is_sample, atts_ref[...], att_ref[...])
    gr = gr_ref[...]
    recs = []
    for h in range(N_REC_HEADS):
        r = ro[:, h * LANES:(h + 1) * LANES]
        r = r * lax.rsqrt(jnp.mean(r * r, axis=-1, keepdims=True) + EPS) * grec_ref[...]
        recs.append((r * gr[:, h * LANES:(h + 1) * LANES]).astype(BF16))
    rec = jnp.concatenate(recs, axis=1)
    a = jnp.dot(att, wpa_ref[...], preferred_element_type=F32)
    b = jnp.dot(rec, wpb_ref[...], preferred_element_type=F32)
    mix = (ga_ref[...] * a + gb_ref[...] * b).astype(BF16)
    y_ref[...] = x_ref[...] + jnp.dot(mix, wo_ref[...], preferred_element_type=F32)


def _back(x_all, att_p, att_s, ro_p, ro_s, gr, ga, gb, grec, wpa, wpb, wo):
    n_rows, d = x_all.shape
    n_prompt_tiles = att_p.shape[0] // TOK_TILE
    row = lambda i: (i, 0)
    prow = lambda i: (jnp.minimum(i, n_prompt_tiles - 1), 0)
    const = lambda i: (0, 0)
    return pl.pallas_call(
        functools.partial(_back_body, n_prompt_tiles=n_prompt_tiles),
        grid=(n_rows // TOK_TILE,),
        in_specs=[
            pl.BlockSpec((TOK_TILE, d), row),
            pl.BlockSpec((TOK_TILE, 512), prow), pl.BlockSpec((TOK_TILE, 512), const),
            pl.BlockSpec((TOK_TILE, 512), prow), pl.BlockSpec((TOK_TILE, 512), const),
            pl.BlockSpec((TOK_TILE, 512), row),
            pl.BlockSpec((TOK_TILE, d), row), pl.BlockSpec((TOK_TILE, d), row),
            pl.BlockSpec((1, LANES), const),
            pl.BlockSpec(wpa.shape, const), pl.BlockSpec(wpb.shape, const), pl.BlockSpec(wo.shape, const),
        ],
        out_specs=pl.BlockSpec((TOK_TILE, d), row),
        out_shape=jax.ShapeDtypeStruct((n_rows, d), F32),
        compiler_params=pltpu.CompilerParams(
            dimension_semantics=("arbitrary",), vmem_limit_bytes=VMEM_LIMIT),
        name="mixer_back",
    )(x_all, att_p, att_s, ro_p, ro_s, gr, ga, gb, grec, wpa, wpb, wo)


def _ffn_body(y_ref, g_ref, wgu_ref, wd_ref, o_ref, *, d_ff):
    y = y_ref[...]
    hb = (y * lax.rsqrt(jnp.mean(y * y, axis=-1, keepdims=True) + EPS) * g_ref[...]).astype(BF16)
    acc = y
    for c in range(d_ff // FFN_CHUNK):
        c0 = c * FFN_CHUNK
        gate = jnp.dot(hb, wgu_ref[:, c0:c0 + FFN_CHUNK], preferred_element_type=F32)
        up = jnp.dot(hb, wgu_ref[:, d_ff + c0:d_ff + c0 + FFN_CHUNK], preferred_element_type=F32)
        act = (gate * _sigmoid(gate) * up).astype(BF16)
        acc = acc + jnp.dot(act, wd_ref[c0:c0 + FFN_CHUNK, :], preferred_element_type=F32)
    o_ref[...] = acc


def _ffn(y, g, wgu, wd):
    n_rows, d = y.shape
    d_ff = wd.shape[0]
    row = lambda i: (i, 0)
    const = lambda i: (0, 0)
    return pl.pallas_call(
        functools.partial(_ffn_body, d_ff=d_ff),
        grid=(n_rows // TOK_TILE,),
        in_specs=[
            pl.BlockSpec((TOK_TILE, d), row), pl.BlockSpec((1, d), const),
            pl.BlockSpec(wgu.shape, const, pipeline_mode=pl.Buffered(1)),
            pl.BlockSpec(wd.shape, const, pipeline_mode=pl.Buffered(1)),
        ],
        out_specs=pl.BlockSpec((TOK_TILE, d), row),
        out_shape=jax.ShapeDtypeStruct((n_rows, d), F32),
        compiler_params=pltpu.CompilerParams(
            dimension_semantics=("arbitrary",), vmem_limit_bytes=VMEM_LIMIT),
        name="ffn",
    )(y, g, wgu, wd)


def _rope_table(pos):
    half = ROT_DIM // 2
    inv = jnp.power(ROPE_THETA, -2.0 * jnp.arange(half, dtype=F32) / ROT_DIM)
    ang = pos.astype(F32)[:, None] * inv[None, :]
    cos, sin = jnp.cos(ang), jnp.sin(ang)
    n = pos.shape[0]
    one = jnp.ones((n, HEAD_DIM - ROT_DIM), F32)
    zero8 = jnp.zeros((n, half), F32)
    zero = jnp.zeros((n, HEAD_DIM - ROT_DIM), F32)
    c64 = jnp.concatenate([cos, cos, one], axis=1)
    up64 = jnp.concatenate([zero8, sin, zero], axis=1)
    dn64 = jnp.concatenate([-sin, zero8, zero], axis=1)
    return jnp.concatenate([c64, c64, up64, up64, dn64, dn64], axis=1)


def _pack_w_in(w):
    d = w.shape[0]
    cuts = (512, 768, 1024, 1536, 1600, 1608, 2120, 2632, 3144, 3656, 4680)
    q, k, v, qi, ki, wi, qr, fr, ir, gr, ga, gb = jnp.split(w, cuts, axis=1)
    pad = jnp.zeros((d, LANES - IDX_DIM - N_IDX_HEADS), w.dtype)
    return jnp.concatenate([q, k, v, qi, ki, wi, pad, qr, fr, ir, gr, ga, gb], axis=1).astype(BF16)


def kernel(x_prompt, x_sample, cache_k, cache_v, cache_idx_k, state_rec, page_table, meta_tokens,
           w_in, norm_mix, q_norm, k_norm, lb_raw, rec_norm, w_pa, w_pb, w_o, norm_ffn, w_gu, w_down):
    batch, seq, d = x_prompt.shape
    db = x_sample.shape[0]
    depth = w_in.shape[0]
    n_pages = page_table.shape[1]
    page = cache_k.shape[2]
    past_len = n_pages * page
    t_real = seq + N_META
    t_pad = -(-t_real // TOK_TILE) * TOK_TILE
    tiles_per_seq = t_pad // TOK_TILE
    n_prompt = batch * t_pad
    top_k = min(TOPK_MAX, seq // 4)

    sm = jax.nn.softmax(lb_raw.astype(F32), axis=0)
    lower_bounds = jnp.cumsum(sm, axis=0) - sm[0:1]

    xp = jnp.concatenate([jnp.broadcast_to(meta_tokens[None].astype(F32), (batch, N_META, d)), x_prompt], axis=1)
    xp = jnp.pad(xp, ((0, 0), (0, t_pad - t_real), (0, 0))).reshape(n_prompt, d)
    xs = jnp.pad(x_sample.reshape(db, d), ((0, TOK_TILE - db), (0, 0)))
    x_all = jnp.concatenate([xp, xs], axis=0)

    rope_tab = jnp.concatenate([
        _rope_table(jnp.arange(t_pad, dtype=jnp.int32)),
        _rope_table(jnp.full((TOK_TILE,), past_len, jnp.int32))], axis=0)

    n_pool = cache_k.shape[1]
    cache_kt = jnp.transpose(cache_k, (0, 1, 3, 4, 2)).reshape(depth, n_pool, N_KV_HEADS * HEAD_DIM, page)
    cache_vt = jnp.transpose(cache_v, (0, 1, 3, 4, 2)).reshape(depth, n_pool, N_KV_HEADS * HEAD_DIM, page)
    cache_kit = jnp.transpose(cache_idx_k, (0, 1, 3, 2))
    srows = slice(n_prompt, n_prompt + db)
    tile2 = lambda g_: jnp.concatenate([g_, g_]).reshape(1, LANES).astype(F32)

    pk, pv, pki, ps, sk, sv, ski, ss = [], [], [], [], [], [], [], []
    for l in range(depth):
        pj = _inproj(x_all, norm_mix[l].reshape(1, d).astype(F32), _pack_w_in(w_in[l]), rope_tab,
                     tile2(q_norm[l]), tile2(k_norm[l]), lower_bounds[l].reshape(1, -1),
                     tiles_per_seq, n_prompt // TOK_TILE)

        att_p = _attn_prompt(pj, batch, t_pad, t_real, top_k)
        ro_p, st_p = _gla_prompt(pj, batch, t_pad, t_real)

        q_s = pj["q"][srows].reshape(db, N_HEADS, LANES)
        zero_s = jnp.zeros_like(q_s)
        q_s = jnp.stack([jnp.concatenate([q_s[:, h] if h // 4 == s else zero_s[:, h] for s in range(2)], axis=1)
                         for h in range(N_HEADS)], axis=1)
        qi_s = pj["qi"][srows].reshape(db, N_IDX_HEADS, IDX_DIM)
        w_s = pj["kiwi"][srows, IDX_DIM:IDX_DIM + N_IDX_HEADS].reshape(db, N_IDX_HEADS, 1)
        ki_s = pj["kk"][srows, 0:IDX_DIM].reshape(db, 1, IDX_DIM)
        kn_s = pj["kb"][srows].reshape(db, 1, N_KV_HEADS * HEAD_DIM)
        vn_s = pj["vb"][srows].reshape(db, 1, N_KV_HEADS * HEAD_DIM)
        att_s = _attn_sample(page_table, l, cache_kit, cache_kt, cache_vt, qi_s, w_s, ki_s, q_s, kn_s, vn_s)
        ro_s, st_s = _gla_sample(pj["rq"][srows], pj["f"][srows], pj["rv"][srows], state_rec[l])

        pad_s = lambda a: jnp.pad(a, ((0, TOK_TILE - db), (0, 0)))
        y = _back(x_all, att_p, pad_s(att_s.astype(BF16)), ro_p, pad_s(ro_s),
                  pj["gr"], pj["ga"], pj["gb"], rec_norm[l].reshape(1, LANES).astype(F32),
                  w_pa[l].astype(BF16), w_pb[l].astype(BF16), w_o[l].astype(BF16))
        x_all = _ffn(y, norm_ffn[l].reshape(1, d).astype(F32), w_gu[l].astype(BF16), w_down[l].astype(BF16))

        seq_view = lambda a, wd: a[:n_prompt].reshape(batch, t_pad, wd)[:, :t_real]
        pk.append(seq_view(pj["kf"], 256).reshape(batch, t_real, N_KV_HEADS, HEAD_DIM))
        pv.append(seq_view(pj["vf"], 256).reshape(batch, t_real, N_KV_HEADS, HEAD_DIM))
        pki.append(seq_view(pj["kiwi"], LANES)[..., :IDX_DIM])
        ps.append(st_p)
        sk.append(pj["kf"][srows].reshape(db, 1, N_KV_HEADS, HEAD_DIM))
        sv.append(pj["vf"][srows].reshape(db, 1, N_KV_HEADS, HEAD_DIM))
        ski.append(pj["kiwi"][srows, :IDX_DIM].reshape(db, 1, IDX_DIM))
        ss.append(st_s)

    y_prompt = x_all[:n_prompt].reshape(batch, t_pad, d)[:, N_META:t_real]
    y_sample = x_all[srows].reshape(db, 1, d)
    return (y_prompt, y_sample, jnp.stack(pk), jnp.stack(pv), jnp.stack(pki), jnp.stack(ps),
            jnp.stack(sk), jnp.stack(sv), jnp.stack(ski), jnp.stack(ss))
```

```python
import functools

import jax
import jax.numpy as jnp
from jax import lax
from jax.experimental import pallas as pl
from jax.experimental.pallas import tpu as pltpu

F32 = jnp.float32
BF16 = jnp.bfloat16

N_META = 16
N_HEADS = 8
HEAD_DIM = 64
N_KV_HEADS = 4
ROT_DIM = 16
ROPE_THETA = 500000.0
N_IDX_HEADS = 8
IDX_DIM = 64
IDX_W_SCALE = (N_IDX_HEADS * IDX_DIM) ** -0.5
TOPK_MAX = 256
N_REC_HEADS = 4
REC_DIM = 128
EPS = 1e-6
Q_SCALE = HEAD_DIM ** -0.5 * 1.4426950408889634

LANES = 128
SUBLANES = 8

TOK_TILE = 384
Q_TILE = 384
KEY_CHUNK = 128
KEY_GROUP = 2
REC_CHUNK = 64
REC_CHUNKS_PER_STEP = 3
MATMUL_MID_LEVELS = 3
PAGES_PER_STEP = 16
FFN_CHUNK = 256
VMEM_LIMIT = 56 * 1024 * 1024

NEG_SCORE = -3.0e38
NEG_HALF = -1.5e38
NEG_BIAS = -1.0e30
BIG = 3.0e38
BISECTIONS_PER_ROUND = 16
MAX_ROUNDS = 160

C_Q, C_K, C_V, C_QI, C_KIWI, C_RQ, C_FR, C_IR, C_GR, C_GA, C_GB, C_END = (
    0, 512, 768, 1024, 1536, 1664, 2176, 2688, 3200, 3712, 4736, 5760)

_NT = (((1,), (1,)), ((), ()))
_TN = (((0,), (0,)), ((), ()))


def _sigmoid(x):
    return 1.0 / (1.0 + jnp.exp(-x))


def _split3(x):
    hi = x.astype(BF16)
    r1 = x - hi.astype(F32)
    mid = r1.astype(BF16)
    lo = (r1 - mid.astype(F32)).astype(BF16)
    return jnp.concatenate([hi, mid, lo], axis=1)


def _join3(r):
    return r[:, 0:LANES] + r[:, LANES:2 * LANES] + r[:, 2 * LANES:3 * LANES]


def _inproj_body(x_ref, g_ref, w_ref, rope_ref, qn_ref, kn_ref, lb_ref,
                 q_ref, kf_ref, vf_ref, kb_ref, vb_ref, qi_ref, kiwi_ref, kk_ref,
                 rq_ref, f_ref, rv_ref, gr_ref, ga_ref, gb_ref):
    x = x_ref[...]
    h = x * lax.rsqrt(jnp.mean(x * x, axis=-1, keepdims=True) + EPS) * g_ref[...]
    hb = h.astype(BF16)
    rows = x.shape[0]
    lane = lax.broadcasted_iota(jnp.int32, (rows, LANES), 1)
    low = lane < HEAD_DIM
    cosv = rope_ref[:, 0:LANES]
    sin_up = rope_ref[:, LANES:2 * LANES]
    sin_dn = rope_ref[:, 2 * LANES:3 * LANES]

    def proj(c0, c1):
        return jnp.dot(hb, w_ref[:, c0:c1], preferred_element_type=F32)

    def rope(xs):
        return (xs * cosv + pltpu.roll(xs, ROT_DIM // 2, 1) * sin_up
                + pltpu.roll(xs, LANES - ROT_DIM // 2, 1) * sin_dn)

    def headnorm(xs, gain):
        sq = xs * xs
        s_lo = jnp.sum(jnp.where(low, sq, 0.0), axis=-1, keepdims=True)
        s_hi = jnp.sum(jnp.where(low, 0.0, sq), axis=-1, keepdims=True)
        ms = jnp.where(low, s_lo, s_hi) * (1.0 / HEAD_DIM)
        return xs * lax.rsqrt(ms + EPS) * gain

    qraw = proj(C_Q, C_K)
    for p in range(N_KV_HEADS):
        qs = rope(headnorm(qraw[:, p * LANES:(p + 1) * LANES], qn_ref[...])) * Q_SCALE
        qr = pltpu.roll(qs, HEAD_DIM, 1)
        if p % 2 == 0:
            h0, h1 = jnp.where(low, qs, 0.0), jnp.where(low, qr, 0.0)
        else:
            h0, h1 = jnp.where(low, 0.0, qr), jnp.where(low, 0.0, qs)
        q_ref[:, (2 * p) * LANES:(2 * p + 1) * LANES] = h0.astype(BF16)
        q_ref[:, (2 * p + 1) * LANES:(2 * p + 2) * LANES] = h1.astype(BF16)

    kraw = proj(C_K, C_V)
    for s in range(2):
        ks = rope(headnorm(kraw[:, s * LANES:(s + 1) * LANES], kn_ref[...]))
        kf_ref[:, s * LANES:(s + 1) * LANES] = ks
        kb_ref[:, s * LANES:(s + 1) * LANES] = ks.astype(BF16)

    vraw = proj(C_V, C_QI)
    vf_ref[...] = vraw
    vb_ref[...] = vraw.astype(BF16)

    qiraw = proj(C_QI, C_KIWI)
    for p in range(4):
        qi_ref[:, p * LANES:(p + 1) * LANES] = rope(qiraw[:, p * LANES:(p + 1) * LANES]).astype(BF16)

    kiwi = proj(C_KIWI, C_RQ)
    kir = rope(kiwi)
    kiwi_ref[...] = jnp.where(low, kir, kiwi * IDX_W_SCALE)
    kk_ref[...] = jnp.where(low, kir, pltpu.roll(kir, HEAD_DIM, 1)).astype(BF16)

    qr_ = proj(C_RQ, C_FR)
    rq_ref[...] = qr_ * _sigmoid(qr_)
    lb = lb_ref[...]
    f_ref[...] = lb + (1.0 - lb) * _sigmoid(proj(C_FR, C_IR))
    rv_ref[...] = proj(C_IR, C_GR)
    g_ = proj(C_GR, C_GA)
    gr_ref[...] = g_ * _sigmoid(g_)
    ga_ref[...] = _sigmoid(proj(C_GA, C_GB))
    gb_ref[...] = _sigmoid(proj(C_GB, C_END))


def _inproj(x_all, g, w, rope_tab, qn, kn, lb, tiles_per_seq, n_prompt_tiles):
    n_rows, d = x_all.shape
    n_tiles = n_rows // TOK_TILE
    row = lambda i: (i, 0)
    const = lambda i: (0, 0)
    rope_map = lambda i: (jnp.where(i < n_prompt_tiles, i % tiles_per_seq, tiles_per_seq), 0)
    widths = dict(q=(1024, BF16), kf=(256, F32), vf=(256, F32), kb=(256, BF16), vb=(256, BF16),
                  qi=(512, BF16), kiwi=(128, F32), kk=(128, BF16), rq=(512, F32), f=(512, F32),
                  rv=(512, F32), gr=(512, F32), ga=(1024, F32), gb=(1024, F32))
    out_shape = [jax.ShapeDtypeStruct((n_rows, wd), dt) for wd, dt in widths.values()]
    out_specs = [pl.BlockSpec((TOK_TILE, wd), row) for wd, _ in widths.values()]
    outs = pl.pallas_call(
        _inproj_body,
        grid=(n_tiles,),
        in_specs=[
            pl.BlockSpec((TOK_TILE, d), row),
            pl.BlockSpec((1, d), const),
            pl.BlockSpec((d, C_END), const, pipeline_mode=pl.Buffered(1)),
            pl.BlockSpec((TOK_TILE, 3 * LANES), rope_map),
            pl.BlockSpec((1, LANES), const),
            pl.BlockSpec((1, LANES), const),
            pl.BlockSpec((1, 512), const),
        ],
        out_specs=out_specs,
        out_shape=out_shape,
        compiler_params=pltpu.CompilerParams(
            dimension_semantics=("arbitrary",), vmem_limit_bytes=VMEM_LIMIT),
        name="inproj",
    )(x_all, g, w, rope_tab, qn, kn, lb)
    return dict(zip(widths.keys(), outs))


def _select_threshold(count_ge, bracket_minmax, lo0, hi0, n_adm, need, top_k):
    def halve(st):
        lo, hi, c_lo, c_hi, _ = st
        mid = 0.5 * (lo + hi)
        c_mid = count_ge(mid)
        ge = c_mid >= top_k
        lo2, c_lo2 = jnp.where(ge, mid, lo), jnp.where(ge, c_mid, c_lo)
        hi2, c_hi2 = jnp.where(ge, hi, mid), jnp.where(ge, c_hi, c_mid)
        r = top_k - c_hi2
        stop = (c_lo2 - c_hi2 == r) | (r == 1.0) | (mid <= lo) | (mid >= hi) | jnp.logical_not(need)
        return lo2, hi2, c_lo2, c_hi2, jnp.where(stop, 1.0, 0.0)

    def inner_cond(st):
        return (jnp.min(st[4]) < 0.5) & (st[5] < BISECTIONS_PER_ROUND)

    def inner_body(st):
        return halve(halve(st[:5])) + (st[5] + 2,)

    def outer_cond(st):
        return (jnp.min(st[4]) < 0.5) & (st[7] < MAX_ROUNDS)

    def outer_body(st):
        lo, hi, c_lo, c_hi, done = lax.while_loop(inner_cond, inner_body, st[:5] + (jnp.int32(0),))[:5]
        mn, mx = bracket_minmax(lo, hi)
        open_ = need & (done < 0.5)
        done = jnp.where(open_ & (mn == mx), 1.0, done)
        return jnp.where(open_, mn, lo), hi, c_lo, c_hi, done, mn, mx, st[7] + 1

    zero = jnp.zeros_like(lo0)
    lo, hi, c_lo, c_hi, _, mn, mx, _ = lax.while_loop(
        outer_cond, outer_body, (lo0, hi0, n_adm, zero, zero, zero, zero, jnp.int32(0)))
    r = top_k - c_hi
    thr = jnp.where(need, jnp.where(r == 1.0, mx, mn), NEG_SCORE)
    n_tie = jnp.where(need & (c_lo - c_hi != r), r, BIG)
    return thr, n_tie


def _attn_body(q_ref, qi_ref, wq_ref, kb_ref, vb_ref, kk_ref, o_ref,
               sc_ref, qia_ref, qa_ref, vt_ref, m_ref, l_ref, acc_ref,
               da_ref, db_ref, sa_ref, sb_ref, *, top_k, t_real):
    j = pl.program_id(1)
    T = KEY_CHUNK
    Q = Q_TILE
    nchunks = (j + 1) * (Q // T)
    low = lax.broadcasted_iota(jnp.int32, (Q, LANES), 1) < HEAD_DIM

    @pl.when(j == 0)
    def _():
        def transpose_v(c, carry):
            blk = vb_ref[pl.ds(pl.multiple_of(c * T, T), T), :].astype(F32)
            for s in range(2):
                vt_ref[c, s] = blk[:, s * LANES:(s + 1) * LANES].T.astype(BF16)
            return carry
        lax.fori_loop(0, vt_ref.shape[0], transpose_v, 0)

    wt = wq_ref[...].T
    qi = qi_ref[...]
    for h in range(N_IDX_HEADS):
        slab = qi[:, (h // 2) * LANES:(h // 2 + 1) * LANES]
        zero = jnp.zeros_like(slab)
        qia_ref[h * Q:(h + 1) * Q, :] = jnp.where(low, slab, zero) if h % 2 == 0 else jnp.where(low, zero, slab)
        qa_ref[h * Q:(h + 1) * Q, :] = q_ref[:, h * LANES:(h + 1) * LANES]

    G = KEY_GROUP
    ngroups = (nchunks + G - 1) // G
    last_chunk = vt_ref.shape[0] - 1

    def chunk_rows(ref, g, u, lanes):
        cc = jnp.minimum(g * G + u, last_chunk)
        return ref[pl.ds(pl.multiple_of(cc * T, T), T), lanes]

    def fold(x, op):
        return op(x.reshape(x.shape[0] // SUBLANES, SUBLANES, Q), axis=0)

    npairs = ngroups // 2
    odd_tail = ngroups % 2 == 1

    def idx_dots(dst, g):
        kk = jnp.concatenate([chunk_rows(kk_ref, g, u, slice(None)) for u in range(G)], axis=0)
        dst[...] = lax.dot_general(kk, qia_ref[...], _NT, preferred_element_type=F32)

    def idx_scores(src, g, carry):
        mn, mx = carry
        s = jnp.zeros((G * T, Q), F32)
        for h in range(N_IDX_HEADS):
            s = s + wt[IDX_DIM + h:IDX_DIM + h + 1, :] * jnp.maximum(src[:, h * Q:(h + 1) * Q], 0.0)
        kpos = g * (G * T) + lax.broadcasted_iota(jnp.int32, (G * T, Q), 0)
        adm = kpos <= j * Q + lax.broadcasted_iota(jnp.int32, (G * T, Q), 1)
        sc_ref[pl.ds(pl.multiple_of(g * (G * T), G * T), G * T), :] = jnp.where(adm, s, NEG_SCORE)
        return (jnp.minimum(mn, fold(jnp.where(adm, s, BIG), jnp.min)),
                jnp.maximum(mx, fold(jnp.where(adm, s, -BIG), jnp.max)))

    def phase_a(i, carry):
        idx_dots(db_ref, 2 * i + 1)
        carry = idx_scores(da_ref, 2 * i, carry)
        idx_dots(da_ref, 2 * i + 2)
        return idx_scores(db_ref, 2 * i + 1, carry)

    idx_dots(da_ref, 0)
    rng = lax.fori_loop(0, npairs, phase_a,
                        (jnp.full((SUBLANES, Q), BIG, F32), jnp.full((SUBLANES, Q), -BIG, F32)))
    mn, mx = lax.cond(odd_tail, lambda c: idx_scores(da_ref, ngroups - 1, c), lambda c: c, rng)

    def sc_group(g):
        return sc_ref[pl.ds(pl.multiple_of(g * (G * T), G * T), G * T), :]

    def count_ge(x):
        def step(g, acc):
            return acc + fold(jnp.where(sc_group(g) >= x, 1.0, 0.0), jnp.sum)
        return jnp.sum(lax.fori_loop(0, ngroups, step, jnp.zeros((SUBLANES, Q), F32)), axis=0, keepdims=True)

    def bracket_minmax(lo, hi):
        def step(g, carry):
            a, b = carry
            s = sc_group(g)
            inb = (s >= lo) & (s < hi)
            return (jnp.minimum(a, fold(jnp.where(inb, s, BIG), jnp.min)),
                    jnp.maximum(b, fold(jnp.where(inb, s, -BIG), jnp.max)))
        a, b = lax.fori_loop(0, ngroups, step,
                             (jnp.full((SUBLANES, Q), BIG, F32), jnp.full((SUBLANES, Q), -BIG, F32)))
        return jnp.min(a, axis=0, keepdims=True), jnp.max(b, axis=0, keepdims=True)

    qrow = j * Q + lax.broadcasted_iota(jnp.int32, (1, Q), 1)
    n_adm = (qrow + 1).astype(F32)
    need = (qrow + 1 > top_k) & (qrow < t_real)
    top = jnp.max(mx, axis=0, keepdims=True)
    thr, n_tie = _select_threshold(count_ge, bracket_minmax, jnp.min(mn, axis=0, keepdims=True),
                                   top + (jnp.abs(top) + 1.0), n_adm, need, float(top_k))

    m_ref[...] = jnp.full(m_ref.shape, NEG_SCORE, F32)
    l_ref[...] = jnp.zeros(l_ref.shape, F32)
    acc_ref[...] = jnp.zeros(acc_ref.shape, F32)
    earlier = jnp.where(lax.broadcasted_iota(jnp.int32, (T, T), 1) < lax.broadcasted_iota(jnp.int32, (T, T), 0),
                        1.0, 0.0).astype(BF16)

    def qk_dots(dst, g):
        for sl in range(2):
            lanes = slice(sl * LANES, (sl + 1) * LANES)
            kc = jnp.concatenate([chunk_rows(kb_ref, g, u, lanes) for u in range(G)], axis=0)
            dst[sl] = lax.dot_general(kc, qa_ref[sl * 4 * Q:(sl + 1) * 4 * Q, :], _NT,
                                      preferred_element_type=F32)

    def softmax_pv(src, g, seen, issue_next):
        s = sc_group(g)
        eq = s == thr
        eqf = jnp.where(eq, 1.0, 0.0)
        eq_l = jnp.concatenate([eqf[u * T:(u + 1) * T] for u in range(G)], axis=1).astype(BF16)
        in_chunk = jnp.dot(earlier, eq_l, preferred_element_type=F32)
        ranks = []
        for u in range(G):
            ranks.append(in_chunk[:, u * Q:(u + 1) * Q] + seen)
            seen = seen + jnp.sum(eqf[u * T:(u + 1) * T], axis=0, keepdims=True)
        rank = jnp.concatenate(ranks, axis=0)
        keep_tie = jnp.where(eq, jnp.where(rank < n_tie, 0.0, NEG_BIAS), NEG_BIAS)
        bias = jnp.where(s > NEG_HALF, jnp.where(s > thr, 0.0, keep_tie), NEG_BIAS)
        bias4 = jnp.concatenate([bias] * 4, axis=1)
        sts, m_news, alphas = [], [], []
        for sl in range(2):
            st = src[sl] + bias4
            m_old = m_ref[sl]
            m_new = jnp.maximum(m_old, jnp.max(st, axis=0, keepdims=True))
            m_ref[sl] = m_new
            sts.append(st)
            m_news.append(m_new)
            alphas.append(jnp.exp2(m_old - m_new))
        issue_next()
        for sl in range(2):
            p = jnp.exp2(sts[sl] - m_news[sl])
            l_ref[sl] = alphas[sl] * l_ref[sl] + jnp.sum(p, axis=0, keepdims=True)
            vt = jnp.concatenate([vt_ref[jnp.minimum(g * G + u, last_chunk), sl] for u in range(G)], axis=1)
            acc_ref[sl] = alphas[sl] * acc_ref[sl] + jnp.dot(vt, p.astype(BF16), preferred_element_type=F32)
        return seen

    def phase_c(i, seen):
        seen = softmax_pv(sa_ref, 2 * i, seen, lambda: qk_dots(sb_ref, 2 * i + 1))
        return softmax_pv(sb_ref, 2 * i + 1, seen, lambda: qk_dots(sa_ref, 2 * i + 2))

    qk_dots(sa_ref, 0)
    seen = lax.fori_loop(0, npairs, phase_c, jnp.zeros((1, Q), F32))

    @pl.when(odd_tail)
    def _():
        softmax_pv(sa_ref, ngroups - 1, seen, lambda: None)

    for p in range(N_KV_HEADS):
        sl, hh = (2 * p) // 4, (2 * p) % 4
        out = acc_ref[sl] / l_ref[sl]
        a0 = out[:, hh * Q:(hh + 1) * Q].T
        a1 = out[:, (hh + 1) * Q:(hh + 2) * Q].T
        if p % 2 == 0:
            slab = jnp.where(low, a0, pltpu.roll(a1, HEAD_DIM, 1))
        else:
            slab = jnp.where(low, pltpu.roll(a0, HEAD_DIM, 1), a1)
        o_ref[:, p * LANES:(p + 1) * LANES] = slab.astype(BF16)


def _attn_prompt(pj, batch, t_pad, t_real, top_k):
    nq = t_pad // Q_TILE
    nkc = t_pad // KEY_CHUNK
    grp_rows = KEY_GROUP * KEY_CHUNK
    n_grp = 2 * -(-(-(-nkc // KEY_GROUP)) // 2)
    qrow = lambda b, j: (b * nq + j, 0)
    seq = lambda b, j: (b, 0)
    return pl.pallas_call(
        functools.partial(_attn_body, top_k=top_k, t_real=t_real),
        grid=(batch, nq),
        in_specs=[
            pl.BlockSpec((Q_TILE, 1024), qrow),
            pl.BlockSpec((Q_TILE, 512), qrow),
            pl.BlockSpec((Q_TILE, LANES), qrow),
            pl.BlockSpec((t_pad, 256), seq),
            pl.BlockSpec((t_pad, 256), seq),
            pl.BlockSpec((t_pad, LANES), seq),
        ],
        out_specs=pl.BlockSpec((Q_TILE, 512), qrow),
        out_shape=jax.ShapeDtypeStruct((batch * t_pad, 512), BF16),
        scratch_shapes=[
            pltpu.VMEM((n_grp * grp_rows, Q_TILE), F32),
            pltpu.VMEM((N_IDX_HEADS * Q_TILE, LANES), BF16),
            pltpu.VMEM((N_HEADS * Q_TILE, LANES), BF16),
            pltpu.VMEM((nkc, 2, LANES, KEY_CHUNK), BF16),
            pltpu.VMEM((2, 1, 4 * Q_TILE), F32),
            pltpu.VMEM((2, 1, 4 * Q_TILE), F32),
            pltpu.VMEM((2, LANES, 4 * Q_TILE), F32),
            pltpu.VMEM((grp_rows, N_IDX_HEADS * Q_TILE), F32),
            pltpu.VMEM((grp_rows, N_IDX_HEADS * Q_TILE), F32),
            pltpu.VMEM((2, grp_rows, 4 * Q_TILE), F32),
            pltpu.VMEM((2, grp_rows, 4 * Q_TILE), F32),
        ],
        compiler_params=pltpu.CompilerParams(
            dimension_semantics=("arbitrary", "arbitrary"), vmem_limit_bytes=VMEM_LIMIT),
        name="attn_prompt",
    )(pj["q"], pj["qi"], pj["kiwi"], pj["kb"], pj["vb"], pj["kk"])


def _gla_body(rq_ref, f_ref, rv_ref, cum_ref, o_ref, sout_ref, st_ref, *, t_real):
    step = pl.program_id(1)
    n_steps = pl.num_programs(1)
    C = REC_CHUNK
    CH = o_ref.shape[0] // C
    n_lev = C.bit_length() - 1

    @pl.when(step == 0)
    def _():
        st_ref[...] = jnp.zeros(st_ref.shape, F32)

    trow = lax.broadcasted_iota(jnp.int32, (C, LANES), 0)
    ti = lax.broadcasted_iota(jnp.int32, (C, C), 0)
    si = lax.broadcasted_iota(jnp.int32, (C, C), 1)
    H = range(N_REC_HEADS)
    units = [(cc, h) for cc in range(CH) for h in H]
    valid = [(step * CH + cc) * C + trow < t_real for cc in range(CH)]

    def tile(ref, u):
        return ref[u[0] * C:(u[0] + 1) * C, u[1] * LANES:(u[1] + 1) * LANES]

    f = {u: tile(f_ref, u) for u in units}
    q = {u: tile(rq_ref, u) for u in units}
    vb = {u: tile(rv_ref, u).astype(BF16) for u in units}
    kk = {u: jnp.where(valid[u[0]], 1.0 - f[u], 0.0) for u in units}

    lf3 = jnp.concatenate([_split3(jnp.where(valid[u[0]], jnp.log(f[u]), 0.0)) for u in units], axis=1)
    cums = jnp.dot(cum_ref[...], lf3, preferred_element_type=F32)
    cum = {u: _join3(cums[:, i * 3 * LANES:(i + 1) * 3 * LANES]) for i, u in enumerate(units)}
    b = {u: cum[u][0:C] for u in units}

    def midpoint(u, lev):
        if lev < MATMUL_MID_LEVELS:
            return cum[u][(1 + lev) * C:(2 + lev) * C]
        half = 1 << lev
        return jnp.concatenate(
            [jnp.broadcast_to(b[u][s + half - 1:s + half, :], (2 * half, LANES)) for s in range(0, C, 2 * half)],
            axis=0)

    a = {u: jnp.where(ti == si, lax.dot_general(q[u].astype(BF16), kk[u].astype(BF16), _NT,
                                                preferred_element_type=F32), 0.0) for u in units}
    for lev in range(n_lev):
        half = 1 << lev
        up = (trow & half) != 0
        blk_mask = ((ti >> (lev + 1)) == (si >> (lev + 1))) & ((ti & half) != 0) & ((si & half) == 0)
        for u in units:
            bl = midpoint(u, lev)
            e = jnp.exp(jnp.where(up, b[u] - bl, bl - b[u]))
            qt = jnp.where(up, q[u] * e, 0.0).astype(BF16)
            kt = jnp.where(up, 0.0, kk[u] * e).astype(BF16)
            a[u] = a[u] + jnp.where(blk_mask, lax.dot_general(qt, kt, _NT, preferred_element_type=F32), 0.0)

    st = {h: st_ref[h] for h in H}
    for u in units:
        cc, h = u
        qe = (q[u] * jnp.exp(b[u])).astype(BF16)
        o = jnp.dot(a[u].astype(BF16), vb[u], preferred_element_type=F32)
        o_ref[cc * C:(cc + 1) * C, h * LANES:(h + 1) * LANES] = o + lax.dot_general(
            qe, st[h].astype(BF16), _NT, preferred_element_type=F32)
        b_last = b[u][C - 1:C, :]
        kd = (kk[u] * jnp.exp(b_last - b[u])).astype(BF16)
        st[h] = st[h] * jnp.exp(b_last) + lax.dot_general(vb[u], kd, _TN, preferred_element_type=F32)
    for h in H:
        st_ref[h] = st[h]

        @pl.when(step == n_steps - 1)
        def _(h=h):
            sout_ref[0, h] = st[h].T


def _gla_consts():
    C = REC_CHUNK
    t = jnp.arange(C)
    tri = t[None, :] <= t[:, None]
    blocks = [tri]
    for lev in range(MATMUL_MID_LEVELS):
        half = 1 << lev
        mid_row = (t >> (lev + 1)) * (2 * half) + half - 1
        blocks.append(t[None, :] <= mid_row[:, None])
    return jnp.concatenate(blocks, axis=0).astype(BF16)


def _gla_prompt(pj, batch, t_pad, t_real):
    rows = REC_CHUNK * REC_CHUNKS_PER_STEP
    nch = t_pad // rows
    cum = _gla_consts()
    row = lambda b, c: (b * nch + c, 0)
    const = lambda b, c: (0, 0)
    return pl.pallas_call(
        functools.partial(_gla_body, t_real=t_real),
        grid=(batch, nch),
        in_specs=[
            pl.BlockSpec((rows, 512), row),
            pl.BlockSpec((rows, 512), row),
            pl.BlockSpec((rows, 512), row),
            pl.BlockSpec(cum.shape, const),
        ],
        out_specs=[
            pl.BlockSpec((rows, 512), row),
            pl.BlockSpec((1, N_REC_HEADS, REC_DIM, REC_DIM), lambda b, c: (b, 0, 0, 0)),
        ],
        out_shape=[
            jax.ShapeDtypeStruct((batch * t_pad, 512), F32),
            jax.ShapeDtypeStruct((batch, N_REC_HEADS, REC_DIM, REC_DIM), F32),
        ],
        scratch_shapes=[pltpu.VMEM((N_REC_HEADS, REC_DIM, REC_DIM), F32)],
        compiler_params=pltpu.CompilerParams(
            dimension_semantics=("arbitrary", "arbitrary"), vmem_limit_bytes=VMEM_LIMIT),
        name="gla_prompt",
    )(pj["rq"], pj["f"], pj["rv"], cum)


def _gla_sample_body(rq_ref, f_ref, rv_ref, s0_ref, o_ref, sout_ref):
    b = pl.program_id(0)
    q = rq_ref[pl.ds(b, 1), :]
    f = f_ref[pl.ds(b, 1), :]
    v = rv_ref[pl.ds(b, 1), :]
    for h in range(N_REC_HEADS):
        sl = slice(h * LANES, (h + 1) * LANES)
        col = lambda r: jnp.broadcast_to(r[:, sl], (REC_DIM, LANES)).T
        s_new = col(f) * s0_ref[0, h] + col(1.0 - f) * v[:, sl]
        sout_ref[0, h] = s_new
        o_ref[0, :, sl] = jnp.sum(col(q) * s_new, axis=0, keepdims=True)


def _gla_sample(rq, f, rv, state):
    db = state.shape[0]
    full = lambda b: (0, 0)
    st = lambda b: (b, 0, 0, 0)
    o, s = pl.pallas_call(
        _gla_sample_body,
        grid=(db,),
        in_specs=[
            pl.BlockSpec(rq.shape, full), pl.BlockSpec(f.shape, full), pl.BlockSpec(rv.shape, full),
            pl.BlockSpec((1, N_REC_HEADS, REC_DIM, REC_DIM), st),
        ],
        out_specs=[
            pl.BlockSpec((1, 1, 512), lambda b: (b, 0, 0)),
            pl.BlockSpec((1, N_REC_HEADS, REC_DIM, REC_DIM), st),
        ],
        out_shape=[
            jax.ShapeDtypeStruct((db, 1, 512), F32),
            jax.ShapeDtypeStruct(state.shape, F32),
        ],
        compiler_params=pltpu.CompilerParams(dimension_semantics=("arbitrary",)),
        name="gla_sample",
    )(rq, f, rv, state)
    return o.reshape(db, 512), s


def _sample_select_body(pt_ref, *refs, n_pages, top_k):
    P = PAGES_PER_STEP
    ki_refs = refs[0:P]
    qi_ref, w_ref, kin_ref, sel_ref, sc_ref = refs[P:]
    g = pl.program_id(1)
    n_groups = n_pages // P
    n_rows = sc_ref.shape[0]
    qi = qi_ref[0]
    w = w_ref[0]

    rows = []
    for i in range(P):
        d = jnp.dot(qi, ki_refs[i][...].astype(BF16), preferred_element_type=F32)
        rows.append(jnp.sum(w * jnp.maximum(d, 0.0), axis=0, keepdims=True))
    sc_ref[pl.ds(pl.multiple_of(g * P, P), P), :] = jnp.concatenate(rows, axis=0)

    @pl.when(g == n_groups - 1)
    def _():
        d_new = jnp.sum(qi.astype(F32) * kin_ref[0].astype(F32), axis=1, keepdims=True)
        s_new = jnp.sum(w * jnp.maximum(d_new, 0.0), axis=0, keepdims=True)
        tail = lax.broadcasted_iota(jnp.int32, (n_rows - n_pages, LANES), 0) * LANES + \
            lax.broadcasted_iota(jnp.int32, (n_rows - n_pages, LANES), 1)
        sc_ref[n_pages:n_rows, :] = jnp.where(tail == 0, s_new, NEG_SCORE)

        s = sc_ref[...]
        adm = s > NEG_HALF
        top = jnp.max(s, keepdims=True)

        def count_ge(x):
            return jnp.sum(jnp.where(sc_ref[...] >= x, 1.0, 0.0), keepdims=True)

        def bracket_minmax(lo, hi):
            v = sc_ref[...]
            inb = (v >= lo) & (v < hi)
            return (jnp.min(jnp.where(inb, v, BIG), keepdims=True),
                    jnp.max(jnp.where(inb, v, -BIG), keepdims=True))

        n_adm = jnp.full((1, 1), float(n_pages * LANES + 1), F32)
        thr, n_tie = _select_threshold(count_ge, bracket_minmax, jnp.min(jnp.where(adm, s, BIG), keepdims=True),
                                       top + (jnp.abs(top) + 1.0), n_adm, n_adm > top_k, float(top_k))
        eqb = jnp.where(s == thr, 1.0, 0.0).astype(BF16)
        ri = lax.broadcasted_iota(jnp.int32, (LANES, LANES), 0)
        ci = lax.broadcasted_iota(jnp.int32, (LANES, LANES), 1)
        in_row = jnp.dot(eqb, jnp.where(ri < ci, 1.0, 0.0).astype(BF16), preferred_element_type=F32)
        row_tot = jnp.dot(eqb, jnp.ones((LANES, LANES), BF16), preferred_element_type=F32)
        rr = lax.broadcasted_iota(jnp.int32, (n_rows, n_rows), 0)
        rc = lax.broadcasted_iota(jnp.int32, (n_rows, n_rows), 1)
        before = jnp.dot(jnp.where(rc < rr, 1.0, 0.0).astype(BF16), row_tot.astype(BF16),
                         preferred_element_type=F32)
        rank = in_row + before
        keep = (s > thr) | ((s == thr) & (rank < n_tie))
        sel_ref[0] = jnp.where(keep & adm, 1.0, 0.0)


def _sample_attend_body(pt_ref, *refs, n_pages):
    P = PAGES_PER_STEP
    k_refs, v_refs = refs[0:P], refs[P:2 * P]
    sel_ref, q_ref, kn_ref, vn_ref, o_ref, m_ref, l_ref, acc_ref = refs[2 * P:]
    g = pl.program_id(1)
    n_groups = n_pages // P

    @pl.when(g == 0)
    def _():
        q = q_ref[0].astype(F32)
        s_own = jnp.sum(q * kn_ref[0].astype(F32), axis=1, keepdims=True)
        own_sel = sel_ref[0, n_pages:n_pages + 1, 0:1]
        m_ref[...] = jnp.broadcast_to(s_own + jnp.where(own_sel > 0.5, 0.0, NEG_BIAS), m_ref.shape)
        l_ref[...] = jnp.ones(l_ref.shape, F32)
        acc_ref[...] = jnp.broadcast_to(vn_ref[0].astype(F32), acc_ref.shape)

    q = q_ref[0]
    sel = sel_ref[0, pl.ds(pl.multiple_of(g * P, P), P), :]
    scores = []
    for i in range(P):
        s = jnp.dot(q, k_refs[i][...].astype(BF16), preferred_element_type=F32)
        scores.append(jnp.where(sel[i:i + 1, :] > 0.5, s, NEG_BIAS))
    s_all = jnp.concatenate(scores, axis=1)
    m_old = m_ref[...]
    m_new = jnp.maximum(m_old, jnp.max(s_all, axis=1, keepdims=True))
    alpha = jnp.exp2(m_old - m_new)
    p32 = jnp.exp2(s_all - m_new[:, 0:1])
    l_ref[...] = alpha * l_ref[...] + jnp.sum(p32, axis=1, keepdims=True)
    m_ref[...] = m_new
    p = p32.astype(BF16)
    pv = jnp.zeros(acc_ref.shape, F32)
    for i in range(P):
        pv = pv + lax.dot_general(p[:, i * LANES:(i + 1) * LANES], v_refs[i][...].astype(BF16), _NT,
                                  preferred_element_type=F32)
    acc_ref[...] = alpha[:, 0:1] * acc_ref[...] + pv

    @pl.when(g == n_groups - 1)
    def _():
        o_ref[0] = acc_ref[...] / l_ref[:, 0:1]


def _attn_sample(page_table, layer, cache_kit, cache_kt, cache_vt, qi, w, ki_new, q, k_new, v_new):
    db, n_pages = page_table.shape
    P = PAGES_PER_STEP
    n_groups = n_pages // P
    page = cache_kit.shape[3]
    kv_w = N_KV_HEADS * HEAD_DIM
    top_k = min(TOPK_MAX, (n_pages * page + 1) // 4)
    n_rows = -(-(n_pages + 1) // SUBLANES) * SUBLANES
    pt = page_table.reshape(-1)
    params = pltpu.CompilerParams(dimension_semantics=("arbitrary", "arbitrary"), vmem_limit_bytes=VMEM_LIMIT)

    def page_map(i):
        return lambda b, g, pt: (layer, pt[b * n_pages + g * P + i], 0, 0)

    per_b = lambda b, g, pt: (b, 0, 0)
    sel = pl.pallas_call(
        functools.partial(_sample_select_body, n_pages=n_pages, top_k=top_k),
        grid_spec=pltpu.PrefetchScalarGridSpec(
            num_scalar_prefetch=1,
            grid=(db, n_groups),
            in_specs=([pl.BlockSpec((None, None, IDX_DIM, page), page_map(i)) for i in range(P)]
                      + [pl.BlockSpec((1,) + a.shape[1:], per_b) for a in (qi, w, ki_new)]),
            out_specs=pl.BlockSpec((1, n_rows, LANES), per_b),
            scratch_shapes=[pltpu.VMEM((n_rows, LANES), F32)]),
        out_shape=jax.ShapeDtypeStruct((db, n_rows, LANES), F32),
        compiler_params=params,
        name="sample_select",
    )(pt, *([cache_kit] * P), qi, w, ki_new)

    out = pl.pallas_call(
        functools.partial(_sample_attend_body, n_pages=n_pages),
        grid_spec=pltpu.PrefetchScalarGridSpec(
            num_scalar_prefetch=1,
            grid=(db, n_groups),
            in_specs=([pl.BlockSpec((None, None, kv_w, page), page_map(i)) for i in range(P)]
                      + [pl.BlockSpec((None, None, kv_w, page), page_map(i)) for i in range(P)]
                      + [pl.BlockSpec((1,) + a.shape[1:], per_b) for a in (sel, q, k_new, v_new)]),
            out_specs=pl.BlockSpec((1, N_HEADS, kv_w), per_b),
            scratch_shapes=[
                pltpu.VMEM((N_HEADS, LANES), F32),
                pltpu.VMEM((N_HEADS, LANES), F32),
                pltpu.VMEM((N_HEADS, kv_w), F32),
            ]),
        out_shape=jax.ShapeDtypeStruct((db, N_HEADS, kv_w), F32),
        compiler_params=params,
        name="sample_attend",
    )(pt, *([cache_kt] * P), *([cache_vt] * P), sel, q, k_new, v_new)
    return jnp.concatenate([out[:, h, (h // 2) * HEAD_DIM:(h // 2 + 1) * HEAD_DIM] for h in range(N_HEADS)], axis=1)


def _back_body(x_ref, att_ref, atts_ref, ro_ref, ros_ref, gr_ref, ga_ref, gb_ref, grec_ref,
               wpa_ref, wpb_ref, wo_ref, y_ref, *, n_prompt_tiles):
    is_sample = pl.program_id(0) >= n_prompt_tiles
    ro = jnp.where(is_sample, ros_ref[...], ro_ref[...])
    att = jnp.where(is_sample, atts_ref[...], att_ref[...])
    gr = gr_ref[...]
    recs = []
    for h in range(N_REC_HEADS):
        r = ro[:, h * LANES:(h + 1) * LANES]
        r = r * lax.rsqrt(jnp.mean(r * r, axis=-1, keepdims=True) + EPS) * grec_ref[...]
        recs.append((r * gr[:, h * LANES:(h + 1) * LANES]).astype(BF16))
    rec = jnp.concatenate(recs, axis=1)
    a = jnp.dot(att, wpa_ref[...], preferred_element_type=F32)
    b = jnp.dot(rec, wpb_ref[...], preferred_element_type=F32)
    mix = (ga_ref[...] * a + gb_ref[...] * b).astype(BF16)
    y_ref[...] = x_ref[...] + jnp.dot(mix, wo_ref[...], preferred_element_type=F32)


def _back(x_all, att_p, att_s, ro_p, ro_s, gr, ga, gb, grec, wpa, wpb, wo):
    n_rows, d = x_all.shape
    n_prompt_tiles = att_p.shape[0] // TOK_TILE
    row = lambda i: (i, 0)
    prow = lambda i: (jnp.minimum(i, n_prompt_tiles - 1), 0)
    const = lambda i: (0, 0)
    return pl.pallas_call(
        functools.partial(_back_body, n_prompt_tiles=n_prompt_tiles),
        grid=(n_rows // TOK_TILE,),
        in_specs=[
            pl.BlockSpec((TOK_TILE, d), row),
            pl.BlockSpec((TOK_TILE, 512), prow), pl.BlockSpec((TOK_TILE, 512), const),
            pl.BlockSpec((TOK_TILE, 512), prow), pl.BlockSpec((TOK_TILE, 512), const),
            pl.BlockSpec((TOK_TILE, 512), row),
            pl.BlockSpec((TOK_TILE, d), row), pl.BlockSpec((TOK_TILE, d), row),
            pl.BlockSpec((1, LANES), const),
            pl.BlockSpec(wpa.shape, const), pl.BlockSpec(wpb.shape, const), pl.BlockSpec(wo.shape, const),
        ],
        out_specs=pl.BlockSpec((TOK_TILE, d), row),
        out_shape=jax.ShapeDtypeStruct((n_rows, d), F32),
        compiler_params=pltpu.CompilerParams(
            dimension_semantics=("arbitrary",), vmem_limit_bytes=VMEM_LIMIT),
        name="mixer_back",
    )(x_all, att_p, att_s, ro_p, ro_s, gr, ga, gb, grec, wpa, wpb, wo)


def _ffn_body(y_ref, g_ref, wgu_ref, wd_ref, o_ref, *, d_ff):
    y = y_ref[...]
    hb = (y * lax.rsqrt(jnp.mean(y * y, axis=-1, keepdims=True) + EPS) * g_ref[...]).astype(BF16)
    acc = y
    for c in range(d_ff // FFN_CHUNK):
        c0 = c * FFN_CHUNK
        gate = jnp.dot(hb, wgu_ref[:, c0:c0 + FFN_CHUNK], preferred_element_type=F32)
        up = jnp.dot(hb, wgu_ref[:, d_ff + c0:d_ff + c0 + FFN_CHUNK], preferred_element_type=F32)
        act = (gate * _sigmoid(gate) * up).astype(BF16)
        acc = acc + jnp.dot(act, wd_ref[c0:c0 + FFN_CHUNK, :], preferred_element_type=F32)
    o_ref[...] = acc


def _ffn(y, g, wgu, wd):
    n_rows, d = y.shape
    d_ff = wd.shape[0]
    row = lambda i: (i, 0)
    const = lambda i: (0, 0)
    return pl.pallas_call(
        functools.partial(_ffn_body, d_ff=d_ff),
        grid=(n_rows // TOK_TILE,),
        in_specs=[
            pl.BlockSpec((TOK_TILE, d), row), pl.BlockSpec((1, d), const),
            pl.BlockSpec(wgu.shape, const, pipeline_mode=pl.Buffered(1)),
            pl.BlockSpec(wd.shape, const, pipeline_mode=pl.Buffered(1)),
        ],
        out_specs=pl.BlockSpec((TOK_TILE, d), row),
        out_shape=jax.ShapeDtypeStruct((n_rows, d), F32),
        compiler_params=pltpu.CompilerParams(
            dimension_semantics=("arbitrary",), vmem_limit_bytes=VMEM_LIMIT),
        name="ffn",
    )(y, g, wgu, wd)


def _rope_table(pos):
    half = ROT_DIM // 2
    inv = jnp.power(ROPE_THETA, -2.0 * jnp.arange(half, dtype=F32) / ROT_DIM)
    ang = pos.astype(F32)[:, None] * inv[None, :]
    cos, sin = jnp.cos(ang), jnp.sin(ang)
    n = pos.shape[0]
    one = jnp.ones((n, HEAD_DIM - ROT_DIM), F32)
    zero8 = jnp.zeros((n, half), F32)
    zero = jnp.zeros((n, HEAD_DIM - ROT_DIM), F32)
    c64 = jnp.concatenate([cos, cos, one], axis=1)
    up64 = jnp.concatenate([zero8, sin, zero], axis=1)
    dn64 = jnp.concatenate([-sin, zero8, zero], axis=1)
    return jnp.concatenate([c64, c64, up64, up64, dn64, dn64], axis=1)


def _pack_w_in(w):
    d = w.shape[0]
    cuts = (512, 768, 1024, 1536, 1600, 1608, 2120, 2632, 3144, 3656, 4680)
    q, k, v, qi, ki, wi, qr, fr, ir, gr, ga, gb = jnp.split(w, cuts, axis=1)
    pad = jnp.zeros((d, LANES - IDX_DIM - N_IDX_HEADS), w.dtype)
    return jnp.concatenate([q, k, v, qi, ki, wi, pad, qr, fr, ir, gr, ga, gb], axis=1).astype(BF16)


def kernel(x_prompt, x_sample, cache_k, cache_v, cache_idx_k, state_rec, page_table, meta_tokens,
           w_in, norm_mix, q_norm, k_norm, lb_raw, rec_norm, w_pa, w_pb, w_o, norm_ffn, w_gu, w_down):
    batch, seq, d = x_prompt.shape
    db = x_sample.shape[0]
    depth = w_in.shape[0]
    n_pages = page_table.shape[1]
    page = cache_k.shape[2]
    past_len = n_pages * page
    t_real = seq + N_META
    t_pad = -(-t_real // TOK_TILE) * TOK_TILE
    tiles_per_seq = t_pad // TOK_TILE
    n_prompt = batch * t_pad
    top_k = min(TOPK_MAX, seq // 4)

    sm = jax.nn.softmax(lb_raw.astype(F32), axis=0)
    lower_bounds = jnp.cumsum(sm, axis=0) - sm[0:1]

    seq_pad = jnp.zeros((t_pad - t_real, d), F32)
    pieces = []
    for b in range(batch):
        pieces += [meta_tokens.astype(F32), x_prompt[b], seq_pad]
    pieces += [x_sample.reshape(db, d), jnp.zeros((TOK_TILE - db, d), F32)]
    x_all = jnp.concatenate(pieces, axis=0)

    rope_tab = jnp.concatenate([
        _rope_table(jnp.arange(t_pad, dtype=jnp.int32)),
        _rope_table(jnp.full((TOK_TILE,), past_len, jnp.int32))], axis=0)

    n_pool = cache_k.shape[1]
    cache_kt = jnp.transpose(cache_k, (0, 1, 3, 4, 2)).reshape(depth, n_pool, N_KV_HEADS * HEAD_DIM, page)
    cache_vt = jnp.transpose(cache_v, (0, 1, 3, 4, 2)).reshape(depth, n_pool, N_KV_HEADS * HEAD_DIM, page)
    cache_kit = jnp.transpose(cache_idx_k, (0, 1, 3, 2))
    srows = slice(n_prompt, n_prompt + db)
    tile2 = lambda g_: jnp.concatenate([g_, g_]).reshape(1, LANES).astype(F32)

    pk, pv, pki, ps, sk, sv, ski, ss = [], [], [], [], [], [], [], []
    for l in range(depth):
        pj = _inproj(x_all, norm_mix[l].reshape(1, d).astype(F32), _pack_w_in(w_in[l]), rope_tab,
                     tile2(q_norm[l]), tile2(k_norm[l]), lower_bounds[l].reshape(1, -1),
                     tiles_per_seq, n_prompt // TOK_TILE)

        att_p = _attn_prompt(pj, batch, t_pad, t_real, top_k)
        ro_p, st_p = _gla_prompt(pj, batch, t_pad, t_real)

        q_s = pj["q"][srows].reshape(db, N_HEADS, LANES)
        zero_s = jnp.zeros_like(q_s)
        q_s = jnp.stack([jnp.concatenate([q_s[:, h] if h // 4 == s else zero_s[:, h] for s in range(2)], axis=1)
                         for h in range(N_HEADS)], axis=1)
        qi_s = pj["qi"][srows].reshape(db, N_IDX_HEADS, IDX_DIM)
        w_s = pj["kiwi"][srows, IDX_DIM:IDX_DIM + N_IDX_HEADS].reshape(db, N_IDX_HEADS, 1)
        ki_s = pj["kk"][srows, 0:IDX_DIM].reshape(db, 1, IDX_DIM)
        kn_s = pj["kb"][srows].reshape(db, 1, N_KV_HEADS * HEAD_DIM)
        vn_s = pj["vb"][srows].reshape(db, 1, N_KV_HEADS * HEAD_DIM)
        att_s = _attn_sample(page_table, l, cache_kit, cache_kt, cache_vt, qi_s, w_s, ki_s, q_s, kn_s, vn_s)
        ro_s, st_s = _gla_sample(pj["rq"][srows], pj["f"][srows], pj["rv"][srows], state_rec[l])

        pad_s = lambda a: jnp.pad(a, ((0, TOK_TILE - db), (0, 0)))
        y = _back(x_all, att_p, pad_s(att_s.astype(BF16)), ro_p, pad_s(ro_s),
                  pj["gr"], pj["ga"], pj["gb"], rec_norm[l].reshape(1, LANES).astype(F32),
                  w_pa[l].astype(BF16), w_pb[l].astype(BF16), w_o[l].astype(BF16))
        x_all = _ffn(y, norm_ffn[l].reshape(1, d).astype(F32), w_gu[l].astype(BF16), w_down[l].astype(BF16))

        seq_view = lambda a, wd: a[:n_prompt].reshape(batch, t_pad, wd)[:, :t_real]
        pk.append(seq_view(pj["kf"], 256).reshape(batch, t_real, N_KV_HEADS, HEAD_DIM))
        pv.append(seq_view(pj["vf"], 256).reshape(batch, t_real, N_KV_HEADS, HEAD_DIM))
        pki.append(seq_view(pj["kiwi"], LANES)[..., :IDX_DIM])
        ps.append(st_p)
        sk.append(pj["kf"][srows].reshape(db, 1, N_KV_HEADS, HEAD_DIM))
        sv.append(pj["vf"][srows].reshape(db, 1, N_KV_HEADS, HEAD_DIM))
        ski.append(pj["kiwi"][srows, :IDX_DIM].reshape(db, 1, IDX_DIM))
        ss.append(st_s)

    y_prompt = jnp.stack([x_all[b * t_pad + N_META:b * t_pad + t_real] for b in range(batch)])
    y_sample = x_all[srows].reshape(db, 1, d)
    return (y_prompt, y_sample, jnp.stack(pk), jnp.stack(pv), jnp.stack(pki), jnp.stack(ps),
            jnp.stack(sk), jnp.stack(sv), jnp.stack(ski), jnp.stack(ss))
```

```python
import functools

import jax
import jax.numpy as jnp
from jax import lax
from jax.experimental import pallas as pl
from jax.experimental.pallas import tpu as pltpu

F32 = jnp.float32
BF16 = jnp.bfloat16

N_META = 16
N_HEADS = 8
HEAD_DIM = 64
N_KV_HEADS = 4
ROT_DIM = 16
ROPE_THETA = 500000.0
N_IDX_HEADS = 8
IDX_DIM = 64
IDX_W_SCALE = (N_IDX_HEADS * IDX_DIM) ** -0.5
TOPK_MAX = 256
N_REC_HEADS = 4
REC_DIM = 128
EPS = 1e-6
Q_SCALE = HEAD_DIM ** -0.5 * 1.4426950408889634

LANES = 128
SUBLANES = 8

TOK_TILE = 384
Q_TILE = 384
KEY_CHUNK = 128
KEY_GROUP = 3
REC_CHUNK = 64
REC_CHUNKS_PER_STEP = 3
MATMUL_MID_LEVELS = 3
PAGES_PER_STEP = 16
FFN_CHUNK = 256
VMEM_LIMIT = 56 * 1024 * 1024

NEG_SCORE = -3.0e38
NEG_HALF = -1.5e38
NEG_BIAS = -1.0e30
BIG = 3.0e38
BISECTIONS_PER_ROUND = 20
MAX_ROUNDS = 160

C_Q, C_K, C_V, C_QI, C_KIWI, C_RQ, C_FR, C_IR, C_GR, C_GA, C_GB, C_END = (
    0, 512, 768, 1024, 1536, 1664, 2176, 2688, 3200, 3712, 4736, 5760)

_NT = (((1,), (1,)), ((), ()))
_TN = (((0,), (0,)), ((), ()))


def _sigmoid(x):
    return 1.0 / (1.0 + jnp.exp(-x))


def _split3(x):
    hi = x.astype(BF16)
    r1 = x - hi.astype(F32)
    mid = r1.astype(BF16)
    lo = (r1 - mid.astype(F32)).astype(BF16)
    return jnp.concatenate([hi, mid, lo], axis=1)


def _join3(r):
    return r[:, 0:LANES] + r[:, LANES:2 * LANES] + r[:, 2 * LANES:3 * LANES]


def _inproj_body(x_ref, g_ref, w_ref, rope_ref, qn_ref, kn_ref, lb_ref,
                 q_ref, kf_ref, vf_ref, kb_ref, vb_ref, qi_ref, kiwi_ref, kk_ref,
                 rq_ref, f_ref, rv_ref, gr_ref, ga_ref, gb_ref):
    x = x_ref[...]
    h = x * lax.rsqrt(jnp.mean(x * x, axis=-1, keepdims=True) + EPS) * g_ref[...]
    hb = h.astype(BF16)
    rows = x.shape[0]
    lane = lax.broadcasted_iota(jnp.int32, (rows, LANES), 1)
    low = lane < HEAD_DIM
    cosv = rope_ref[:, 0:LANES]
    sin_up = rope_ref[:, LANES:2 * LANES]
    sin_dn = rope_ref[:, 2 * LANES:3 * LANES]

    def proj(c0, c1):
        return jnp.dot(hb, w_ref[:, c0:c1], preferred_element_type=F32)

    def rope(xs):
        return (xs * cosv + pltpu.roll(xs, ROT_DIM // 2, 1) * sin_up
                + pltpu.roll(xs, LANES - ROT_DIM // 2, 1) * sin_dn)

    def headnorm(xs, gain):
        sq = xs * xs
        s_lo = jnp.sum(jnp.where(low, sq, 0.0), axis=-1, keepdims=True)
        s_hi = jnp.sum(jnp.where(low, 0.0, sq), axis=-1, keepdims=True)
        ms = jnp.where(low, s_lo, s_hi) * (1.0 / HEAD_DIM)
        return xs * lax.rsqrt(ms + EPS) * gain

    qraw = proj(C_Q, C_K)
    for p in range(N_KV_HEADS):
        qs = rope(headnorm(qraw[:, p * LANES:(p + 1) * LANES], qn_ref[...])) * Q_SCALE
        qr = pltpu.roll(qs, HEAD_DIM, 1)
        if p % 2 == 0:
            h0, h1 = jnp.where(low, qs, 0.0), jnp.where(low, qr, 0.0)
        else:
            h0, h1 = jnp.where(low, 0.0, qr), jnp.where(low, 0.0, qs)
        q_ref[:, (2 * p) * LANES:(2 * p + 1) * LANES] = h0.astype(BF16)
        q_ref[:, (2 * p + 1) * LANES:(2 * p + 2) * LANES] = h1.astype(BF16)

    kraw = proj(C_K, C_V)
    for s in range(2):
        ks = rope(headnorm(kraw[:, s * LANES:(s + 1) * LANES], kn_ref[...]))
        kf_ref[:, s * LANES:(s + 1) * LANES] = ks
        kb_ref[:, s * LANES:(s + 1) * LANES] = ks.astype(BF16)

    vraw = proj(C_V, C_QI)
    vf_ref[...] = vraw
    vb_ref[...] = vraw.astype(BF16)

    qiraw = proj(C_QI, C_KIWI)
    for p in range(4):
        qi_ref[:, p * LANES:(p + 1) * LANES] = rope(qiraw[:, p * LANES:(p + 1) * LANES]).astype(BF16)

    kiwi = proj(C_KIWI, C_RQ)
    kir = rope(kiwi)
    kiwi_ref[...] = jnp.where(low, kir, kiwi * IDX_W_SCALE)
    kk_ref[...] = jnp.where(low, kir, pltpu.roll(kir, HEAD_DIM, 1)).astype(BF16)

    qr_ = proj(C_RQ, C_FR)
    rq_ref[...] = qr_ * _sigmoid(qr_)
    lb = lb_ref[...]
    f_ref[...] = lb + (1.0 - lb) * _sigmoid(proj(C_FR, C_IR))
    rv_ref[...] = proj(C_IR, C_GR)
    g_ = proj(C_GR, C_GA)
    gr_ref[...] = g_ * _sigmoid(g_)
    ga_ref[...] = _sigmoid(proj(C_GA, C_GB))
    gb_ref[...] = _sigmoid(proj(C_GB, C_END))


def _inproj(x_all, g, w, rope_tab, qn, kn, lb, tiles_per_seq, n_prompt_tiles):
    n_rows, d = x_all.shape
    n_tiles = n_rows // TOK_TILE
    row = lambda i: (i, 0)
    const = lambda i: (0, 0)
    rope_map = lambda i: (jnp.where(i < n_prompt_tiles, i % tiles_per_seq, tiles_per_seq), 0)
    widths = dict(q=(1024, BF16), kf=(256, F32), vf=(256, F32), kb=(256, BF16), vb=(256, BF16),
                  qi=(512, BF16), kiwi=(128, F32), kk=(128, BF16), rq=(512, F32), f=(512, F32),
                  rv=(512, F32), gr=(512, F32), ga=(1024, F32), gb=(1024, F32))
    out_shape = [jax.ShapeDtypeStruct((n_rows, wd), dt) for wd, dt in widths.values()]
    out_specs = [pl.BlockSpec((TOK_TILE, wd), row) for wd, _ in widths.values()]
    outs = pl.pallas_call(
        _inproj_body,
        grid=(n_tiles,),
        in_specs=[
            pl.BlockSpec((TOK_TILE, d), row),
            pl.BlockSpec((1, d), const),
            pl.BlockSpec((d, C_END), const, pipeline_mode=pl.Buffered(1)),
            pl.BlockSpec((TOK_TILE, 3 * LANES), rope_map),
            pl.BlockSpec((1, LANES), const),
            pl.BlockSpec((1, LANES), const),
            pl.BlockSpec((1, 512), const),
        ],
        out_specs=out_specs,
        out_shape=out_shape,
        compiler_params=pltpu.CompilerParams(
            dimension_semantics=("arbitrary",), vmem_limit_bytes=VMEM_LIMIT),
        name="inproj",
    )(x_all, g, w, rope_tab, qn, kn, lb)
    return dict(zip(widths.keys(), outs))


def _select_threshold(count_ge, bracket_minmax, lo0, hi0, n_adm, need, top_k):
    def halve(st):
        lo, hi, c_lo, c_hi, _ = st
        mid = 0.5 * (lo + hi)
        c_mid = count_ge(mid)
        ge = c_mid >= top_k
        lo2, c_lo2 = jnp.where(ge, mid, lo), jnp.where(ge, c_mid, c_lo)
        hi2, c_hi2 = jnp.where(ge, hi, mid), jnp.where(ge, c_hi, c_mid)
        r = top_k - c_hi2
        stop = (c_lo2 - c_hi2 == r) | (r == 1.0) | (mid <= lo) | (mid >= hi) | jnp.logical_not(need)
        return lo2, hi2, c_lo2, c_hi2, jnp.where(stop, 1.0, 0.0)

    def inner_cond(st):
        return (jnp.min(st[4]) < 0.5) & (st[5] < BISECTIONS_PER_ROUND)

    def inner_body(st):
        return halve(halve(st[:5])) + (st[5] + 2,)

    def outer_cond(st):
        return (jnp.min(st[4]) < 0.5) & (st[7] < MAX_ROUNDS)

    def outer_body(st):
        lo, hi, c_lo, c_hi, done = lax.while_loop(inner_cond, inner_body, st[:5] + (jnp.int32(0),))[:5]
        mn, mx = bracket_minmax(lo, hi)
        open_ = need & (done < 0.5)
        done = jnp.where(open_ & (mn == mx), 1.0, done)
        return jnp.where(open_, mn, lo), hi, c_lo, c_hi, done, mn, mx, st[7] + 1

    zero = jnp.zeros_like(lo0)
    lo, hi, c_lo, c_hi, _, mn, mx, _ = lax.while_loop(
        outer_cond, outer_body, (lo0, hi0, n_adm, zero, zero, zero, zero, jnp.int32(0)))
    r = top_k - c_hi
    thr = jnp.where(need, jnp.where(r == 1.0, mx, mn), NEG_SCORE)
    n_tie = jnp.where(need, jnp.where(c_lo - c_hi != r, r, BIG), 0.0)
    return thr, n_tie


def _attn_body(q_ref, qi_ref, wq_ref, kb_ref, vb_ref, kk_ref, o_ref,
               sc_ref, qia_ref, qa_ref, vt_ref, m_ref, l_ref, acc_ref,
               da_ref, db_ref, sa_ref, sb_ref, *, top_k, t_real):
    j = pl.program_id(1)
    T = KEY_CHUNK
    Q = Q_TILE
    nchunks = (j + 1) * (Q // T)
    low = lax.broadcasted_iota(jnp.int32, (Q, LANES), 1) < HEAD_DIM

    @pl.when(j == 0)
    def _():
        def transpose_v(c, carry):
            blk = vb_ref[pl.ds(pl.multiple_of(c * T, T), T), :].astype(F32)
            for s in range(2):
                vt_ref[c, s] = blk[:, s * LANES:(s + 1) * LANES].T.astype(BF16)
            return carry
        lax.fori_loop(0, vt_ref.shape[0], transpose_v, 0)

    wt = wq_ref[...].T
    qi = qi_ref[...]
    for h in range(N_IDX_HEADS):
        slab = qi[:, (h // 2) * LANES:(h // 2 + 1) * LANES]
        zero = jnp.zeros_like(slab)
        qia_ref[h * Q:(h + 1) * Q, :] = jnp.where(low, slab, zero) if h % 2 == 0 else jnp.where(low, zero, slab)
        qa_ref[h * Q:(h + 1) * Q, :] = q_ref[:, h * LANES:(h + 1) * LANES]

    G = KEY_GROUP
    ngroups = (nchunks + G - 1) // G
    last_chunk = vt_ref.shape[0] - 1

    def chunk_rows(ref, g, u, lanes):
        cc = jnp.minimum(g * G + u, last_chunk)
        return ref[pl.ds(pl.multiple_of(cc * T, T), T), lanes]

    def fold(x, op):
        return op(x.reshape(x.shape[0] // SUBLANES, SUBLANES, Q), axis=0)

    npairs = ngroups // 2
    odd_tail = ngroups % 2 == 1

    def idx_dots(dst, g):
        kk = jnp.concatenate([chunk_rows(kk_ref, g, u, slice(None)) for u in range(G)], axis=0)
        dst[...] = lax.dot_general(kk, qia_ref[...], _NT, preferred_element_type=F32)

    def idx_scores(src, g, carry):
        mn, mx = carry
        s = jnp.zeros((G * T, Q), F32)
        for h in range(N_IDX_HEADS):
            s = s + wt[IDX_DIM + h:IDX_DIM + h + 1, :] * jnp.maximum(src[:, h * Q:(h + 1) * Q], 0.0)
        kpos = g * (G * T) + lax.broadcasted_iota(jnp.int32, (G * T, Q), 0)
        adm = kpos <= j * Q + lax.broadcasted_iota(jnp.int32, (G * T, Q), 1)
        sc_ref[pl.ds(pl.multiple_of(g * (G * T), G * T), G * T), :] = jnp.where(adm, s, NEG_SCORE)
        return (jnp.minimum(mn, fold(jnp.where(adm, s, BIG), jnp.min)),
                jnp.maximum(mx, fold(jnp.where(adm, s, -BIG), jnp.max)))

    def phase_a(i, carry):
        idx_dots(db_ref, 2 * i + 1)
        carry = idx_scores(da_ref, 2 * i, carry)
        idx_dots(da_ref, 2 * i + 2)
        return idx_scores(db_ref, 2 * i + 1, carry)

    idx_dots(da_ref, 0)
    rng = lax.fori_loop(0, npairs, phase_a,
                        (jnp.full((SUBLANES, Q), BIG, F32), jnp.full((SUBLANES, Q), -BIG, F32)))
    mn, mx = lax.cond(odd_tail, lambda c: idx_scores(da_ref, ngroups - 1, c), lambda c: c, rng)

    def sc_group(g):
        return sc_ref[pl.ds(pl.multiple_of(g * (G * T), G * T), G * T), :]

    def count_ge(x):
        def step(g, acc):
            return acc + fold(jnp.where(sc_group(g) >= x, 1.0, 0.0), jnp.sum)
        return jnp.sum(lax.fori_loop(0, ngroups, step, jnp.zeros((SUBLANES, Q), F32)), axis=0, keepdims=True)

    def bracket_minmax(lo, hi):
        def step(g, carry):
            a, b = carry
            s = sc_group(g)
            inb = (s >= lo) & (s < hi)
            return (jnp.minimum(a, fold(jnp.where(inb, s, BIG), jnp.min)),
                    jnp.maximum(b, fold(jnp.where(inb, s, -BIG), jnp.max)))
        a, b = lax.fori_loop(0, ngroups, step,
                             (jnp.full((SUBLANES, Q), BIG, F32), jnp.full((SUBLANES, Q), -BIG, F32)))
        return jnp.min(a, axis=0, keepdims=True), jnp.max(b, axis=0, keepdims=True)

    qrow = j * Q + lax.broadcasted_iota(jnp.int32, (1, Q), 1)
    n_adm = (qrow + 1).astype(F32)
    need = (qrow + 1 > top_k) & (qrow < t_real)
    top = jnp.max(mx, axis=0, keepdims=True)
    thr, n_tie = _select_threshold(count_ge, bracket_minmax, jnp.min(mn, axis=0, keepdims=True),
                                   top + (jnp.abs(top) + 1.0), n_adm, need, float(top_k))

    m_ref[...] = jnp.full(m_ref.shape, NEG_SCORE, F32)
    l_ref[...] = jnp.zeros(l_ref.shape, F32)
    acc_ref[...] = jnp.zeros(acc_ref.shape, F32)
    earlier = jnp.where(lax.broadcasted_iota(jnp.int32, (T, T), 1) < lax.broadcasted_iota(jnp.int32, (T, T), 0),
                        1.0, 0.0).astype(BF16)

    def qk_dots(dst, g):
        for sl in range(2):
            lanes = slice(sl * LANES, (sl + 1) * LANES)
            kc = jnp.concatenate([chunk_rows(kb_ref, g, u, lanes) for u in range(G)], axis=0)
            dst[sl] = lax.dot_general(kc, qa_ref[sl * 4 * Q:(sl + 1) * 4 * Q, :], _NT,
                                      preferred_element_type=F32)

    def softmax_pv(src, g, seen, issue_next):
        s = sc_group(g)
        eq = s == thr
        eqf = jnp.where(eq, 1.0, 0.0)
        eq_l = jnp.concatenate([eqf[u * T:(u + 1) * T] for u in range(G)], axis=1).astype(BF16)
        in_chunk = jnp.dot(earlier, eq_l, preferred_element_type=F32)
        ranks = []
        for u in range(G):
            ranks.append(in_chunk[:, u * Q:(u + 1) * Q] + seen)
            seen = seen + jnp.sum(eqf[u * T:(u + 1) * T], axis=0, keepdims=True)
        rank = jnp.concatenate(ranks, axis=0)
        bias = jnp.where(s > thr, 0.0, jnp.where(eq & (rank < n_tie), 0.0, NEG_BIAS))
        bias4 = jnp.concatenate([bias] * 4, axis=1)
        sts, m_news, alphas = [], [], []
        for sl in range(2):
            st = src[sl] + bias4
            m_old = m_ref[sl]
            m_new = jnp.maximum(m_old, jnp.max(st, axis=0, keepdims=True))
            m_ref[sl] = m_new
            sts.append(st)
            m_news.append(m_new)
            alphas.append(jnp.exp2(m_old - m_new))
        issue_next()
        for sl in range(2):
            p = jnp.exp2(sts[sl] - m_news[sl])
            l_ref[sl] = alphas[sl] * l_ref[sl] + jnp.sum(p, axis=0, keepdims=True)
            vt = jnp.concatenate([vt_ref[jnp.minimum(g * G + u, last_chunk), sl] for u in range(G)], axis=1)
            acc_ref[sl] = alphas[sl] * acc_ref[sl] + jnp.dot(vt, p.astype(BF16), preferred_element_type=F32)
        return seen

    def phase_c(i, seen):
        seen = softmax_pv(sa_ref, 2 * i, seen, lambda: qk_dots(sb_ref, 2 * i + 1))
        return softmax_pv(sb_ref, 2 * i + 1, seen, lambda: qk_dots(sa_ref, 2 * i + 2))

    qk_dots(sa_ref, 0)
    seen = lax.fori_loop(0, npairs, phase_c, jnp.zeros((1, Q), F32))

    @pl.when(odd_tail)
    def _():
        softmax_pv(sa_ref, ngroups - 1, seen, lambda: None)

    for p in range(N_KV_HEADS):
        sl, hh = (2 * p) // 4, (2 * p) % 4
        out = acc_ref[sl] / l_ref[sl]
        a0 = out[:, hh * Q:(hh + 1) * Q].T
        a1 = out[:, (hh + 1) * Q:(hh + 2) * Q].T
        if p % 2 == 0:
            slab = jnp.where(low, a0, pltpu.roll(a1, HEAD_DIM, 1))
        else:
            slab = jnp.where(low, pltpu.roll(a0, HEAD_DIM, 1), a1)
        o_ref[:, p * LANES:(p + 1) * LANES] = slab.astype(BF16)


def _attn_prompt(pj, batch, t_pad, t_real, top_k):
    nq = t_pad // Q_TILE
    nkc = t_pad // KEY_CHUNK
    grp_rows = KEY_GROUP * KEY_CHUNK
    n_grp = 2 * -(-(-(-nkc // KEY_GROUP)) // 2)
    qrow = lambda b, j: (b * nq + j, 0)
    seq = lambda b, j: (b, 0)
    return pl.pallas_call(
        functools.partial(_attn_body, top_k=top_k, t_real=t_real),
        grid=(batch, nq),
        in_specs=[
            pl.BlockSpec((Q_TILE, 1024), qrow),
            pl.BlockSpec((Q_TILE, 512), qrow),
            pl.BlockSpec((Q_TILE, LANES), qrow),
            pl.BlockSpec((t_pad, 256), seq, pipeline_mode=pl.Buffered(1)),
            pl.BlockSpec((t_pad, 256), seq, pipeline_mode=pl.Buffered(1)),
            pl.BlockSpec((t_pad, LANES), seq, pipeline_mode=pl.Buffered(1)),
        ],
        out_specs=pl.BlockSpec((Q_TILE, 512), qrow),
        out_shape=jax.ShapeDtypeStruct((batch * t_pad, 512), BF16),
        scratch_shapes=[
            pltpu.VMEM((n_grp * grp_rows, Q_TILE), F32),
            pltpu.VMEM((N_IDX_HEADS * Q_TILE, LANES), BF16),
            pltpu.VMEM((N_HEADS * Q_TILE, LANES), BF16),
            pltpu.VMEM((nkc, 2, LANES, KEY_CHUNK), BF16),
            pltpu.VMEM((2, 1, 4 * Q_TILE), F32),
            pltpu.VMEM((2, 1, 4 * Q_TILE), F32),
            pltpu.VMEM((2, LANES, 4 * Q_TILE), F32),
            pltpu.VMEM((grp_rows, N_IDX_HEADS * Q_TILE), F32),
            pltpu.VMEM((grp_rows, N_IDX_HEADS * Q_TILE), F32),
            pltpu.VMEM((2, grp_rows, 4 * Q_TILE), F32),
            pltpu.VMEM((2, grp_rows, 4 * Q_TILE), F32),
        ],
        compiler_params=pltpu.CompilerParams(
            dimension_semantics=("arbitrary", "arbitrary"), vmem_limit_bytes=VMEM_LIMIT),
        name="attn_prompt",
    )(pj["q"], pj["qi"], pj["kiwi"], pj["kb"], pj["vb"], pj["kk"])


def _gla_body(rq_ref, f_ref, rv_ref, cum_ref, o_ref, sout_ref, st_ref, *, t_real):
    step = pl.program_id(1)
    n_steps = pl.num_programs(1)
    C = REC_CHUNK
    CH = o_ref.shape[0] // C
    n_lev = C.bit_length() - 1

    @pl.when(step == 0)
    def _():
        st_ref[...] = jnp.zeros(st_ref.shape, F32)

    trow = lax.broadcasted_iota(jnp.int32, (C, LANES), 0)
    ti = lax.broadcasted_iota(jnp.int32, (C, C), 0)
    si = lax.broadcasted_iota(jnp.int32, (C, C), 1)
    H = range(N_REC_HEADS)
    units = [(cc, h) for cc in range(CH) for h in H]
    valid = [(step * CH + cc) * C + trow < t_real for cc in range(CH)]

    def tile(ref, u):
        return ref[u[0] * C:(u[0] + 1) * C, u[1] * LANES:(u[1] + 1) * LANES]

    f = {u: tile(f_ref, u) for u in units}
    q = {u: tile(rq_ref, u) for u in units}
    vb = {u: tile(rv_ref, u).astype(BF16) for u in units}
    kk = {u: jnp.where(valid[u[0]], 1.0 - f[u], 0.0) for u in units}

    lf3 = jnp.concatenate([_split3(jnp.where(valid[u[0]], jnp.log(f[u]), 0.0)) for u in units], axis=1)
    cums = jnp.dot(cum_ref[...], lf3, preferred_element_type=F32)
    cum = {u: _join3(cums[:, i * 3 * LANES:(i + 1) * 3 * LANES]) for i, u in enumerate(units)}
    b = {u: cum[u][0:C] for u in units}

    def midpoint(u, lev):
        if lev < MATMUL_MID_LEVELS:
            return cum[u][(1 + lev) * C:(2 + lev) * C]
        half = 1 << lev
        return jnp.concatenate(
            [jnp.broadcast_to(b[u][s + half - 1:s + half, :], (2 * half, LANES)) for s in range(0, C, 2 * half)],
            axis=0)

    a = {u: jnp.where(ti == si, lax.dot_general(q[u].astype(BF16), kk[u].astype(BF16), _NT,
                                                preferred_element_type=F32), 0.0) for u in units}
    for lev in range(n_lev):
        half = 1 << lev
        up = (trow & half) != 0
        blk_mask = ((ti >> (lev + 1)) == (si >> (lev + 1))) & ((ti & half) != 0) & ((si & half) == 0)
        for u in units:
            bl = midpoint(u, lev)
            e = jnp.exp(jnp.where(up, b[u] - bl, bl - b[u]))
            qt = jnp.where(up, q[u] * e, 0.0).astype(BF16)
            kt = jnp.where(up, 0.0, kk[u] * e).astype(BF16)
            a[u] = a[u] + jnp.where(blk_mask, lax.dot_general(qt, kt, _NT, preferred_element_type=F32), 0.0)

    st = {h: st_ref[h] for h in H}
    for u in units:
        cc, h = u
        qe = (q[u] * jnp.exp(b[u])).astype(BF16)
        o = jnp.dot(a[u].astype(BF16), vb[u], preferred_element_type=F32)
        o_ref[cc * C:(cc + 1) * C, h * LANES:(h + 1) * LANES] = o + lax.dot_general(
            qe, st[h].astype(BF16), _NT, preferred_element_type=F32)
        b_last = b[u][C - 1:C, :]
        kd = (kk[u] * jnp.exp(b_last - b[u])).astype(BF16)
        st[h] = st[h] * jnp.exp(b_last) + lax.dot_general(vb[u], kd, _TN, preferred_element_type=F32)
    for h in H:
        st_ref[h] = st[h]

        @pl.when(step == n_steps - 1)
        def _(h=h):
            sout_ref[0, h] = st[h].T


def _gla_consts():
    C = REC_CHUNK
    t = jnp.arange(C)
    tri = t[None, :] <= t[:, None]
    blocks = [tri]
    for lev in range(MATMUL_MID_LEVELS):
        half = 1 << lev
        mid_row = (t >> (lev + 1)) * (2 * half) + half - 1
        blocks.append(t[None, :] <= mid_row[:, None])
    return jnp.concatenate(blocks, axis=0).astype(BF16)


def _gla_prompt(pj, batch, t_pad, t_real):
    rows = REC_CHUNK * REC_CHUNKS_PER_STEP
    nch = t_pad // rows
    cum = _gla_consts()
    row = lambda b, c: (b * nch + c, 0)
    const = lambda b, c: (0, 0)
    return pl.pallas_call(
        functools.partial(_gla_body, t_real=t_real),
        grid=(batch, nch),
        in_specs=[
            pl.BlockSpec((rows, 512), row),
            pl.BlockSpec((rows, 512), row),
            pl.BlockSpec((rows, 512), row),
            pl.BlockSpec(cum.shape, const),
        ],
        out_specs=[
            pl.BlockSpec((rows, 512), row),
            pl.BlockSpec((1, N_REC_HEADS, REC_DIM, REC_DIM), lambda b, c: (b, 0, 0, 0)),
        ],
        out_shape=[
            jax.ShapeDtypeStruct((batch * t_pad, 512), F32),
            jax.ShapeDtypeStruct((batch, N_REC_HEADS, REC_DIM, REC_DIM), F32),
        ],
        scratch_shapes=[pltpu.VMEM((N_REC_HEADS, REC_DIM, REC_DIM), F32)],
        compiler_params=pltpu.CompilerParams(
            dimension_semantics=("arbitrary", "arbitrary"), vmem_limit_bytes=VMEM_LIMIT),
        name="gla_prompt",
    )(pj["rq"], pj["f"], pj["rv"], cum)


def _gla_sample_body(rq_ref, f_ref, rv_ref, s0_ref, o_ref, sout_ref):
    b = pl.program_id(0)
    q = rq_ref[pl.ds(b, 1), :]
    f = f_ref[pl.ds(b, 1), :]
    v = rv_ref[pl.ds(b, 1), :]
    for h in range(N_REC_HEADS):
        sl = slice(h * LANES, (h + 1) * LANES)
        col = lambda r: jnp.broadcast_to(r[:, sl], (REC_DIM, LANES)).T
        s_new = col(f) * s0_ref[0, h] + col(1.0 - f) * v[:, sl]
        sout_ref[0, h] = s_new
        o_ref[0, :, sl] = jnp.sum(col(q) * s_new, axis=0, keepdims=True)


def _gla_sample(rq, f, rv, state):
    db = state.shape[0]
    full = lambda b: (0, 0)
    st = lambda b: (b, 0, 0, 0)
    o, s = pl.pallas_call(
        _gla_sample_body,
        grid=(db,),
        in_specs=[
            pl.BlockSpec(rq.shape, full), pl.BlockSpec(f.shape, full), pl.BlockSpec(rv.shape, full),
            pl.BlockSpec((1, N_REC_HEADS, REC_DIM, REC_DIM), st),
        ],
        out_specs=[
            pl.BlockSpec((1, 1, 512), lambda b: (b, 0, 0)),
            pl.BlockSpec((1, N_REC_HEADS, REC_DIM, REC_DIM), st),
        ],
        out_shape=[
            jax.ShapeDtypeStruct((db, 1, 512), F32),
            jax.ShapeDtypeStruct(state.shape, F32),
        ],
        compiler_params=pltpu.CompilerParams(dimension_semantics=("arbitrary",)),
        name="gla_sample",
    )(rq, f, rv, state)
    return o.reshape(db, 512), s


def _sample_select_body(pt_ref, *refs, n_pages, top_k):
    P = PAGES_PER_STEP
    ki_refs = refs[0:P]
    qi_ref, w_ref, kin_ref, sel_ref, sc_ref = refs[P:]
    g = pl.program_id(1)
    n_groups = n_pages // P
    n_rows = sc_ref.shape[0]
    qi = qi_ref[0]
    w = w_ref[0]

    rows = []
    for i in range(P):
        d = jnp.dot(qi, ki_refs[i][...].astype(BF16), preferred_element_type=F32)
        rows.append(jnp.sum(w * jnp.maximum(d, 0.0), axis=0, keepdims=True))
    sc_ref[pl.ds(pl.multiple_of(g * P, P), P), :] = jnp.concatenate(rows, axis=0)

    @pl.when(g == n_groups - 1)
    def _():
        d_new = jnp.sum(qi.astype(F32) * kin_ref[0].astype(F32), axis=1, keepdims=True)
        s_new = jnp.sum(w * jnp.maximum(d_new, 0.0), axis=0, keepdims=True)
        tail = lax.broadcasted_iota(jnp.int32, (n_rows - n_pages, LANES), 0) * LANES + \
            lax.broadcasted_iota(jnp.int32, (n_rows - n_pages, LANES), 1)
        sc_ref[n_pages:n_rows, :] = jnp.where(tail == 0, s_new, NEG_SCORE)

        s = sc_ref[...]
        adm = s > NEG_HALF
        top = jnp.max(s, keepdims=True)

        def count_ge(x):
            return jnp.sum(jnp.where(sc_ref[...] >= x, 1.0, 0.0), keepdims=True)

        def bracket_minmax(lo, hi):
            v = sc_ref[...]
            inb = (v >= lo) & (v < hi)
            return (jnp.min(jnp.where(inb, v, BIG), keepdims=True),
                    jnp.max(jnp.where(inb, v, -BIG), keepdims=True))

        n_adm = jnp.full((1, 1), float(n_pages * LANES + 1), F32)
        thr, n_tie = _select_threshold(count_ge, bracket_minmax, jnp.min(jnp.where(adm, s, BIG), keepdims=True),
                                       top + (jnp.abs(top) + 1.0), n_adm, n_adm > top_k, float(top_k))
        eqb = jnp.where(s == thr, 1.0, 0.0).astype(BF16)
        ri = lax.broadcasted_iota(jnp.int32, (LANES, LANES), 0)
        ci = lax.broadcasted_iota(jnp.int32, (LANES, LANES), 1)
        in_row = jnp.dot(eqb, jnp.where(ri < ci, 1.0, 0.0).astype(BF16), preferred_element_type=F32)
        row_tot = jnp.dot(eqb, jnp.ones((LANES, LANES), BF16), preferred_element_type=F32)
        rr = lax.broadcasted_iota(jnp.int32, (n_rows, n_rows), 0)
        rc = lax.broadcasted_iota(jnp.int32, (n_rows, n_rows), 1)
        before = jnp.dot(jnp.where(rc < rr, 1.0, 0.0).astype(BF16), row_tot.astype(BF16),
                         preferred_element_type=F32)
        rank = in_row + before
        keep = (s > thr) | ((s == thr) & (rank < n_tie))
        sel_ref[0] = jnp.where(keep & adm, 1.0, 0.0)


def _sample_attend_body(pt_ref, *refs, n_pages):
    P = PAGES_PER_STEP
    k_refs, v_refs = refs[0:P], refs[P:2 * P]
    sel_ref, q_ref, kn_ref, vn_ref, o_ref, m_ref, l_ref, acc_ref = refs[2 * P:]
    g = pl.program_id(1)
    n_groups = n_pages // P

    @pl.when(g == 0)
    def _():
        q = q_ref[0].astype(F32)
        s_own = jnp.sum(q * kn_ref[0].astype(F32), axis=1, keepdims=True)
        own_sel = sel_ref[0, n_pages:n_pages + 1, 0:1]
        m_ref[...] = jnp.broadcast_to(s_own + jnp.where(own_sel > 0.5, 0.0, NEG_BIAS), m_ref.shape)
        l_ref[...] = jnp.ones(l_ref.shape, F32)
        acc_ref[...] = jnp.broadcast_to(vn_ref[0].astype(F32), acc_ref.shape)

    q = q_ref[0]
    sel = sel_ref[0, pl.ds(pl.multiple_of(g * P, P), P), :]
    scores = []
    for i in range(P):
        s = jnp.dot(q, k_refs[i][...].astype(BF16), preferred_element_type=F32)
        scores.append(jnp.where(sel[i:i + 1, :] > 0.5, s, NEG_BIAS))
    s_all = jnp.concatenate(scores, axis=1)
    m_old = m_ref[...]
    m_new = jnp.maximum(m_old, jnp.max(s_all, axis=1, keepdims=True))
    alpha = jnp.exp2(m_old - m_new)
    p32 = jnp.exp2(s_all - m_new[:, 0:1])
    l_ref[...] = alpha * l_ref[...] + jnp.sum(p32, axis=1, keepdims=True)
    m_ref[...] = m_new
    p = p32.astype(BF16)
    pv = jnp.zeros(acc_ref.shape, F32)
    for i in range(P):
        pv = pv + lax.dot_general(p[:, i * LANES:(i + 1) * LANES], v_refs[i][...].astype(BF16), _NT,
                                  preferred_element_type=F32)
    acc_ref[...] = alpha[:, 0:1] * acc_ref[...] + pv

    @pl.when(g == n_groups - 1)
    def _():
        o_ref[0] = acc_ref[...] / l_ref[:, 0:1]


def _attn_sample(page_table, layer, cache_kit, cache_kt, cache_vt, qi, w, ki_new, q, k_new, v_new):
    db, n_pages = page_table.shape
    P = PAGES_PER_STEP
    n_groups = n_pages // P
    page = cache_kit.shape[3]
    kv_w = N_KV_HEADS * HEAD_DIM
    top_k = min(TOPK_MAX, (n_pages * page + 1) // 4)
    n_rows = -(-(n_pages + 1) // SUBLANES) * SUBLANES
    pt = page_table.reshape(-1)
    params = pltpu.CompilerParams(dimension_semantics=("arbitrary", "arbitrary"), vmem_limit_bytes=VMEM_LIMIT)

    def page_map(i):
        return lambda b, g, pt: (layer, pt[b * n_pages + g * P + i], 0, 0)

    per_b = lambda b, g, pt: (b, 0, 0)
    sel = pl.pallas_call(
        functools.partial(_sample_select_body, n_pages=n_pages, top_k=top_k),
        grid_spec=pltpu.PrefetchScalarGridSpec(
            num_scalar_prefetch=1,
            grid=(db, n_groups),
            in_specs=([pl.BlockSpec((None, None, IDX_DIM, page), page_map(i)) for i in range(P)]
                      + [pl.BlockSpec((1,) + a.shape[1:], per_b) for a in (qi, w, ki_new)]),
            out_specs=pl.BlockSpec((1, n_rows, LANES), per_b),
            scratch_shapes=[pltpu.VMEM((n_rows, LANES), F32)]),
        out_shape=jax.ShapeDtypeStruct((db, n_rows, LANES), F32),
        compiler_params=params,
        name="sample_select",
    )(pt, *([cache_kit] * P), qi, w, ki_new)

    out = pl.pallas_call(
        functools.partial(_sample_attend_body, n_pages=n_pages),
        grid_spec=pltpu.PrefetchScalarGridSpec(
            num_scalar_prefetch=1,
            grid=(db, n_groups),
            in_specs=([pl.BlockSpec((None, None, kv_w, page), page_map(i)) for i in range(P)]
                      + [pl.BlockSpec((None, None, kv_w, page), page_map(i)) for i in range(P)]
                      + [pl.BlockSpec((1,) + a.shape[1:], per_b) for a in (sel, q, k_new, v_new)]),
            out_specs=pl.BlockSpec((1, N_HEADS, kv_w), per_b),
            scratch_shapes=[
                pltpu.VMEM((N_HEADS, LANES), F32),
                pltpu.VMEM((N_HEADS, LANES), F32),
                pltpu.VMEM((N_HEADS, kv_w), F32),
            ]),
        out_shape=jax.ShapeDtypeStruct((db, N_HEADS, kv_w), F32),
        compiler_params=params,
        name="sample_attend",
    )(pt, *([cache_kt] * P), *([cache_vt] * P), sel, q, k_new, v_new)
    return jnp.concatenate([out[:, h, (h // 2) * HEAD_DIM:(h // 2 + 1) * HEAD_DIM] for h in range(N_HEADS)], axis=1)


def _back_body(x_ref, att_ref, atts_ref, ro_ref, ros_ref, gr_ref, ga_ref, gb_ref, grec_ref,
               wpa_ref, wpb_ref, wo_ref, gffn_ref, wgu_ref, wd_ref, y_ref, *, n_prompt_tiles):
    is_sample = pl.program_id(0) >= n_prompt_tiles
    ro = jnp.where(is_sample, ros_ref[...], ro_ref[...])
    att = jnp.where(is_sample, atts_ref[...], att_ref[...])
    gr = gr_ref[...]
    recs = []
    for h in range(N_REC_HEADS):
        r = ro[:, h * LANES:(h + 1) * LANES]
        r = r * lax.rsqrt(jnp.mean(r * r, axis=-1, keepdims=True) + EPS) * grec_ref[...]
        recs.append((r * gr[:, h * LANES:(h + 1) * LANES]).astype(BF16))
    rec = jnp.concatenate(recs, axis=1)
    a = jnp.dot(att, wpa_ref[...], preferred_element_type=F32)
    b = jnp.dot(rec, wpb_ref[...], preferred_element_type=F32)
    mix = (ga_ref[...] * a + gb_ref[...] * b).astype(BF16)
    y = x_ref[...] + jnp.dot(mix, wo_ref[...], preferred_element_type=F32)

    d_ff = wd_ref.shape[0]
    hb = (y * lax.rsqrt(jnp.mean(y * y, axis=-1, keepdims=True) + EPS) * gffn_ref[...]).astype(BF16)
    for c in range(d_ff // FFN_CHUNK):
        c0 = c * FFN_CHUNK
        gate = jnp.dot(hb, wgu_ref[:, c0:c0 + FFN_CHUNK], preferred_element_type=F32)
        up = jnp.dot(hb, wgu_ref[:, d_ff + c0:d_ff + c0 + FFN_CHUNK], preferred_element_type=F32)
        act = (gate * _sigmoid(gate) * up).astype(BF16)
        y = y + jnp.dot(act, wd_ref[c0:c0 + FFN_CHUNK, :], preferred_element_type=F32)
    y_ref[...] = y


def _back(x_all, att_p, att_s, ro_p, ro_s, gr, ga, gb, grec, wpa, wpb, wo, gffn, wgu, wd):
    n_rows, d = x_all.shape
    once = dict(pipeline_mode=pl.Buffered(1))
    n_prompt_tiles = att_p.shape[0] // TOK_TILE
    row = lambda i: (i, 0)
    prow = lambda i: (jnp.minimum(i, n_prompt_tiles - 1), 0)
    const = lambda i: (0, 0)
    return pl.pallas_call(
        functools.partial(_back_body, n_prompt_tiles=n_prompt_tiles),
        grid=(n_rows // TOK_TILE,),
        in_specs=[
            pl.BlockSpec((TOK_TILE, d), row),
            pl.BlockSpec((TOK_TILE, 512), prow), pl.BlockSpec((TOK_TILE, 512), const),
            pl.BlockSpec((TOK_TILE, 512), prow), pl.BlockSpec((TOK_TILE, 512), const),
            pl.BlockSpec((TOK_TILE, 512), row),
            pl.BlockSpec((TOK_TILE, d), row), pl.BlockSpec((TOK_TILE, d), row),
            pl.BlockSpec((1, LANES), const),
            pl.BlockSpec(wpa.shape, const, **once), pl.BlockSpec(wpb.shape, const, **once),
            pl.BlockSpec(wo.shape, const, **once),
            pl.BlockSpec((1, d), const),
            pl.BlockSpec(wgu.shape, const, **once), pl.BlockSpec(wd.shape, const, **once),
        ],
        out_specs=pl.BlockSpec((TOK_TILE, d), row),
        out_shape=jax.ShapeDtypeStruct((n_rows, d), F32),
        compiler_params=pltpu.CompilerParams(
            dimension_semantics=("arbitrary",), vmem_limit_bytes=VMEM_LIMIT),
        name="mixer_back_ffn",
    )(x_all, att_p, att_s, ro_p, ro_s, gr, ga, gb, grec, wpa, wpb, wo, gffn, wgu, wd)


def _rope_table(pos):
    half = ROT_DIM // 2
    inv = jnp.power(ROPE_THETA, -2.0 * jnp.arange(half, dtype=F32) / ROT_DIM)
    ang = pos.astype(F32)[:, None] * inv[None, :]
    cos, sin = jnp.cos(ang), jnp.sin(ang)
    n = pos.shape[0]
    one = jnp.ones((n, HEAD_DIM - ROT_DIM), F32)
    zero8 = jnp.zeros((n, half), F32)
    zero = jnp.zeros((n, HEAD_DIM - ROT_DIM), F32)
    c64 = jnp.concatenate([cos, cos, one], axis=1)
    up64 = jnp.concatenate([zero8, sin, zero], axis=1)
    dn64 = jnp.concatenate([-sin, zero8, zero], axis=1)
    return jnp.concatenate([c64, c64, up64, up64, dn64, dn64], axis=1)


def _pack_w_in(w):
    d = w.shape[0]
    cuts = (512, 768, 1024, 1536, 1600, 1608, 2120, 2632, 3144, 3656, 4680)
    q, k, v, qi, ki, wi, qr, fr, ir, gr, ga, gb = jnp.split(w, cuts, axis=1)
    pad = jnp.zeros((d, LANES - IDX_DIM - N_IDX_HEADS), w.dtype)
    return jnp.concatenate([q, k, v, qi, ki, wi, pad, qr, fr, ir, gr, ga, gb], axis=1).astype(BF16)


def kernel(x_prompt, x_sample, cache_k, cache_v, cache_idx_k, state_rec, page_table, meta_tokens,
           w_in, norm_mix, q_norm, k_norm, lb_raw, rec_norm, w_pa, w_pb, w_o, norm_ffn, w_gu, w_down):
    batch, seq, d = x_prompt.shape
    db = x_sample.shape[0]
    depth = w_in.shape[0]
    n_pages = page_table.shape[1]
    page = cache_k.shape[2]
    past_len = n_pages * page
    t_real = seq + N_META
    t_pad = -(-t_real // TOK_TILE) * TOK_TILE
    tiles_per_seq = t_pad // TOK_TILE
    n_prompt = batch * t_pad
    top_k = min(TOPK_MAX, seq // 4)

    sm = jax.nn.softmax(lb_raw.astype(F32), axis=0)
    lower_bounds = jnp.cumsum(sm, axis=0) - sm[0:1]

    seq_pad = jnp.zeros((t_pad - t_real, d), F32)
    pieces = []
    for b in range(batch):
        pieces += [meta_tokens.astype(F32), x_prompt[b], seq_pad]
    pieces += [x_sample.reshape(db, d), jnp.zeros((TOK_TILE - db, d), F32)]
    x_all = jnp.concatenate(pieces, axis=0)

    rope_tab = jnp.concatenate([
        _rope_table(jnp.arange(t_pad, dtype=jnp.int32)),
        _rope_table(jnp.full((TOK_TILE,), past_len, jnp.int32))], axis=0)

    n_pool = cache_k.shape[1]
    cache_kt = jnp.transpose(cache_k, (0, 1, 3, 4, 2)).reshape(depth, n_pool, N_KV_HEADS * HEAD_DIM, page)
    cache_vt = jnp.transpose(cache_v, (0, 1, 3, 4, 2)).reshape(depth, n_pool, N_KV_HEADS * HEAD_DIM, page)
    cache_kit = jnp.transpose(cache_idx_k, (0, 1, 3, 2))
    srows = slice(n_prompt, n_prompt + db)
    tile2 = lambda g_: jnp.concatenate([g_, g_]).reshape(1, LANES).astype(F32)

    pk, pv, pki, ps, sk, sv, ski, ss = [], [], [], [], [], [], [], []
    for l in range(depth):
        pj = _inproj(x_all, norm_mix[l].reshape(1, d).astype(F32), _pack_w_in(w_in[l]), rope_tab,
                     tile2(q_norm[l]), tile2(k_norm[l]), lower_bounds[l].reshape(1, -1),
                     tiles_per_seq, n_prompt // TOK_TILE)

        att_p = _attn_prompt(pj, batch, t_pad, t_real, top_k)
        ro_p, st_p = _gla_prompt(pj, batch, t_pad, t_real)

        q_s = pj["q"][srows].reshape(db, N_HEADS, LANES)
        zero_s = jnp.zeros_like(q_s)
        q_s = jnp.stack([jnp.concatenate([q_s[:, h] if h // 4 == s else zero_s[:, h] for s in range(2)], axis=1)
                         for h in range(N_HEADS)], axis=1)
        qi_s = pj["qi"][srows].reshape(db, N_IDX_HEADS, IDX_DIM)
        w_s = pj["kiwi"][srows, IDX_DIM:IDX_DIM + N_IDX_HEADS].reshape(db, N_IDX_HEADS, 1)
        ki_s = pj["kk"][srows, 0:IDX_DIM].reshape(db, 1, IDX_DIM)
        kn_s = pj["kb"][srows].reshape(db, 1, N_KV_HEADS * HEAD_DIM)
        vn_s = pj["vb"][srows].reshape(db, 1, N_KV_HEADS * HEAD_DIM)
        att_s = _attn_sample(page_table, l, cache_kit, cache_kt, cache_vt, qi_s, w_s, ki_s, q_s, kn_s, vn_s)
        ro_s, st_s = _gla_sample(pj["rq"][srows], pj["f"][srows], pj["rv"][srows], state_rec[l])

        pad_s = lambda a: jnp.pad(a, ((0, TOK_TILE - db), (0, 0)))
        x_all = _back(x_all, att_p, pad_s(att_s.astype(BF16)), ro_p, pad_s(ro_s),
                      pj["gr"], pj["ga"], pj["gb"], rec_norm[l].reshape(1, LANES).astype(F32),
                      w_pa[l].astype(BF16), w_pb[l].astype(BF16), w_o[l].astype(BF16),
                      norm_ffn[l].reshape(1, d).astype(F32), w_gu[l].astype(BF16), w_down[l].astype(BF16))

        seq_view = lambda a, wd: a[:n_prompt].reshape(batch, t_pad, wd)[:, :t_real]
        pk.append(seq_view(pj["kf"], 256).reshape(batch, t_real, N_KV_HEADS, HEAD_DIM))
        pv.append(seq_view(pj["vf"], 256).reshape(batch, t_real, N_KV_HEADS, HEAD_DIM))
        pki.append(seq_view(pj["kiwi"], LANES)[..., :IDX_DIM])
        ps.append(st_p)
        sk.append(pj["kf"][srows].reshape(db, 1, N_KV_HEADS, HEAD_DIM))
        sv.append(pj["vf"][srows].reshape(db, 1, N_KV_HEADS, HEAD_DIM))
        ski.append(pj["kiwi"][srows, :IDX_DIM].reshape(db, 1, IDX_DIM))
        ss.append(st_s)

    y_prompt = jnp.stack([x_all[b * t_pad + N_META:b * t_pad + t_real] for b in range(batch)])
    y_sample = x_all[srows].reshape(db, 1, d)
    return (y_prompt, y_sample, jnp.stack(pk), jnp.stack(pv), jnp.stack(pki), jnp.stack(ps),
            jnp.stack(sk), jnp.stack(sv), jnp.stack(ski), jnp.stack(ss))
```

```python
import functools

import jax
import jax.numpy as jnp
from jax import lax
from jax.experimental import pallas as pl
from jax.experimental.pallas import tpu as pltpu

F32 = jnp.float32
BF16 = jnp.bfloat16

N_META = 16
N_HEADS = 8
HEAD_DIM = 64
N_KV_HEADS = 4
ROT_DIM = 16
ROPE_THETA = 500000.0
N_IDX_HEADS = 8
IDX_DIM = 64
IDX_W_SCALE = (N_IDX_HEADS * IDX_DIM) ** -0.5
TOPK_MAX = 256
N_REC_HEADS = 4
REC_DIM = 128
EPS = 1e-6
Q_SCALE = HEAD_DIM ** -0.5 * 1.4426950408889634

LANES = 128
SUBLANES = 8

TOK_TILE = 384
Q_TILE = 384
KEY_CHUNK = 128
KEY_GROUP = 3
SUM_ROWS = 16
REC_CHUNK = 64
REC_CHUNKS_PER_STEP = 3
MATMUL_MID_LEVELS = 3
PAGES_PER_STEP = 32
FFN_CHUNK = 256
VMEM_LIMIT = 56 * 1024 * 1024

NEG_SCORE = -3.0e38
NEG_HALF = -1.5e38
NEG_BIAS = -1.0e30
BIG = 3.0e38
BISECTIONS_PER_ROUND = 20
MAX_ROUNDS = 160

C_Q, C_K, C_V, C_QI, C_KIWI, C_RQ, C_FR, C_IR, C_GR, C_GA, C_GB, C_END = (
    0, 512, 768, 1024, 1536, 1664, 2176, 2688, 3200, 3712, 4736, 5760)

_NT = (((1,), (1,)), ((), ()))
_TN = (((0,), (0,)), ((), ()))


def _sigmoid(x):
    return 1.0 / (1.0 + jnp.exp(-x))


def _split3(x):
    hi = x.astype(BF16)
    r1 = x - hi.astype(F32)
    mid = r1.astype(BF16)
    lo = (r1 - mid.astype(F32)).astype(BF16)
    return jnp.concatenate([hi, mid, lo], axis=1)


def _join3(r):
    return r[:, 0:LANES] + r[:, LANES:2 * LANES] + r[:, 2 * LANES:3 * LANES]


def _inproj_body(x_ref, g_ref, w_ref, rope_ref, qn_ref, kn_ref, lb_ref,
                 q_ref, kf_ref, vf_ref, kb_ref, vb_ref, qi_ref, kiwi_ref, kk_ref,
                 rq_ref, f_ref, rv_ref, gr_ref, ga_ref, gb_ref):
    x = x_ref[...]
    h = x * lax.rsqrt(jnp.mean(x * x, axis=-1, keepdims=True) + EPS) * g_ref[...]
    hb = h.astype(BF16)
    rows = x.shape[0]
    lane = lax.broadcasted_iota(jnp.int32, (rows, LANES), 1)
    low = lane < HEAD_DIM
    cosv = rope_ref[:, 0:LANES]
    sin_up = rope_ref[:, LANES:2 * LANES]
    sin_dn = rope_ref[:, 2 * LANES:3 * LANES]

    def proj(c0, c1):
        return jnp.dot(hb, w_ref[:, c0:c1], preferred_element_type=F32)

    def rope(xs):
        return (xs * cosv + pltpu.roll(xs, ROT_DIM // 2, 1) * sin_up
                + pltpu.roll(xs, LANES - ROT_DIM // 2, 1) * sin_dn)

    def headnorm(xs, gain):
        sq = xs * xs
        s_lo = jnp.sum(jnp.where(low, sq, 0.0), axis=-1, keepdims=True)
        s_hi = jnp.sum(jnp.where(low, 0.0, sq), axis=-1, keepdims=True)
        ms = jnp.where(low, s_lo, s_hi) * (1.0 / HEAD_DIM)
        return xs * lax.rsqrt(ms + EPS) * gain

    qraw = proj(C_Q, C_K)
    for p in range(N_KV_HEADS):
        qs = rope(headnorm(qraw[:, p * LANES:(p + 1) * LANES], qn_ref[...])) * Q_SCALE
        qr = pltpu.roll(qs, HEAD_DIM, 1)
        if p % 2 == 0:
            h0, h1 = jnp.where(low, qs, 0.0), jnp.where(low, qr, 0.0)
        else:
            h0, h1 = jnp.where(low, 0.0, qr), jnp.where(low, 0.0, qs)
        q_ref[:, (2 * p) * LANES:(2 * p + 1) * LANES] = h0.astype(BF16)
        q_ref[:, (2 * p + 1) * LANES:(2 * p + 2) * LANES] = h1.astype(BF16)

    kraw = proj(C_K, C_V)
    for s in range(2):
        ks = rope(headnorm(kraw[:, s * LANES:(s + 1) * LANES], kn_ref[...]))
        kf_ref[:, s * LANES:(s + 1) * LANES] = ks
        kb_ref[:, s * LANES:(s + 1) * LANES] = ks.astype(BF16)

    vraw = proj(C_V, C_QI)
    vf_ref[...] = vraw
    vb_ref[...] = vraw.astype(BF16)

    qiraw = proj(C_QI, C_KIWI)
    for p in range(4):
        qi_ref[:, p * LANES:(p + 1) * LANES] = rope(qiraw[:, p * LANES:(p + 1) * LANES]).astype(BF16)

    kiwi = proj(C_KIWI, C_RQ)
    kir = rope(kiwi)
    kiwi_ref[...] = jnp.where(low, kir, kiwi * IDX_W_SCALE)
    kk_ref[...] = jnp.where(low, kir, pltpu.roll(kir, HEAD_DIM, 1)).astype(BF16)

    qr_ = proj(C_RQ, C_FR)
    rq_ref[...] = qr_ * _sigmoid(qr_)
    lb = lb_ref[...]
    f_ref[...] = lb + (1.0 - lb) * _sigmoid(proj(C_FR, C_IR))
    rv_ref[...] = proj(C_IR, C_GR)
    g_ = proj(C_GR, C_GA)
    gr_ref[...] = g_ * _sigmoid(g_)
    ga_ref[...] = _sigmoid(proj(C_GA, C_GB))
    gb_ref[...] = _sigmoid(proj(C_GB, C_END))


def _inproj(x_all, g, w, rope_tab, qn, kn, lb, tiles_per_seq, n_prompt_tiles):
    n_rows, d = x_all.shape
    n_tiles = n_rows // TOK_TILE
    row = lambda i: (i, 0)
    const = lambda i: (0, 0)
    rope_map = lambda i: (jnp.where(i < n_prompt_tiles, i % tiles_per_seq, tiles_per_seq), 0)
    widths = dict(q=(1024, BF16), kf=(256, F32), vf=(256, F32), kb=(256, BF16), vb=(256, BF16),
                  qi=(512, BF16), kiwi=(128, F32), kk=(128, BF16), rq=(512, F32), f=(512, F32),
                  rv=(512, F32), gr=(512, F32), ga=(1024, F32), gb=(1024, F32))
    out_shape = [jax.ShapeDtypeStruct((n_rows, wd), dt) for wd, dt in widths.values()]
    out_specs = [pl.BlockSpec((TOK_TILE, wd), row) for wd, _ in widths.values()]
    outs = pl.pallas_call(
        _inproj_body,
        grid=(n_tiles,),
        in_specs=[
            pl.BlockSpec((TOK_TILE, d), row),
            pl.BlockSpec((1, d), const),
            pl.BlockSpec((d, C_END), const, pipeline_mode=pl.Buffered(1)),
            pl.BlockSpec((TOK_TILE, 3 * LANES), rope_map),
            pl.BlockSpec((1, LANES), const),
            pl.BlockSpec((1, LANES), const),
            pl.BlockSpec((1, 512), const),
        ],
        out_specs=out_specs,
        out_shape=out_shape,
        compiler_params=pltpu.CompilerParams(
            dimension_semantics=("arbitrary",), vmem_limit_bytes=VMEM_LIMIT),
        name="inproj",
    )(x_all, g, w, rope_tab, qn, kn, lb)
    return dict(zip(widths.keys(), outs))


def _select_threshold(count_ge, bracket_minmax, lo0, hi0, n_adm, need, top_k):
    def halve(st):
        lo, hi, c_lo, c_hi, _ = st
        mid = 0.5 * (lo + hi)
        c_mid = count_ge(mid)
        ge = c_mid >= top_k
        lo2, c_lo2 = jnp.where(ge, mid, lo), jnp.where(ge, c_mid, c_lo)
        hi2, c_hi2 = jnp.where(ge, hi, mid), jnp.where(ge, c_hi, c_mid)
        r = top_k - c_hi2
        stop = (c_lo2 - c_hi2 == r) | (r == 1.0) | (mid <= lo) | (mid >= hi) | jnp.logical_not(need)
        return lo2, hi2, c_lo2, c_hi2, jnp.where(stop, 1.0, 0.0)

    def inner_cond(st):
        return (jnp.min(st[4]) < 0.5) & (st[5] < BISECTIONS_PER_ROUND)

    def inner_body(st):
        return halve(halve(st[:5])) + (st[5] + 2,)

    def outer_cond(st):
        return (jnp.min(st[4]) < 0.5) & (st[7] < MAX_ROUNDS)

    def outer_body(st):
        lo, hi, c_lo, c_hi, done = lax.while_loop(inner_cond, inner_body, st[:5] + (jnp.int32(0),))[:5]
        mn, mx = bracket_minmax(lo, hi)
        open_ = need & (done < 0.5)
        done = jnp.where(open_ & (mn == mx), 1.0, done)
        return jnp.where(open_, mn, lo), hi, c_lo, c_hi, done, mn, mx, st[7] + 1

    zero = jnp.zeros_like(lo0)
    lo, hi, c_lo, c_hi, _, mn, mx, _ = lax.while_loop(
        outer_cond, outer_body, (lo0, hi0, n_adm, zero, zero, zero, zero, jnp.int32(0)))
    r = top_k - c_hi
    thr = jnp.where(need, jnp.where(r == 1.0, mx, mn), NEG_SCORE)
    n_tie = jnp.where(need, jnp.where(c_lo - c_hi != r, r, BIG), 0.0)
    return thr, n_tie


def _attn_body(q_ref, qi_ref, wq_ref, kb_ref, vb_ref, kk_ref, o_ref,
               sc_ref, qia_ref, qa_ref, vt_ref, m_ref, acc_ref,
               da_ref, db_ref, sa_ref, sb_ref, *, top_k, t_real):
    j = pl.program_id(1)
    T = KEY_CHUNK
    Q = Q_TILE
    nchunks = (j + 1) * (Q // T)
    low = lax.broadcasted_iota(jnp.int32, (Q, LANES), 1) < HEAD_DIM

    @pl.when(j == 0)
    def _():
        def transpose_v(c, carry):
            blk = vb_ref[pl.ds(pl.multiple_of(c * T, T), T), :].astype(F32)
            for s in range(2):
                vt_ref[c, s, 0:LANES, :] = blk[:, s * LANES:(s + 1) * LANES].T.astype(BF16)
                vt_ref[c, s, LANES:LANES + SUM_ROWS, :] = jnp.ones((SUM_ROWS, T), BF16)
            return carry
        lax.fori_loop(0, vt_ref.shape[0], transpose_v, 0)

    wt = wq_ref[...].T
    qi = qi_ref[...]
    for h in range(N_IDX_HEADS):
        slab = qi[:, (h // 2) * LANES:(h // 2 + 1) * LANES]
        zero = jnp.zeros_like(slab)
        qia_ref[h * Q:(h + 1) * Q, :] = jnp.where(low, slab, zero) if h % 2 == 0 else jnp.where(low, zero, slab)
        qa_ref[h * Q:(h + 1) * Q, :] = q_ref[:, h * LANES:(h + 1) * LANES]

    G = KEY_GROUP
    ngroups = (nchunks + G - 1) // G
    last_chunk = vt_ref.shape[0] - 1

    def chunk_rows(ref, g, u, lanes):
        cc = jnp.minimum(g * G + u, last_chunk)
        return ref[pl.ds(pl.multiple_of(cc * T, T), T), lanes]

    def fold(x, op):
        return op(x.reshape(x.shape[0] // SUBLANES, SUBLANES, Q), axis=0)

    npairs = ngroups // 2
    odd_tail = ngroups % 2 == 1

    def idx_dots(dst, g):
        kk = jnp.concatenate([chunk_rows(kk_ref, g, u, slice(None)) for u in range(G)], axis=0)
        dst[...] = lax.dot_general(kk, qia_ref[...], _NT, preferred_element_type=F32)

    def idx_scores(src, g, carry):
        mn, mx = carry
        s = jnp.zeros((G * T, Q), F32)
        for h in range(N_IDX_HEADS):
            s = s + wt[IDX_DIM + h:IDX_DIM + h + 1, :] * jnp.maximum(src[:, h * Q:(h + 1) * Q], 0.0)
        kpos = g * (G * T) + lax.broadcasted_iota(jnp.int32, (G * T, Q), 0)
        adm = kpos <= j * Q + lax.broadcasted_iota(jnp.int32, (G * T, Q), 1)
        sc_ref[pl.ds(pl.multiple_of(g * (G * T), G * T), G * T), :] = jnp.where(adm, s, NEG_SCORE)
        return (jnp.minimum(mn, fold(jnp.where(adm, s, BIG), jnp.min)),
                jnp.maximum(mx, fold(jnp.where(adm, s, -BIG), jnp.max)))

    def phase_a(i, carry):
        idx_dots(db_ref, 2 * i + 1)
        carry = idx_scores(da_ref, 2 * i, carry)
        idx_dots(da_ref, 2 * i + 2)
        return idx_scores(db_ref, 2 * i + 1, carry)

    idx_dots(da_ref, 0)
    rng = lax.fori_loop(0, npairs, phase_a,
                        (jnp.full((SUBLANES, Q), BIG, F32), jnp.full((SUBLANES, Q), -BIG, F32)))
    mn, mx = lax.cond(odd_tail, lambda c: idx_scores(da_ref, ngroups - 1, c), lambda c: c, rng)

    def sc_group(g):
        return sc_ref[pl.ds(pl.multiple_of(g * (G * T), G * T), G * T), :]

    def count_ge(x):
        def step(g, acc):
            return acc + fold(jnp.where(sc_group(g) >= x, 1.0, 0.0), jnp.sum)
        return jnp.sum(lax.fori_loop(0, ngroups, step, jnp.zeros((SUBLANES, Q), F32)), axis=0, keepdims=True)

    def bracket_minmax(lo, hi):
        def step(g, carry):
            a, b = carry
            s = sc_group(g)
            inb = (s >= lo) & (s < hi)
            return (jnp.minimum(a, fold(jnp.where(inb, s, BIG), jnp.min)),
                    jnp.maximum(b, fold(jnp.where(inb, s, -BIG), jnp.max)))
        a, b = lax.fori_loop(0, ngroups, step,
                             (jnp.full((SUBLANES, Q), BIG, F32), jnp.full((SUBLANES, Q), -BIG, F32)))
        return jnp.min(a, axis=0, keepdims=True), jnp.max(b, axis=0, keepdims=True)

    qrow = j * Q + lax.broadcasted_iota(jnp.int32, (1, Q), 1)
    n_adm = (qrow + 1).astype(F32)
    need = (qrow + 1 > top_k) & (qrow < t_real)
    top = jnp.max(mx, axis=0, keepdims=True)
    thr, n_tie = _select_threshold(count_ge, bracket_minmax, jnp.min(mn, axis=0, keepdims=True),
                                   top + (jnp.abs(top) + 1.0), n_adm, need, float(top_k))

    m_ref[...] = jnp.full(m_ref.shape, NEG_SCORE, F32)
    acc_ref[...] = jnp.zeros(acc_ref.shape, F32)
    earlier = jnp.where(lax.broadcasted_iota(jnp.int32, (T, T), 1) < lax.broadcasted_iota(jnp.int32, (T, T), 0),
                        1.0, 0.0).astype(BF16)

    def qk_dots(dst, g):
        for sl in range(2):
            lanes = slice(sl * LANES, (sl + 1) * LANES)
            kc = jnp.concatenate([chunk_rows(kb_ref, g, u, lanes) for u in range(G)], axis=0)
            dst[sl] = lax.dot_general(kc, qa_ref[sl * 4 * Q:(sl + 1) * 4 * Q, :], _NT,
                                      preferred_element_type=F32)

    def softmax_pv(src, g, seen, issue_next):
        s = sc_group(g)
        eq = s == thr
        eqf = jnp.where(eq, 1.0, 0.0)
        eq_l = jnp.concatenate([eqf[u * T:(u + 1) * T] for u in range(G)], axis=1).astype(BF16)
        in_chunk = jnp.dot(earlier, eq_l, preferred_element_type=F32)
        ranks = []
        for u in range(G):
            ranks.append(in_chunk[:, u * Q:(u + 1) * Q] + seen)
            seen = seen + jnp.sum(eqf[u * T:(u + 1) * T], axis=0, keepdims=True)
        rank = jnp.concatenate(ranks, axis=0)
        bias = jnp.where(s > thr, 0.0, jnp.where(eq & (rank < n_tie), 0.0, NEG_BIAS))
        bias4 = jnp.concatenate([bias] * 4, axis=1)
        sts, m_news, alphas = [], [], []
        for sl in range(2):
            st = src[sl] + bias4
            m_old = m_ref[sl]
            m_new = jnp.maximum(m_old, jnp.max(st, axis=0, keepdims=True))
            m_ref[sl] = m_new
            sts.append(st)
            m_news.append(m_new)
            alphas.append(jnp.exp2(m_old - m_new))
        issue_next()
        for sl in range(2):
            p = jnp.exp2(sts[sl] - m_news[sl])
            vt =jnp.concatenate([vt_ref[jnp.minimum(g * G + u, last_chunk), sl] for u in range(G)], axis=1)
            acc_ref[sl] = alphas[sl] * acc_ref[sl] + jnp.dot(vt, p.astype(BF16), preferred_element_type=F32)
        return seen

    def phase_c(i, seen):
        seen = softmax_pv(sa_ref, 2 * i, seen, lambda: qk_dots(sb_ref, 2 * i + 1))
        return softmax_pv(sb_ref, 2 * i + 1, seen, lambda: qk_dots(sa_ref, 2 * i + 2))

    qk_dots(sa_ref, 0)
    seen = lax.fori_loop(0, npairs, phase_c, jnp.zeros((1, Q), F32))

    @pl.when(odd_tail)
    def _():
        softmax_pv(sa_ref, ngroups - 1, seen, lambda: None)

    for p in range(N_KV_HEADS):
        sl, hh = (2 * p) // 4, (2 * p) % 4
        out = acc_ref[sl, 0:LANES, :] / acc_ref[sl, LANES:LANES + 1, :]
        a0 = out[:, hh * Q:(hh + 1) * Q].T
        a1 = out[:, (hh + 1) * Q:(hh + 2) * Q].T
        if p % 2 == 0:
            slab = jnp.where(low, a0, pltpu.roll(a1, HEAD_DIM, 1))
        else:
            slab = jnp.where(low, pltpu.roll(a0, HEAD_DIM, 1), a1)
        o_ref[:, p * LANES:(p + 1) * LANES] = slab.astype(BF16)


def _attn_prompt(pj, batch, t_pad, t_real, top_k):
    nq = t_pad // Q_TILE
    nkc = t_pad // KEY_CHUNK
    grp_rows = KEY_GROUP * KEY_CHUNK
    n_grp = 2 * -(-(-(-nkc // KEY_GROUP)) // 2)
    qrow = lambda b, j: (b * nq + j, 0)
    seq = lambda b, j: (b, 0)
    return pl.pallas_call(
        functools.partial(_attn_body, top_k=top_k, t_real=t_real),
        grid=(batch, nq),
        in_specs=[
            pl.BlockSpec((Q_TILE, 1024), qrow),
            pl.BlockSpec((Q_TILE, 512), qrow),
            pl.BlockSpec((Q_TILE, LANES), qrow),
            pl.BlockSpec((t_pad, 256), seq, pipeline_mode=pl.Buffered(1)),
            pl.BlockSpec((t_pad, 256), seq, pipeline_mode=pl.Buffered(1)),
            pl.BlockSpec((t_pad, LANES), seq, pipeline_mode=pl.Buffered(1)),
        ],
        out_specs=pl.BlockSpec((Q_TILE, 512), qrow),
        out_shape=jax.ShapeDtypeStruct((batch * t_pad, 512), BF16),
        scratch_shapes=[
            pltpu.VMEM((n_grp * grp_rows, Q_TILE), F32),
            pltpu.VMEM((N_IDX_HEADS * Q_TILE, LANES), BF16),
            pltpu.VMEM((N_HEADS * Q_TILE, LANES), BF16),
            pltpu.VMEM((nkc, 2, LANES + SUM_ROWS, KEY_CHUNK), BF16),
            pltpu.VMEM((2, 1, 4 * Q_TILE), F32),
            pltpu.VMEM((2, LANES + SUM_ROWS, 4 * Q_TILE), F32),
            pltpu.VMEM((grp_rows, N_IDX_HEADS * Q_TILE), F32),
            pltpu.VMEM((grp_rows, N_IDX_HEADS * Q_TILE), F32),
            pltpu.VMEM((2, grp_rows, 4 * Q_TILE), F32),
            pltpu.VMEM((2, grp_rows, 4 * Q_TILE), F32),
        ],
        compiler_params=pltpu.CompilerParams(
            dimension_semantics=("arbitrary", "arbitrary"), vmem_limit_bytes=VMEM_LIMIT),
        name="attn_prompt",
    )(pj["q"], pj["qi"], pj["kiwi"], pj["kb"], pj["vb"], pj["kk"])


def _gla_body(rq_ref, f_ref, rv_ref, cum_ref, o_ref, sout_ref, st_ref, *, t_real):
    step = pl.program_id(1)
    n_steps = pl.num_programs(1)
    C = REC_CHUNK
    CH = o_ref.shape[0] // C
    n_lev = C.bit_length() - 1

    @pl.when(step == 0)
    def _():
        st_ref[...] = jnp.zeros(st_ref.shape, F32)

    trow = lax.broadcasted_iota(jnp.int32, (C, LANES), 0)
    ti = lax.broadcasted_iota(jnp.int32, (C, C), 0)
    si = lax.broadcasted_iota(jnp.int32, (C, C), 1)
    H = range(N_REC_HEADS)
    units = [(cc, h) for cc in range(CH) for h in H]
    valid = [(step * CH + cc) * C + trow < t_real for cc in range(CH)]

    def tile(ref, u):
        return ref[u[0] * C:(u[0] + 1) * C, u[1] * LANES:(u[1] + 1) * LANES]

    f = {u: tile(f_ref, u) for u in units}
    q = {u: tile(rq_ref, u) for u in units}
    vb = {u: tile(rv_ref, u).astype(BF16) for u in units}
    kk = {u: jnp.where(valid[u[0]], 1.0 - f[u], 0.0) for u in units}

    lf3 = jnp.concatenate([_split3(jnp.where(valid[u[0]], jnp.log(f[u]), 0.0)) for u in units], axis=1)
    cums = jnp.dot(cum_ref[...], lf3, preferred_element_type=F32)
    cum = {u: _join3(cums[:, i * 3 * LANES:(i + 1) * 3 * LANES]) for i, u in enumerate(units)}
    b = {u: cum[u][0:C] for u in units}

    def midpoint(u, lev):
        if lev < MATMUL_MID_LEVELS:
            return cum[u][(1 + lev) * C:(2 + lev) * C]
        half = 1 << lev
        return jnp.concatenate(
            [jnp.broadcast_to(b[u][s + half - 1:s + half, :], (2 * half, LANES)) for s in range(0, C, 2 * half)],
            axis=0)

    a = {u: jnp.where(ti == si, lax.dot_general(q[u].astype(BF16), kk[u].astype(BF16), _NT,
                                                preferred_element_type=F32), 0.0) for u in units}
    for lev in range(n_lev):
        half = 1 << lev
        up = (trow & half) != 0
        blk_mask = ((ti >> (lev + 1)) == (si >> (lev + 1))) & ((ti & half) != 0) & ((si & half) == 0)
        for u in units:
            bl = midpoint(u, lev)
            e = jnp.exp(jnp.where(up, b[u] - bl, bl - b[u]))
            qt = jnp.where(up, q[u] * e, 0.0).astype(BF16)
            kt = jnp.where(up, 0.0, kk[u] * e).astype(BF16)
            a[u] = a[u] + jnp.where(blk_mask, lax.dot_general(qt, kt, _NT, preferred_element_type=F32), 0.0)

    st = {h: st_ref[h] for h in H}
    for u in units:
        cc, h = u
        qe = (q[u] * jnp.exp(b[u])).astype(BF16)
        o = jnp.dot(a[u].astype(BF16), vb[u], preferred_element_type=F32)
        o_ref[cc * C:(cc + 1) * C, h * LANES:(h + 1) * LANES] = o + lax.dot_general(
            qe, st[h].astype(BF16), _NT, preferred_element_type=F32)
        b_last = b[u][C - 1:C, :]
        kd = (kk[u] * jnp.exp(b_last - b[u])).astype(BF16)
        st[h] = st[h] * jnp.exp(b_last) + lax.dot_general(vb[u], kd, _TN, preferred_element_type=F32)
    for h in H:
        st_ref[h] = st[h]

        @pl.when(step == n_steps - 1)
        def _(h=h):
            sout_ref[0, h] = st[h].T


def _gla_consts():
    C = REC_CHUNK
    t = jnp.arange(C)
    tri = t[None, :] <= t[:, None]
    blocks = [tri]
    for lev in range(MATMUL_MID_LEVELS):
        half = 1 << lev
        mid_row = (t >> (lev + 1)) * (2 * half) + half - 1
        blocks.append(t[None, :] <= mid_row[:, None])
    return jnp.concatenate(blocks, axis=0).astype(BF16)


def _gla_prompt(pj, batch, t_pad, t_real):
    rows = REC_CHUNK * REC_CHUNKS_PER_STEP
    nch = t_pad // rows
    cum = _gla_consts()
    row = lambda b, c: (b * nch + c, 0)
    const = lambda b, c: (0, 0)
    return pl.pallas_call(
        functools.partial(_gla_body, t_real=t_real),
        grid=(batch, nch),
        in_specs=[
            pl.BlockSpec((rows, 512), row),
            pl.BlockSpec((rows, 512), row),
            pl.BlockSpec((rows, 512), row),
            pl.BlockSpec(cum.shape, const),
        ],
        out_specs=[
            pl.BlockSpec((rows, 512), row),
            pl.BlockSpec((1, N_REC_HEADS, REC_DIM, REC_DIM), lambda b, c: (b, 0, 0, 0)),
        ],
        out_shape=[
            jax.ShapeDtypeStruct((batch * t_pad, 512), F32),
            jax.ShapeDtypeStruct((batch, N_REC_HEADS, REC_DIM, REC_DIM), F32),
        ],
        scratch_shapes=[pltpu.VMEM((N_REC_HEADS, REC_DIM, REC_DIM), F32)],
        compiler_params=pltpu.CompilerParams(
            dimension_semantics=("arbitrary", "arbitrary"), vmem_limit_bytes=VMEM_LIMIT),
        name="gla_prompt",
    )(pj["rq"], pj["f"], pj["rv"], cum)


def _gla_sample_body(rq_ref, f_ref, rv_ref, s0_ref, o_ref, sout_ref):
    b = pl.program_id(0)
    q = rq_ref[pl.ds(b, 1), :]
    f = f_ref[pl.ds(b, 1), :]
    v = rv_ref[pl.ds(b, 1), :]
    for h in range(N_REC_HEADS):
        sl = slice(h * LANES, (h + 1) * LANES)
        col = lambda r: jnp.broadcast_to(r[:, sl], (REC_DIM, LANES)).T
        s_new = col(f) * s0_ref[0, h] + col(1.0 - f) * v[:, sl]
        sout_ref[0, h] = s_new
        o_ref[0, :, sl] = jnp.sum(col(q) * s_new, axis=0, keepdims=True)


def _gla_sample(rq, f, rv, state):
    db = state.shape[0]
    full = lambda b: (0, 0)
    st = lambda b: (b, 0, 0, 0)
    o, s = pl.pallas_call(
        _gla_sample_body,
        grid=(db,),
        in_specs=[
            pl.BlockSpec(rq.shape, full), pl.BlockSpec(f.shape, full), pl.BlockSpec(rv.shape, full),
            pl.BlockSpec((1, N_REC_HEADS, REC_DIM, REC_DIM), st),
        ],
        out_specs=[
            pl.BlockSpec((1, 1, 512), lambda b: (b, 0, 0)),
            pl.BlockSpec((1, N_REC_HEADS, REC_DIM, REC_DIM), st),
        ],
        out_shape=[
            jax.ShapeDtypeStruct((db, 1, 512), F32),
            jax.ShapeDtypeStruct(state.shape, F32),
        ],
        compiler_params=pltpu.CompilerParams(dimension_semantics=("arbitrary",)),
        name="gla_sample",
    )(rq, f, rv, state)
    return o.reshape(db, 512), s


def _sample_select_body(pt_ref, *refs, n_pages, top_k):
    P = PAGES_PER_STEP
    ki_refs = refs[0:P]
    qi_ref, w_ref, kin_ref, sel_ref, sc_ref = refs[P:]
    g = pl.program_id(1)
    n_groups = n_pages // P
    n_rows = sc_ref.shape[0]
    qi = qi_ref[0]
    w = w_ref[0]

    rows = []
    for i in range(P):
        d = jnp.dot(qi, ki_refs[i][...].astype(BF16), preferred_element_type=F32)
        rows.append(jnp.sum(w * jnp.maximum(d, 0.0), axis=0, keepdims=True))
    sc_ref[pl.ds(pl.multiple_of(g * P, P), P), :] = jnp.concatenate(rows, axis=0)

    @pl.when(g == n_groups - 1)
    def _():
        d_new = jnp.sum(qi.astype(F32) * kin_ref[0].astype(F32), axis=1, keepdims=True)
        s_new = jnp.sum(w * jnp.maximum(d_new, 0.0), axis=0, keepdims=True)
        tail = lax.broadcasted_iota(jnp.int32, (n_rows - n_pages, LANES), 0) * LANES + \
            lax.broadcasted_iota(jnp.int32, (n_rows - n_pages, LANES), 1)
        sc_ref[n_pages:n_rows, :] = jnp.where(tail == 0, s_new, NEG_SCORE)

        s = sc_ref[...]
        adm = s > NEG_HALF
        top = jnp.max(s, keepdims=True)

        def count_ge(x):
            return jnp.sum(jnp.where(sc_ref[...] >= x, 1.0, 0.0), keepdims=True)

        def bracket_minmax(lo, hi):
            v = sc_ref[...]
            inb = (v >= lo) & (v < hi)
            return (jnp.min(jnp.where(inb, v, BIG), keepdims=True),
                    jnp.max(jnp.where(inb, v, -BIG), keepdims=True))

        n_adm = jnp.full((1, 1), float(n_pages * LANES + 1), F32)
        thr, n_tie = _select_threshold(count_ge, bracket_minmax, jnp.min(jnp.where(adm, s, BIG), keepdims=True),
                                       top + (jnp.abs(top) + 1.0), n_adm, n_adm > top_k, float(top_k))
        eqb = jnp.where(s == thr, 1.0, 0.0).astype(BF16)
        ri = lax.broadcasted_iota(jnp.int32, (LANES, LANES), 0)
        ci = lax.broadcasted_iota(jnp.int32, (LANES, LANES), 1)
        in_row = jnp.dot(eqb, jnp.where(ri < ci, 1.0, 0.0).astype(BF16), preferred_element_type=F32)
        row_tot = jnp.dot(eqb, jnp.ones((LANES, LANES), BF16), preferred_element_type=F32)
        rr = lax.broadcasted_iota(jnp.int32, (n_rows, n_rows), 0)
        rc = lax.broadcasted_iota(jnp.int32, (n_rows, n_rows), 1)
        before = jnp.dot(jnp.where(rc < rr, 1.0, 0.0).astype(BF16), row_tot.astype(BF16),
                         preferred_element_type=F32)
        rank = in_row + before
        keep = (s > thr) | ((s == thr) & (rank < n_tie))
        sel_ref[0] = jnp.where(keep & adm, 1.0, 0.0)


def _sample_attend_body(pt_ref, *refs, n_pages):
    P = PAGES_PER_STEP
    k_refs, v_refs = refs[0:P], refs[P:2 * P]
    sel_ref, q_ref, kn_ref, vn_ref, o_ref, m_ref, l_ref, acc_ref = refs[2 * P:]
    g = pl.program_id(1)
    n_groups = n_pages // P

    @pl.when(g == 0)
    def _():
        q = q_ref[0].astype(F32)
        s_own = jnp.sum(q * kn_ref[0].astype(F32), axis=1, keepdims=True)
        own_sel = sel_ref[0, n_pages:n_pages + 1, 0:1]
        m_ref[...] = jnp.broadcast_to(s_own + jnp.where(own_sel > 0.5, 0.0, NEG_BIAS), m_ref.shape)
        l_ref[...] = jnp.ones(l_ref.shape, F32)
        acc_ref[...] = jnp.broadcast_to(vn_ref[0].astype(F32), acc_ref.shape)

    q = q_ref[0]
    sel = sel_ref[0, pl.ds(pl.multiple_of(g * P, P), P), :]
    scores = []
    for i in range(P):
        s = jnp.dot(q, k_refs[i][...].astype(BF16), preferred_element_type=F32)
        scores.append(jnp.where(sel[i:i + 1, :] > 0.5, s, NEG_BIAS))
    s_all = jnp.concatenate(scores, axis=1)
    m_old = m_ref[...]
    m_new = jnp.maximum(m_old, jnp.max(s_all, axis=1, keepdims=True))
    alpha = jnp.exp2(m_old - m_new)
    p32 = jnp.exp2(s_all - m_new[:, 0:1])
    l_ref[...] = alpha * l_ref[...] + jnp.sum(p32, axis=1, keepdims=True)
    m_ref[...] = m_new
    p = p32.astype(BF16)
    pv = jnp.zeros(acc_ref.shape, F32)
    for i in range(P):
        pv = pv + lax.dot_general(p[:, i * LANES:(i + 1) * LANES], v_refs[i][...].astype(BF16), _NT,
                                  preferred_element_type=F32)
    acc_ref[...] = alpha[:, 0:1] * acc_ref[...] + pv

    @pl.when(g == n_groups - 1)
    def _():
        o_ref[0] = acc_ref[...] / l_ref[:, 0:1]


def _attn_sample(page_table, layer, cache_kit, cache_kt, cache_vt, qi, w, ki_new, q, k_new, v_new):
    db, n_pages = page_table.shape
    P = PAGES_PER_STEP
    n_groups = n_pages // P
    page = cache_kit.shape[3]
    kv_w = N_KV_HEADS * HEAD_DIM
    top_k = min(TOPK_MAX, (n_pages * page + 1) // 4)
    n_rows = -(-(n_pages + 1) // SUBLANES) * SUBLANES
    pt = page_table.reshape(-1)
    params = pltpu.CompilerParams(dimension_semantics=("arbitrary", "arbitrary"), vmem_limit_bytes=VMEM_LIMIT)

    def page_map(i):
        return lambda b, g, pt: (layer, pt[b * n_pages + g * P + i], 0, 0)

    per_b = lambda b, g, pt: (b, 0, 0)
    sel = pl.pallas_call(
        functools.partial(_sample_select_body, n_pages=n_pages, top_k=top_k),
        grid_spec=pltpu.PrefetchScalarGridSpec(
            num_scalar_prefetch=1,
            grid=(db, n_groups),
            in_specs=([pl.BlockSpec((None, None, IDX_DIM, page), page_map(i)) for i in range(P)]
                      + [pl.BlockSpec((1,) + a.shape[1:], per_b) for a in (qi, w, ki_new)]),
            out_specs=pl.BlockSpec((1, n_rows, LANES), per_b),
            scratch_shapes=[pltpu.VMEM((n_rows, LANES), F32)]),
        out_shape=jax.ShapeDtypeStruct((db, n_rows, LANES), F32),
        compiler_params=params,
        name="sample_select",
    )(pt, *([cache_kit] * P), qi, w, ki_new)

    out = pl.pallas_call(
        functools.partial(_sample_attend_body, n_pages=n_pages),
        grid_spec=pltpu.PrefetchScalarGridSpec(
            num_scalar_prefetch=1,
            grid=(db, n_groups),
            in_specs=([pl.BlockSpec((None, None, kv_w, page), page_map(i)) for i in range(P)]
                      + [pl.BlockSpec((None, None, kv_w, page), page_map(i)) for i in range(P)]
                      + [pl.BlockSpec((1,) + a.shape[1:], per_b) for a in (sel, q, k_new, v_new)]),
            out_specs=pl.BlockSpec((1, N_HEADS, kv_w), per_b),
            scratch_shapes=[
                pltpu.VMEM((N_HEADS, LANES), F32),
                pltpu.VMEM((N_HEADS, LANES), F32),
                pltpu.VMEM((N_HEADS, kv_w), F32),
            ]),
        out_shape=jax.ShapeDtypeStruct((db, N_HEADS, kv_w), F32),
        compiler_params=params,
        name="sample_attend",
    )(pt, *([cache_kt] * P), *([cache_vt] * P), sel, q, k_new, v_new)
    return jnp.concatenate([out[:, h, (h // 2) * HEAD_DIM:(h // 2 + 1) * HEAD_DIM] for h in range(N_HEADS)], axis=1)


def _back_body(x_ref, att_ref, atts_ref, ro_ref, ros_ref, gr_ref, ga_ref, gb_ref, grec_ref,
               wpa_ref, wpb_ref, wo_ref, gffn_ref, wgu_ref, wd_ref, y_ref, *, n_prompt_tiles):
    is_sample = pl.program_id(0) >= n_prompt_tiles
    ro = jnp.where(is_sample, ros_ref[...], ro_ref[...])
    att = jnp.where(is_sample, atts_ref[...], att_ref[...])
    gr = gr_ref[...]
    recs = []
    for h in range(N_REC_HEADS):
        r = ro[:, h * LANES:(h + 1) * LANES]
        r = r * lax.rsqrt(jnp.mean(r * r, axis=-1, keepdims=True) + EPS) * grec_ref[...]
        recs.append((r * gr[:, h * LANES:(h + 1) * LANES]).astype(BF16))
    rec = jnp.concatenate(recs, axis=1)
    a = jnp.dot(att, wpa_ref[...], preferred_element_type=F32)
    b = jnp.dot(rec, wpb_ref[...], preferred_element_type=F32)
    mix = (ga_ref[...] * a + gb_ref[...] * b).astype(BF16)
    y = x_ref[...] + jnp.dot(mix, wo_ref[...], preferred_element_type=F32)

    d_ff = wd_ref.shape[0]
    hb = (y * lax.rsqrt(jnp.mean(y * y, axis=-1, keepdims=True) + EPS) * gffn_ref[...]).astype(BF16)
    for c in range(d_ff // FFN_CHUNK):
        c0 = c * FFN_CHUNK
        gate = jnp.dot(hb, wgu_ref[:, c0:c0 + FFN_CHUNK], preferred_element_type=F32)
        up = jnp.dot(hb, wgu_ref[:, d_ff + c0:d_ff + c0 + FFN_CHUNK], preferred_element_type=F32)
        act = (gate * _sigmoid(gate) * up).astype(BF16)
        y = y + jnp.dot(act, wd_ref[c0:c0 + FFN_CHUNK, :], preferred_element_type=F32)
    y_ref[...] = y


def _back(x_all, att_p, att_s, ro_p, ro_s, gr, ga, gb, grec, wpa, wpb, wo, gffn, wgu, wd):
    n_rows, d = x_all.shape
    once = dict(pipeline_mode=pl.Buffered(1))
    n_prompt_tiles = att_p.shape[0] // TOK_TILE
    row = lambda i: (i, 0)
    prow = lambda i: (jnp.minimum(i, n_prompt_tiles - 1), 0)
    const = lambda i: (0, 0)
    return pl.pallas_call(
        functools.partial(_back_body, n_prompt_tiles=n_prompt_tiles),
        grid=(n_rows // TOK_TILE,),
        in_specs=[
            pl.BlockSpec((TOK_TILE, d), row),
            pl.BlockSpec((TOK_TILE, 512), prow), pl.BlockSpec((TOK_TILE, 512), const),
            pl.BlockSpec((TOK_TILE, 512), prow), pl.BlockSpec((TOK_TILE, 512), const),
            pl.BlockSpec((TOK_TILE, 512), row),
            pl.BlockSpec((TOK_TILE, d), row), pl.BlockSpec((TOK_TILE, d), row),
            pl.BlockSpec((1, LANES), const),
            pl.BlockSpec(wpa.shape, const, **once), pl.BlockSpec(wpb.shape, const, **once),
            pl.BlockSpec(wo.shape, const, **once),
            pl.BlockSpec((1, d), const),
            pl.BlockSpec(wgu.shape, const, **once), pl.BlockSpec(wd.shape, const, **once),
        ],
        out_specs=pl.BlockSpec((TOK_TILE, d), row),
        out_shape=jax.ShapeDtypeStruct((n_rows, d), F32),
        compiler_params=pltpu.CompilerParams(
            dimension_semantics=("arbitrary",), vmem_limit_bytes=VMEM_LIMIT),
        name="mixer_back_ffn",
    )(x_all, att_p, att_s, ro_p, ro_s, gr, ga, gb, grec, wpa, wpb, wo, gffn, wgu, wd)


def _rope_table(pos):
    half = ROT_DIM // 2
    inv = jnp.power(ROPE_THETA, -2.0 * jnp.arange(half, dtype=F32) / ROT_DIM)
    ang = pos.astype(F32)[:, None] * inv[None, :]
    cos, sin = jnp.cos(ang), jnp.sin(ang)
    n = pos.shape[0]
    one = jnp.ones((n, HEAD_DIM - ROT_DIM), F32)
    zero8 = jnp.zeros((n, half), F32)
    zero = jnp.zeros((n, HEAD_DIM - ROT_DIM), F32)
    c64 = jnp.concatenate([cos, cos, one], axis=1)
    up64 = jnp.concatenate([zero8, sin, zero], axis=1)
    dn64 = jnp.concatenate([-sin, zero8, zero], axis=1)
    return jnp.concatenate([c64, c64, up64, up64, dn64, dn64], axis=1)


def _pack_w_in(w):
    d = w.shape[0]
    cuts = (512, 768, 1024, 1536, 1600, 1608, 2120, 2632, 3144, 3656, 4680)
    q, k, v, qi, ki, wi, qr, fr, ir, gr, ga, gb = jnp.split(w, cuts, axis=1)
    pad = jnp.zeros((d, LANES - IDX_DIM - N_IDX_HEADS), w.dtype)
    return jnp.concatenate([q, k, v, qi, ki, wi, pad, qr, fr, ir, gr, ga, gb], axis=1).astype(BF16)


def kernel(x_prompt, x_sample, cache_k, cache_v, cache_idx_k, state_rec, page_table, meta_tokens,
           w_in, norm_mix, q_norm, k_norm, lb_raw, rec_norm, w_pa, w_pb, w_o, norm_ffn, w_gu, w_down):
    batch, seq, d = x_prompt.shape
    db = x_sample.shape[0]
    depth = w_in.shape[0]
    n_pages = page_table.shape[1]
    page = cache_k.shape[2]
    past_len = n_pages * page
    t_real = seq + N_META
    t_pad = -(-t_real // TOK_TILE) * TOK_TILE
    tiles_per_seq = t_pad // TOK_TILE
    n_prompt = batch * t_pad
    top_k = min(TOPK_MAX, seq // 4)

    sm = jax.nn.softmax(lb_raw.astype(F32), axis=0)
    lower_bounds = jnp.cumsum(sm, axis=0) - sm[0:1]

    seq_pad = jnp.zeros((t_pad - t_real, d), F32)
    pieces = []
    for b in range(batch):
        pieces += [meta_tokens.astype(F32), x_prompt[b], seq_pad]
    pieces += [x_sample.reshape(db, d), jnp.zeros((TOK_TILE - db, d), F32)]
    x_all = jnp.concatenate(pieces, axis=0)

    rope_tab = jnp.concatenate([
        _rope_table(jnp.arange(t_pad, dtype=jnp.int32)),
        _rope_table(jnp.full((TOK_TILE,), past_len, jnp.int32))], axis=0)

    n_pool = cache_k.shape[1]
    cache_kt = jnp.transpose(cache_k, (0, 1, 3, 4, 2)).reshape(depth, n_pool, N_KV_HEADS * HEAD_DIM, page)
    cache_vt = jnp.transpose(cache_v, (0, 1, 3, 4, 2)).reshape(depth, n_pool, N_KV_HEADS * HEAD_DIM, page)
    cache_kit = jnp.transpose(cache_idx_k, (0, 1, 3, 2))
    srows = slice(n_prompt, n_prompt + db)
    tile2 = lambda g_: jnp.concatenate([g_, g_]).reshape(1, LANES).astype(F32)

    pk, pv, pki, ps, sk, sv, ski, ss = [], [], [], [], [], [], [], []
    for l in range(depth):
        pj = _inproj(x_all, norm_mix[l].reshape(1, d).astype(F32), _pack_w_in(w_in[l]), rope_tab,
                     tile2(q_norm[l]), tile2(k_norm[l]), lower_bounds[l].reshape(1, -1),
                     tiles_per_seq, n_prompt // TOK_TILE)

        att_p = _attn_prompt(pj, batch, t_pad, t_real, top_k)
        ro_p, st_p = _gla_prompt(pj, batch, t_pad, t_real)

        q_s = pj["q"][srows].reshape(db, N_HEADS, LANES)
        zero_s = jnp.zeros_like(q_s)
        q_s = jnp.stack([jnp.concatenate([q_s[:, h] if h // 4 == s else zero_s[:, h] for s in range(2)], axis=1)
                         for h in range(N_HEADS)], axis=1)
        qi_s = pj["qi"][srows].reshape(db, N_IDX_HEADS, IDX_DIM)
        w_s = pj["kiwi"][srows, IDX_DIM:IDX_DIM + N_IDX_HEADS].reshape(db, N_IDX_HEADS, 1)
        ki_s = pj["kk"][srows, 0:IDX_DIM].reshape(db, 1, IDX_DIM)
        kn_s = pj["kb"][srows].reshape(db, 1, N_KV_HEADS * HEAD_DIM)
        vn_s = pj["vb"][srows].reshape(db, 1, N_KV_HEADS * HEAD_DIM)
        att_s = _attn_sample(page_table, l, cache_kit, cache_kt, cache_vt, qi_s, w_s, ki_s, q_s, kn_s, vn_s)
        ro_s, st_s = _gla_sample(pj["rq"][srows], pj["f"][srows], pj["rv"][srows], state_rec[l])

        pad_s = lambda a: jnp.pad(a, ((0, TOK_TILE - db), (0, 0)))
        x_all = _back(x_all, att_p, pad_s(att_s.astype(BF16)), ro_p, pad_s(ro_s),
                      pj["gr"], pj["ga"], pj["gb"], rec_norm[l].reshape(1, LANES).astype(F32),
                      w_pa[l].astype(BF16), w_pb[l].astype(BF16), w_o[l].astype(BF16),
                      norm_ffn[l].reshape(1, d).astype(F32), w_gu[l].astype(BF16), w_down[l].astype(BF16))

        seq_view = lambda a, wd: a[:n_prompt].reshape(batch, t_pad, wd)[:, :t_real]
        pk.append(seq_view(pj["kf"], 256).reshape(batch, t_real, N_KV_HEADS, HEAD_DIM))
        pv.append(seq_view(pj["vf"], 256).reshape(batch, t_real, N_KV_HEADS, HEAD_DIM))
        pki.append(seq_view(pj["kiwi"], LANES)[..., :IDX_DIM])
        ps.append(st_p)
        sk.append(pj["kf"][srows].reshape(db, 1, N_KV_HEADS, HEAD_DIM))
        sv.append(pj["vf"][srows].reshape(db, 1, N_KV_HEADS, HEAD_DIM))
        ski.append(pj["kiwi"][srows, :IDX_DIM].reshape(db, 1, IDX_DIM))
        ss.append(st_s)

    y_prompt = jnp.stack([x_all[b * t_pad + N_META:b * t_pad + t_real] for b in range(batch)])
    y_sample = x_all[srows].reshape(db, 1, d)
    return (y_prompt, y_sample, jnp.stack(pk), jnp.stack(pv), jnp.stack(pki), jnp.stack(ps),
            jnp.stack(sk), jnp.stack(sv), jnp.stack(ski), jnp.stack(ss))
```

```python
import functools
import math

import jax
import jax.numpy as jnp
from jax import lax
from jax.experimental import pallas as pl
from jax.experimental.pallas import tpu as pltpu

F32 = jnp.float32
BF16 = jnp.bfloat16

N_META = 16
N_HEADS = 8
HEAD_DIM = 64
N_KV_HEADS = 4
ROT_DIM = 16
ROPE_THETA = 500000.0
N_IDX_HEADS = 8
IDX_DIM = 64
IDX_W_SCALE = (N_IDX_HEADS * IDX_DIM) ** -0.5
TOPK_MAX = 256
N_REC_HEADS = 4
REC_DIM = 128
EPS = 1e-6
Q_SCALE = HEAD_DIM ** -0.5 * 1.4426950408889634

LANES = 128
SUBLANES = 8

TOK_TILE = 384
Q_TILE = 384
KEY_CHUNK = 128
KEY_GROUP = 3
SUM_ROWS = 16
REC_CHUNK = 64
REC_CHUNKS_PER_STEP = 3
MATMUL_MID_LEVELS = 3
PAGES_PER_STEP = 32
SELECT_PAGES_PER_STEP = 64
FFN_CHUNK = 256
VMEM_LIMIT = 56 * 1024 * 1024

NEG_SCORE = -3.0e38
NEG_HALF = -1.5e38
NEG_BIAS = -1.0e30
BIG = 3.0e38
BISECTIONS_PER_ROUND = 20
MAX_ROUNDS = 160

C_Q, C_K, C_V, C_QI, C_KIWI, C_RQ, C_FR, C_IR, C_GR, C_GA, C_GB, C_END = (
    0, 512, 768, 1024, 1536, 1664, 2176, 2688, 3200, 3712, 4736, 5760)

_NT = (((1,), (1,)), ((), ()))
_TN = (((0,), (0,)), ((), ()))


def _sigmoid(x):
    return 1.0 / (1.0 + jnp.exp(-x))


def _split3(x):
    hi = x.astype(BF16)
    r1 = x - hi.astype(F32)
    mid = r1.astype(BF16)
    lo = (r1 - mid.astype(F32)).astype(BF16)
    return jnp.concatenate([hi, mid, lo], axis=1)


def _join3(r):
    return r[:, 0:LANES] + r[:, LANES:2 * LANES] + r[:, 2 * LANES:3 * LANES]


def _inproj_body(x_ref, g_ref, w_ref, rope_ref, qn_ref, kn_ref, lb_ref,
                 q_ref, kf_ref, vf_ref, kb_ref, vb_ref, qi_ref, kiwi_ref, kk_ref,
                 rq_ref, f_ref, rv_ref, gr_ref, ga_ref, gb_ref):
    x = x_ref[...]
    h = x * lax.rsqrt(jnp.mean(x * x, axis=-1, keepdims=True) + EPS) * g_ref[...]
    hb = h.astype(BF16)
    rows = x.shape[0]
    lane = lax.broadcasted_iota(jnp.int32, (rows, LANES), 1)
    low = lane < HEAD_DIM
    cosv = rope_ref[:, 0:LANES]
    sin_up = rope_ref[:, LANES:2 * LANES]
    sin_dn = rope_ref[:, 2 * LANES:3 * LANES]

    def proj(c0, c1):
        return jnp.dot(hb, w_ref[:, c0:c1], preferred_element_type=F32)

    def rope(xs):
        return (xs * cosv + pltpu.roll(xs, ROT_DIM // 2, 1) * sin_up
                + pltpu.roll(xs, LANES - ROT_DIM // 2, 1) * sin_dn)

    def headnorm(xs, gain):
        sq = xs * xs
        s_lo = jnp.sum(jnp.where(low, sq, 0.0), axis=-1, keepdims=True)
        s_hi = jnp.sum(jnp.where(low, 0.0, sq), axis=-1, keepdims=True)
        ms = jnp.where(low, s_lo, s_hi) * (1.0 / HEAD_DIM)
        return xs * lax.rsqrt(ms + EPS) * gain

    qraw = proj(C_Q, C_K)
    for p in range(N_KV_HEADS):
        qs = rope(headnorm(qraw[:, p * LANES:(p + 1) * LANES], qn_ref[...])) * Q_SCALE
        qr = pltpu.roll(qs, HEAD_DIM, 1)
        if p % 2 == 0:
            h0, h1 = jnp.where(low, qs, 0.0), jnp.where(low, qr, 0.0)
        else:
            h0, h1 = jnp.where(low, 0.0, qr), jnp.where(low, 0.0, qs)
        q_ref[:, (2 * p) * LANES:(2 * p + 1) * LANES] = h0.astype(BF16)
        q_ref[:, (2 * p + 1) * LANES:(2 * p + 2) * LANES] = h1.astype(BF16)

    kraw = proj(C_K, C_V)
    for s in range(2):
        ks = rope(headnorm(kraw[:, s * LANES:(s + 1) * LANES], kn_ref[...]))
        kf_ref[:, s * LANES:(s + 1) * LANES] = ks
        kb_ref[:, s * LANES:(s + 1) * LANES] = ks.astype(BF16)

    vraw = proj(C_V, C_QI)
    vf_ref[...] = vraw
    vb_ref[...] = vraw.astype(BF16)

    qiraw = proj(C_QI, C_KIWI)
    for p in range(4):
        qi_ref[:, p * LANES:(p + 1) * LANES] = rope(qiraw[:, p * LANES:(p + 1) * LANES]).astype(BF16)

    kiwi = proj(C_KIWI, C_RQ)
    kir = rope(kiwi)
    kiwi_ref[...] = jnp.where(low, kir, kiwi * IDX_W_SCALE)
    kk_ref[...] = jnp.where(low, kir, pltpu.roll(kir, HEAD_DIM, 1)).astype(BF16)

    qr_ = proj(C_RQ, C_FR)
    rq_ref[...] = qr_ * _sigmoid(qr_)
    lb = lb_ref[...]
    f_ref[...] = lb + (1.0 - lb) * _sigmoid(proj(C_FR, C_IR))
    rv_ref[...] = proj(C_IR, C_GR)
    g_ = proj(C_GR, C_GA)
    gr_ref[...] = g_ * _sigmoid(g_)
    ga_ref[...] = _sigmoid(proj(C_GA, C_GB))
    gb_ref[...] = _sigmoid(proj(C_GB, C_END))


def _inproj(x_all, g, w, rope_tab, qn, kn, lb, tiles_per_seq, n_prompt_tiles):
    n_rows, d = x_all.shape
    n_tiles = n_rows // TOK_TILE
    row = lambda i: (i, 0)
    const = lambda i: (0, 0)
    rope_map = lambda i: (jnp.where(i < n_prompt_tiles, i % tiles_per_seq, tiles_per_seq), 0)
    widths = dict(q=(1024, BF16), kf=(256, F32), vf=(256, F32), kb=(256, BF16), vb=(256, BF16),
                  qi=(512, BF16), kiwi=(128, F32), kk=(128, BF16), rq=(512, F32), f=(512, F32),
                  rv=(512, F32), gr=(512, F32), ga=(1024, F32), gb=(1024, F32))
    out_shape = [jax.ShapeDtypeStruct((n_rows, wd), dt) for wd, dt in widths.values()]
    out_specs = [pl.BlockSpec((TOK_TILE, wd), row) for wd, _ in widths.values()]
    outs = pl.pallas_call(
        _inproj_body,
        grid=(n_tiles,),
        in_specs=[
            pl.BlockSpec((TOK_TILE, d), row),
            pl.BlockSpec((1, d), const),
            pl.BlockSpec((d, C_END), const, pipeline_mode=pl.Buffered(1)),
            pl.BlockSpec((TOK_TILE, 3 * LANES), rope_map),
            pl.BlockSpec((1, LANES), const),
            pl.BlockSpec((1, LANES), const),
            pl.BlockSpec((1, 512), const),
        ],
        out_specs=out_specs,
        out_shape=out_shape,
        compiler_params=pltpu.CompilerParams(
            dimension_semantics=("arbitrary",), vmem_limit_bytes=VMEM_LIMIT),
        name="inproj",
    )(x_all, g, w, rope_tab, qn, kn, lb)
    return dict(zip(widths.keys(), outs))


def _select_threshold(count_ge, bracket_minmax, lo0, hi0, n_adm, need, top_k):
    def halve(st):
        lo, hi, c_lo, c_hi, _ = st
        mid = 0.5 * (lo + hi)
        c_mid = count_ge(mid)
        ge = c_mid >= top_k
        lo2, c_lo2 = jnp.where(ge, mid, lo), jnp.where(ge, c_mid, c_lo)
        hi2, c_hi2 = jnp.where(ge, hi, mid), jnp.where(ge, c_hi, c_mid)
        r = top_k - c_hi2
        stop = (c_lo2 - c_hi2 == r) | (r == 1.0) | (mid <= lo) | (mid >= hi) | jnp.logical_not(need)
        return lo2, hi2, c_lo2, c_hi2, jnp.where(stop, 1.0, 0.0)

    def inner_cond(st):
        return (jnp.min(st[4]) < 0.5) & (st[5] < BISECTIONS_PER_ROUND)

    def inner_body(st):
        return halve(halve(st[:5])) + (st[5] + 2,)

    def outer_cond(st):
        return (jnp.min(st[4]) < 0.5) & (st[7] < MAX_ROUNDS)

    def outer_body(st):
        lo, hi, c_lo, c_hi, done = lax.while_loop(inner_cond, inner_body, st[:5] + (jnp.int32(0),))[:5]
        mn, mx = bracket_minmax(lo, hi)
        open_ = need & (done < 0.5)
        done = jnp.where(open_ & (mn == mx), 1.0, done)
        return jnp.where(open_, mn, lo), hi, c_lo, c_hi, done, mn, mx, st[7] + 1

    zero = jnp.zeros_like(lo0)
    lo, hi, c_lo, c_hi, _, mn, mx, _ = lax.while_loop(
        outer_cond, outer_body, (lo0, hi0, n_adm, zero, zero, zero, zero, jnp.int32(0)))
    r = top_k - c_hi
    thr = jnp.where(need, jnp.where(r == 1.0, mx, mn), NEG_SCORE)
    n_tie = jnp.where(need, jnp.where(c_lo - c_hi != r, r, BIG), 0.0)
    return thr, n_tie


def _attn_body(q_ref, qi_ref, wq_ref, kb_ref, vb_ref, kk_ref, o_ref,
               sc_ref, qia_ref, qa_ref, vt_ref, m_ref, acc_ref,
               da_ref, db_ref, sa_ref, sb_ref, *, top_k, t_real):
    j = pl.program_id(1)
    T = KEY_CHUNK
    Q = Q_TILE
    nchunks = (j + 1) * (Q // T)
    low = lax.broadcasted_iota(jnp.int32, (Q, LANES), 1) < HEAD_DIM

    @pl.when(j == 0)
    def _():
        def transpose_v(c, carry):
            blk = vb_ref[pl.ds(pl.multiple_of(c * T, T), T), :].astype(F32)
            for s in range(2):
                vt_ref[c, s, 0:LANES, :] = blk[:, s * LANES:(s + 1) * LANES].T.astype(BF16)
                vt_ref[c, s, LANES:LANES + SUM_ROWS, :] = jnp.ones((SUM_ROWS, T), BF16)
            return carry
        lax.fori_loop(0, vt_ref.shape[0], transpose_v, 0)

    wt = wq_ref[...].T
    qi = qi_ref[...]
    for h in range(N_IDX_HEADS):
        slab = qi[:, (h // 2) * LANES:(h // 2 + 1) * LANES]
        zero = jnp.zeros_like(slab)
        qia_ref[h * Q:(h + 1) * Q, :] = jnp.where(low, slab, zero) if h % 2 == 0 else jnp.where(low, zero, slab)
        qa_ref[h * Q:(h + 1) * Q, :] = q_ref[:, h * LANES:(h + 1) * LANES]

    G = KEY_GROUP
    ngroups = (nchunks + G - 1) // G
    last_chunk = vt_ref.shape[0] - 1

    def chunk_rows(ref, g, u, lanes):
        cc = jnp.minimum(g * G + u, last_chunk)
        return ref[pl.ds(pl.multiple_of(cc * T, T), T), lanes]

    def fold(x, op):
        return op(x.reshape(x.shape[0] // SUBLANES, SUBLANES, Q), axis=0)

    npairs = ngroups // 2
    odd_tail = ngroups % 2 == 1

    def idx_dots(dst, g):
        kk = jnp.concatenate([chunk_rows(kk_ref, g, u, slice(None)) for u in range(G)], axis=0)
        dst[...] = lax.dot_general(kk, qia_ref[...], _NT, preferred_element_type=F32)

    def idx_scores(src, g, carry):
        mn, mx = carry
        s = jnp.zeros((G * T, Q), F32)
        for h in range(N_IDX_HEADS):
            s = s + wt[IDX_DIM + h:IDX_DIM + h + 1, :] * jnp.maximum(src[:, h * Q:(h + 1) * Q], 0.0)
        kpos = g * (G * T) + lax.broadcasted_iota(jnp.int32, (G * T, Q), 0)
        adm = kpos <= j * Q + lax.broadcasted_iota(jnp.int32, (G * T, Q), 1)
        sc_ref[pl.ds(pl.multiple_of(g * (G * T), G * T), G * T), :] = jnp.where(adm, s, NEG_SCORE)
        return (jnp.minimum(mn, fold(jnp.where(adm, s, BIG), jnp.min)),
                jnp.maximum(mx, fold(jnp.where(adm, s, -BIG), jnp.max)))

    def phase_a(i, carry):
        idx_dots(db_ref, 2 * i + 1)
        carry = idx_scores(da_ref, 2 * i, carry)
        idx_dots(da_ref, 2 * i + 2)
        return idx_scores(db_ref, 2 * i + 1, carry)

    idx_dots(da_ref, 0)
    rng = lax.fori_loop(0, npairs, phase_a,
                        (jnp.full((SUBLANES, Q), BIG, F32), jnp.full((SUBLANES, Q), -BIG, F32)))
    mn, mx = lax.cond(odd_tail, lambda c: idx_scores(da_ref, ngroups - 1, c), lambda c: c, rng)

    def sc_group(g):
        return sc_ref[pl.ds(pl.multiple_of(g * (G * T), G * T), G * T), :]

    def count_ge(x):
        def step(g, acc):
            return acc + fold(jnp.where(sc_group(g) >= x, 1.0, 0.0), jnp.sum)
        return jnp.sum(lax.fori_loop(0, ngroups, step, jnp.zeros((SUBLANES, Q), F32)), axis=0, keepdims=True)

    def bracket_minmax(lo, hi):
        def step(g, carry):
            a, b = carry
            s = sc_group(g)
            inb = (s >= lo) & (s < hi)
            return (jnp.minimum(a, fold(jnp.where(inb, s, BIG), jnp.min)),
                    jnp.maximum(b, fold(jnp.where(inb, s, -BIG), jnp.max)))
        a, b = lax.fori_loop(0, ngroups, step,
                             (jnp.full((SUBLANES, Q), BIG, F32), jnp.full((SUBLANES, Q), -BIG, F32)))
        return jnp.min(a, axis=0, keepdims=True), jnp.max(b, axis=0, keepdims=True)

    qrow = j * Q + lax.broadcasted_iota(jnp.int32, (1, Q), 1)
    n_adm = (qrow + 1).astype(F32)
    need = (qrow + 1 > top_k) & (qrow < t_real)
    top = jnp.max(mx, axis=0, keepdims=True)
    thr, n_tie = _select_threshold(count_ge, bracket_minmax, jnp.min(mn, axis=0, keepdims=True),
                                   top + (jnp.abs(top) + 1.0), n_adm, need, float(top_k))

    m_ref[...] = jnp.full(m_ref.shape, NEG_SCORE, F32)
    acc_ref[...] = jnp.zeros(acc_ref.shape, F32)
    earlier = jnp.where(lax.broadcasted_iota(jnp.int32, (T, T), 1) < lax.broadcasted_iota(jnp.int32, (T, T), 0),
                        1.0, 0.0).astype(BF16)

    def qk_dots(dst, g):
        for sl in range(2):
            lanes = slice(sl * LANES, (sl + 1) * LANES)
            kc = jnp.concatenate([chunk_rows(kb_ref, g, u, lanes) for u in range(G)], axis=0)
            dst[sl] = lax.dot_general(kc, qa_ref[sl * 4 * Q:(sl + 1) * 4 * Q, :], _NT,
                                      preferred_element_type=F32)

    def softmax_pv(src, g, seen, issue_next):
        s = sc_group(g)
        eq = s == thr
        eqf = jnp.where(eq, 1.0, 0.0)
        eq_l = jnp.concatenate([eqf[u * T:(u + 1) * T] for u in range(G)], axis=1).astype(BF16)
        in_chunk = jnp.dot(earlier, eq_l, preferred_element_type=F32)
        ranks = []
        for u in range(G):
            ranks.append(in_chunk[:, u * Q:(u + 1) * Q] + seen)
            seen = seen + jnp.sum(eqf[u * T:(u + 1) * T], axis=0, keepdims=True)
        rank = jnp.concatenate(ranks, axis=0)
        bias = jnp.where(s > thr, 0.0, jnp.where(eq & (rank < n_tie), 0.0, NEG_BIAS))
        bias4 = jnp.concatenate([bias] * 4, axis=1)
        sts, m_news, alphas = [], [], []
        for sl in range(2):
            st = src[sl] + bias4
            m_old = m_ref[sl]
            m_new = jnp.maximum(m_old, jnp.max(st, axis=0, keepdims=True))
            m_ref[sl] = m_new
            sts.append(st)
            m_news.append(m_new)
            alphas.append(jnp.exp2(m_old - m_new))
        issue_next()
        for sl in range(2):
            p = jnp.exp2(sts[sl] - m_news[sl])
            vt =jnp.concatenate([vt_ref[jnp.minimum(g * G + u, last_chunk), sl] for u in range(G)], axis=1)
            acc_ref[sl] = alphas[sl] * acc_ref[sl] + jnp.dot(vt, p.astype(BF16), preferred_element_type=F32)
        return seen

    def phase_c(i, seen):
        seen = softmax_pv(sa_ref, 2 * i, seen, lambda: qk_dots(sb_ref, 2 * i + 1))
        return softmax_pv(sb_ref, 2 * i + 1, seen, lambda: qk_dots(sa_ref, 2 * i + 2))

    qk_dots(sa_ref, 0)
    seen = lax.fori_loop(0, npairs, phase_c, jnp.zeros((1, Q), F32))

    @pl.when(odd_tail)
    def _():
        softmax_pv(sa_ref, ngroups - 1, seen, lambda: None)

    for p in range(N_KV_HEADS):
        sl, hh = (2 * p) // 4, (2 * p) % 4
        out = acc_ref[sl, 0:LANES, :] / acc_ref[sl, LANES:LANES + 1, :]
        a0 = out[:, hh * Q:(hh + 1) * Q].T
        a1 = out[:, (hh + 1) * Q:(hh + 2) * Q].T
        if p % 2 == 0:
            slab = jnp.where(low, a0, pltpu.roll(a1, HEAD_DIM, 1))
        else:
            slab = jnp.where(low, pltpu.roll(a0, HEAD_DIM, 1), a1)
        o_ref[:, p * LANES:(p + 1) * LANES] = slab.astype(BF16)


def _attn_prompt(pj, batch, t_pad, t_real, top_k):
    nq = t_pad // Q_TILE
    nkc = t_pad // KEY_CHUNK
    grp_rows = KEY_GROUP * KEY_CHUNK
    n_grp = 2 * -(-(-(-nkc // KEY_GROUP)) // 2)
    qrow = lambda b, j: (b * nq + j, 0)
    seq = lambda b, j: (b, 0)
    return pl.pallas_call(
        functools.partial(_attn_body, top_k=top_k, t_real=t_real),
        grid=(batch, nq),
        in_specs=[
            pl.BlockSpec((Q_TILE, 1024), qrow),
            pl.BlockSpec((Q_TILE, 512), qrow),
            pl.BlockSpec((Q_TILE, LANES), qrow),
            pl.BlockSpec((t_pad, 256), seq, pipeline_mode=pl.Buffered(1)),
            pl.BlockSpec((t_pad, 256), seq, pipeline_mode=pl.Buffered(1)),
            pl.BlockSpec((t_pad, LANES), seq, pipeline_mode=pl.Buffered(1)),
        ],
        out_specs=pl.BlockSpec((Q_TILE, 512), qrow),
        out_shape=jax.ShapeDtypeStruct((batch * t_pad, 512), BF16),
        scratch_shapes=[
            pltpu.VMEM((n_grp * grp_rows, Q_TILE), F32),
            pltpu.VMEM((N_IDX_HEADS * Q_TILE, LANES), BF16),
            pltpu.VMEM((N_HEADS * Q_TILE, LANES), BF16),
            pltpu.VMEM((nkc, 2, LANES + SUM_ROWS, KEY_CHUNK), BF16),
            pltpu.VMEM((2, 1, 4 * Q_TILE), F32),
            pltpu.VMEM((2, LANES + SUM_ROWS, 4 * Q_TILE), F32),
            pltpu.VMEM((grp_rows, N_IDX_HEADS * Q_TILE), F32),
            pltpu.VMEM((grp_rows, N_IDX_HEADS * Q_TILE), F32),
            pltpu.VMEM((2, grp_rows, 4 * Q_TILE), F32),
            pltpu.VMEM((2, grp_rows, 4 * Q_TILE), F32),
        ],
        compiler_params=pltpu.CompilerParams(
            dimension_semantics=("arbitrary", "arbitrary"), vmem_limit_bytes=VMEM_LIMIT),
        name="attn_prompt",
    )(pj["q"], pj["qi"], pj["kiwi"], pj["kb"], pj["vb"], pj["kk"])


def _gla_body(rq_ref, f_ref, rv_ref, cum_ref, o_ref, sout_ref, st_ref, *, t_real):
    step = pl.program_id(1)
    n_steps = pl.num_programs(1)
    C = REC_CHUNK
    CH = o_ref.shape[0] // C
    n_lev = C.bit_length() - 1

    @pl.when(step == 0)
    def _():
        st_ref[...] = jnp.zeros(st_ref.shape, F32)

    trow = lax.broadcasted_iota(jnp.int32, (C, LANES), 0)
    ti = lax.broadcasted_iota(jnp.int32, (C, C), 0)
    si = lax.broadcasted_iota(jnp.int32, (C, C), 1)
    H = range(N_REC_HEADS)
    units = [(cc, h) for cc in range(CH) for h in H]
    valid = [(step * CH + cc) * C + trow < t_real for cc in range(CH)]

    def tile(ref, u):
        return ref[u[0] * C:(u[0] + 1) * C, u[1] * LANES:(u[1] + 1) * LANES]

    f = {u: tile(f_ref, u) for u in units}
    q = {u: tile(rq_ref, u) for u in units}
    vb = {u: tile(rv_ref, u).astype(BF16) for u in units}
    kk = {u: jnp.where(valid[u[0]], 1.0 - f[u], 0.0) for u in units}

    lf3 = jnp.concatenate([_split3(jnp.where(valid[u[0]], jnp.log(f[u]), 0.0)) for u in units], axis=1)
    cums = jnp.dot(cum_ref[...], lf3, preferred_element_type=F32)
    cum = {u: _join3(cums[:, i * 3 * LANES:(i + 1) * 3 * LANES]) for i, u in enumerate(units)}
    b = {u: cum[u][0:C] for u in units}

    def midpoint(u, lev):
        if lev < MATMUL_MID_LEVELS:
            return cum[u][(1 + lev) * C:(2 + lev) * C]
        half = 1 << lev
        return jnp.concatenate(
            [jnp.broadcast_to(b[u][s + half - 1:s + half, :], (2 * half, LANES)) for s in range(0, C, 2 * half)],
            axis=0)

    a = {u: jnp.where(ti == si, lax.dot_general(q[u].astype(BF16), kk[u].astype(BF16), _NT,
                                                preferred_element_type=F32), 0.0) for u in units}
    for lev in range(n_lev):
        half = 1 << lev
        up = (trow & half) != 0
        blk_mask = ((ti >> (lev + 1)) == (si >> (lev + 1))) & ((ti & half) != 0) & ((si & half) == 0)
        for u in units:
            bl = midpoint(u, lev)
            e = jnp.exp(jnp.where(up, b[u] - bl, bl - b[u]))
            qt = jnp.where(up, q[u] * e, 0.0).astype(BF16)
            kt = jnp.where(up, 0.0, kk[u] * e).astype(BF16)
            a[u] = a[u] + jnp.where(blk_mask, lax.dot_general(qt, kt, _NT, preferred_element_type=F32), 0.0)

    st = {h: st_ref[h] for h in H}
    for u in units:
        cc, h = u
        qe = (q[u] * jnp.exp(b[u])).astype(BF16)
        o = jnp.dot(a[u].astype(BF16), vb[u], preferred_element_type=F32)
        o_ref[cc * C:(cc + 1) * C, h * LANES:(h + 1) * LANES] = o + lax.dot_general(
            qe, st[h].astype(BF16), _NT, preferred_element_type=F32)
        b_last = b[u][C - 1:C, :]
        kd = (kk[u] * jnp.exp(b_last - b[u])).astype(BF16)
        st[h] = st[h] * jnp.exp(b_last) + lax.dot_general(vb[u], kd, _TN, preferred_element_type=F32)
    for h in H:
        st_ref[h] = st[h]

        @pl.when(step == n_steps - 1)
        def _(h=h):
            sout_ref[0, h] = st[h].T


def _gla_consts():
    C = REC_CHUNK
    t = jnp.arange(C)
    tri = t[None, :] <= t[:, None]
    blocks = [tri]
    for lev in range(MATMUL_MID_LEVELS):
        half = 1 << lev
        mid_row = (t >> (lev + 1)) * (2 * half) + half - 1
        blocks.append(t[None, :] <= mid_row[:, None])
    return jnp.concatenate(blocks, axis=0).astype(BF16)


def _gla_prompt(pj, batch, t_pad, t_real):
    rows = REC_CHUNK * REC_CHUNKS_PER_STEP
    nch = t_pad // rows
    cum = _gla_consts()
    row = lambda b, c: (b * nch + c, 0)
    const = lambda b, c: (0, 0)
    return pl.pallas_call(
        functools.partial(_gla_body, t_real=t_real),
        grid=(batch, nch),
        in_specs=[
            pl.BlockSpec((rows, 512), row),
            pl.BlockSpec((rows, 512), row),
            pl.BlockSpec((rows, 512), row),
            pl.BlockSpec(cum.shape, const),
        ],
        out_specs=[
            pl.BlockSpec((rows, 512), row),
            pl.BlockSpec((1, N_REC_HEADS, REC_DIM, REC_DIM), lambda b, c: (b, 0, 0, 0)),
        ],
        out_shape=[
            jax.ShapeDtypeStruct((batch * t_pad, 512), F32),
            jax.ShapeDtypeStruct((batch, N_REC_HEADS, REC_DIM, REC_DIM), F32),
        ],
        scratch_shapes=[pltpu.VMEM((N_REC_HEADS, REC_DIM, REC_DIM), F32)],
        compiler_params=pltpu.CompilerParams(
            dimension_semantics=("arbitrary", "arbitrary"), vmem_limit_bytes=VMEM_LIMIT),
        name="gla_prompt",
    )(pj["rq"], pj["f"], pj["rv"], cum)


def _gla_sample_body(rq_ref, f_ref, rv_ref, s0_ref, o_ref, sout_ref):
    b = pl.program_id(0)
    q = rq_ref[pl.ds(b, 1), :]
    f = f_ref[pl.ds(b, 1), :]
    v = rv_ref[pl.ds(b, 1), :]
    for h in range(N_REC_HEADS):
        sl = slice(h * LANES, (h + 1) * LANES)
        col = lambda r: jnp.broadcast_to(r[:, sl], (REC_DIM, LANES)).T
        s_new = col(f) * s0_ref[0, h] + col(1.0 - f) * v[:, sl]
        sout_ref[0, h] = s_new
        o_ref[0, :, sl] = jnp.sum(col(q) * s_new, axis=0, keepdims=True)


def _gla_sample(rq, f, rv, state):
    db = state.shape[0]
    full = lambda b: (0, 0)
    st = lambda b: (b, 0, 0, 0)
    o, s = pl.pallas_call(
        _gla_sample_body,
        grid=(db,),
        in_specs=[
            pl.BlockSpec(rq.shape, full), pl.BlockSpec(f.shape, full), pl.BlockSpec(rv.shape, full),
            pl.BlockSpec((1, N_REC_HEADS, REC_DIM, REC_DIM), st),
        ],
        out_specs=[
            pl.BlockSpec((1, 1, 512), lambda b: (b, 0, 0)),
            pl.BlockSpec((1, N_REC_HEADS, REC_DIM, REC_DIM), st),
        ],
        out_shape=[
            jax.ShapeDtypeStruct((db, 1, 512), F32),
            jax.ShapeDtypeStruct(state.shape, F32),
        ],
        compiler_params=pltpu.CompilerParams(dimension_semantics=("arbitrary",)),
        name="gla_sample",
    )(rq, f, rv, state)
    return o.reshape(db, 512), s


def _sample_select_body(pt_ref, *refs, n_pages, top_k):
    P = SELECT_PAGES_PER_STEP
    ki_refs = refs[0:P]
    qi_ref, w_ref, kin_ref, sel_ref, sc_ref = refs[P:]
    g = pl.program_id(1)
    n_groups = n_pages // P
    n_rows = sc_ref.shape[0]
    qi = qi_ref[0]
    w = w_ref[0]

    rows = []
    for i in range(P):
        d = jnp.dot(qi, ki_refs[i][...].astype(BF16), preferred_element_type=F32)
        rows.append(jnp.sum(w * jnp.maximum(d, 0.0), axis=0, keepdims=True))
    sc_ref[pl.ds(pl.multiple_of(g * P, P), P), :] = jnp.concatenate(rows, axis=0)

    @pl.when(g == n_groups - 1)
    def _():
        d_new = jnp.sum(qi.astype(F32) * kin_ref[0].astype(F32), axis=1, keepdims=True)
        s_new = jnp.sum(w * jnp.maximum(d_new, 0.0), axis=0, keepdims=True)
        tail = lax.broadcasted_iota(jnp.int32, (n_rows - n_pages, LANES), 0) * LANES + \
            lax.broadcasted_iota(jnp.int32, (n_rows - n_pages, LANES), 1)
        sc_ref[n_pages:n_rows, :] = jnp.where(tail == 0, s_new, NEG_SCORE)

        s = sc_ref[...]
        adm = s > NEG_HALF
        top = jnp.max(s, keepdims=True)

        def count_ge(x):
            return jnp.sum(jnp.where(sc_ref[...] >= x, 1.0, 0.0), keepdims=True)

        def bracket_minmax(lo, hi):
            v = sc_ref[...]
            inb = (v >= lo) & (v < hi)
            return (jnp.min(jnp.where(inb, v, BIG), keepdims=True),
                    jnp.max(jnp.where(inb, v, -BIG), keepdims=True))

        n_adm = jnp.full((1, 1), float(n_pages * LANES + 1), F32)
        thr, n_tie = _select_threshold(count_ge, bracket_minmax, jnp.min(jnp.where(adm, s, BIG), keepdims=True),
                                       top + (jnp.abs(top) + 1.0), n_adm, n_adm > top_k, float(top_k))
        eqb = jnp.where(s == thr, 1.0, 0.0).astype(BF16)
        ri = lax.broadcasted_iota(jnp.int32, (LANES, LANES), 0)
        ci = lax.broadcasted_iota(jnp.int32, (LANES, LANES), 1)
        in_row = jnp.dot(eqb, jnp.where(ri < ci, 1.0, 0.0).astype(BF16), preferred_element_type=F32)
        row_tot = jnp.dot(eqb, jnp.ones((LANES, LANES), BF16), preferred_element_type=F32)
        rr = lax.broadcasted_iota(jnp.int32, (n_rows, n_rows), 0)
        rc = lax.broadcasted_iota(jnp.int32, (n_rows, n_rows), 1)
        before = jnp.dot(jnp.where(rc < rr, 1.0, 0.0).astype(BF16), row_tot.astype(BF16),
                         preferred_element_type=F32)
        rank = in_row + before
        keep = (s > thr) | ((s == thr) & (rank < n_tie))
        sel_ref[0] = jnp.where(keep & adm, 1.0, 0.0)


def _sample_attend_body(pt_ref, *refs, n_pages):
    P = PAGES_PER_STEP
    k_refs, v_refs = refs[0:P], refs[P:2 * P]
    sel_ref, q_ref, kn_ref, vn_ref, o_ref, m_ref, l_ref, acc_ref = refs[2 * P:]
    g = pl.program_id(1)
    n_groups = n_pages // P

    q_slab = q_ref[0]
    first = lax.broadcasted_iota(jnp.int32, q_slab.shape, 0) < N_HEADS // 2
    zero = jnp.zeros_like(q_slab)
    q = jnp.concatenate([jnp.where(first, q_slab, zero), jnp.where(first, zero, q_slab)], axis=1)

    @pl.when(g == 0)
    def _():
        s_own = jnp.sum(q.astype(F32) * kn_ref[0].astype(F32), axis=1, keepdims=True)
        own_sel = sel_ref[0, n_pages:n_pages + 1, 0:1]
        m_ref[...] = jnp.broadcast_to(s_own + jnp.where(own_sel > 0.5, 0.0, NEG_BIAS), m_ref.shape)
        l_ref[...] = jnp.ones(l_ref.shape, F32)
        acc_ref[...] = jnp.broadcast_to(vn_ref[0].astype(F32), acc_ref.shape)

    sel = sel_ref[0, pl.ds(pl.multiple_of(g * P, P), P), :]
    scores = []
    for i in range(P):
        s = jnp.dot(q, k_refs[i][...].astype(BF16), preferred_element_type=F32)
        scores.append(jnp.where(sel[i:i + 1, :] > 0.5, s, NEG_BIAS))
    s_all = jnp.concatenate(scores, axis=1)
    m_old = m_ref[...]
    m_new = jnp.maximum(m_old, jnp.max(s_all, axis=1, keepdims=True))
    alpha = jnp.exp2(m_old - m_new)
    p32 = jnp.exp2(s_all - m_new[:, 0:1])
    l_ref[...] = alpha * l_ref[...] + jnp.sum(p32, axis=1, keepdims=True)
    m_ref[...] = m_new
    p = p32.astype(BF16)
    pv = jnp.zeros(acc_ref.shape, F32)
    for i in range(P):
        pv = pv + lax.dot_general(p[:, i * LANES:(i + 1) * LANES], v_refs[i][...].astype(BF16), _NT,
                                  preferred_element_type=F32)
    acc_ref[...] = alpha[:, 0:1] * acc_ref[...] + pv

    @pl.when(g == n_groups - 1)
    def _():
        res = acc_ref[...] / l_ref[:, 0:1]
        for h in range(N_HEADS):
            o_ref[0, h:h + 1, :] = res[h:h + 1, (h // 2) * HEAD_DIM:(h // 2 + 1) * HEAD_DIM]


def _attn_sample(page_table, layer, cache_kit, cache_kt, cache_vt, qi, w, ki_new, q, k_new, v_new):
    db, n_pages = page_table.shape
    page = cache_kit.shape[3]
    kv_w = N_KV_HEADS * HEAD_DIM
    top_k = min(TOPK_MAX, (n_pages * page + 1) // 4)
    n_rows = -(-(n_pages + 1) // SUBLANES) * SUBLANES
    pt = page_table.reshape(-1)
    params = pltpu.CompilerParams(dimension_semantics=("arbitrary", "arbitrary"), vmem_limit_bytes=VMEM_LIMIT)

    def page_specs(rows, pages_per_step):
        def page_map(i):
            return lambda b, g, pt: (layer, pt[b * n_pages + g * pages_per_step + i], 0, 0)
        return [pl.BlockSpec((None, None, rows, page), page_map(i)) for i in range(pages_per_step)]

    per_b = lambda b, g, pt: (b, 0, 0)
    P = SELECT_PAGES_PER_STEP
    sel = pl.pallas_call(
        functools.partial(_sample_select_body, n_pages=n_pages, top_k=top_k),
        grid_spec=pltpu.PrefetchScalarGridSpec(
            num_scalar_prefetch=1,
            grid=(db, n_pages // P),
            in_specs=(page_specs(IDX_DIM, P)
                      + [pl.BlockSpec((1,) + a.shape[1:], per_b) for a in (qi, w, ki_new)]),
            out_specs=pl.BlockSpec((1, n_rows, LANES), per_b),
            scratch_shapes=[pltpu.VMEM((n_rows, LANES), F32)]),
        out_shape=jax.ShapeDtypeStruct((db, n_rows, LANES), F32),
        compiler_params=params,
        name="sample_select",
    )(pt, *([cache_kit] * P), qi, w, ki_new)

    P = PAGES_PER_STEP
    out = pl.pallas_call(
        functools.partial(_sample_attend_body, n_pages=n_pages),
        grid_spec=pltpu.PrefetchScalarGridSpec(
            num_scalar_prefetch=1,
            grid=(db, n_pages // P),
            in_specs=(page_specs(kv_w, P) + page_specs(kv_w, P)
                      + [pl.BlockSpec((1,) + a.shape[1:], per_b) for a in (sel, q, k_new, v_new)]),
            out_specs=pl.BlockSpec((1, N_HEADS, HEAD_DIM), per_b),
            scratch_shapes=[
                pltpu.VMEM((N_HEADS, LANES), F32),
                pltpu.VMEM((N_HEADS, LANES), F32),
                pltpu.VMEM((N_HEADS, kv_w), F32),
            ]),
        out_shape=jax.ShapeDtypeStruct((db, N_HEADS, HEAD_DIM), F32),
        compiler_params=params,
        name="sample_attend",
    )(pt, *([cache_kt] * P), *([cache_vt] * P), sel, q, k_new, v_new)
    return out.reshape(db, N_HEADS * HEAD_DIM)


def _back_body(x_ref, att_ref, atts_ref, ro_ref, ros_ref, gr_ref, ga_ref, gb_ref, grec_ref,
               wpa_ref, wpb_ref, wo_ref, gffn_ref, wgu_ref, wd_ref, y_ref, *, n_prompt_tiles):
    is_sample = pl.program_id(0) >= n_prompt_tiles
    ro = jnp.where(is_sample, ros_ref[...], ro_ref[...])
    att = jnp.where(is_sample, atts_ref[...], att_ref[...])
    gr = gr_ref[...]
    recs = []
    for h in range(N_REC_HEADS):
        r = ro[:, h * LANES:(h + 1) * LANES]
        r = r * lax.rsqrt(jnp.mean(r * r, axis=-1, keepdims=True) + EPS) * grec_ref[...]
        recs.append((r * gr[:, h * LANES:(h + 1) * LANES]).astype(BF16))
    rec = jnp.concatenate(recs, axis=1)
    a = jnp.dot(att, wpa_ref[...], preferred_element_type=F32)
    b = jnp.dot(rec, wpb_ref[...], preferred_element_type=F32)
    mix = (ga_ref[...] * a + gb_ref[...] * b).astype(BF16)
    y = x_ref[...] + jnp.dot(mix, wo_ref[...], preferred_element_type=F32)

    d_ff = wd_ref.shape[0]
    hb = (y * lax.rsqrt(jnp.mean(y * y, axis=-1, keepdims=True) + EPS) * gffn_ref[...]).astype(BF16)
    for c in range(d_ff // FFN_CHUNK):
        c0 = c * FFN_CHUNK
        gate = jnp.dot(hb, wgu_ref[:, c0:c0 + FFN_CHUNK], preferred_element_type=F32)
        up = jnp.dot(hb, wgu_ref[:, d_ff + c0:d_ff + c0 + FFN_CHUNK], preferred_element_type=F32)
        act = (gate * _sigmoid(gate) * up).astype(BF16)
        y = y + jnp.dot(act, wd_ref[c0:c0 + FFN_CHUNK, :], preferred_element_type=F32)
    y_ref[...] = y


def _back(x_all, att_p, att_s, ro_p, ro_s, gr, ga, gb, grec, wpa, wpb, wo, gffn, wgu, wd):
    n_rows, d = x_all.shape
    once = dict(pipeline_mode=pl.Buffered(1))
    n_prompt_tiles = att_p.shape[0] // TOK_TILE
    row = lambda i: (i, 0)
    prow = lambda i: (jnp.minimum(i, n_prompt_tiles - 1), 0)
    const = lambda i: (0, 0)
    return pl.pallas_call(
        functools.partial(_back_body, n_prompt_tiles=n_prompt_tiles),
        grid=(n_rows // TOK_TILE,),
        in_specs=[
            pl.BlockSpec((TOK_TILE, d), row),
            pl.BlockSpec((TOK_TILE, 512), prow), pl.BlockSpec((TOK_TILE, 512), const),
            pl.BlockSpec((TOK_TILE, 512), prow), pl.BlockSpec((TOK_TILE, 512), const),
            pl.BlockSpec((TOK_TILE, 512), row),
            pl.BlockSpec((TOK_TILE, d), row), pl.BlockSpec((TOK_TILE, d), row),
            pl.BlockSpec((1, LANES), const),
            pl.BlockSpec(wpa.shape, const, **once), pl.BlockSpec(wpb.shape, const, **once),
            pl.BlockSpec(wo.shape, const, **once),
            pl.BlockSpec((1, d), const),
            pl.BlockSpec(wgu.shape, const, **once), pl.BlockSpec(wd.shape, const, **once),
        ],
        out_specs=pl.BlockSpec((TOK_TILE, d), row),
        out_shape=jax.ShapeDtypeStruct((n_rows, d), F32),
        compiler_params=pltpu.CompilerParams(
            dimension_semantics=("arbitrary",), vmem_limit_bytes=VMEM_LIMIT),
        name="mixer_back_ffn",
    )(x_all, att_p, att_s, ro_p, ro_s, gr, ga, gb, grec, wpa, wpb, wo, gffn, wgu, wd)


def _rope_table(pos):
    half = ROT_DIM // 2
    inv = jnp.power(ROPE_THETA, -2.0 * jnp.arange(half, dtype=F32) / ROT_DIM)
    ang = pos.astype(F32)[:, None] * inv[None, :]
    cos, sin = jnp.cos(ang), jnp.sin(ang)
    n = pos.shape[0]
    one = jnp.ones((n, HEAD_DIM - ROT_DIM), F32)
    zero8 = jnp.zeros((n, half), F32)
    zero = jnp.zeros((n, HEAD_DIM - ROT_DIM), F32)
    c64 = jnp.concatenate([cos, cos, one], axis=1)
    up64 = jnp.concatenate([zero8, sin, zero], axis=1)
    dn64 = jnp.concatenate([-sin, zero8, zero], axis=1)
    return jnp.concatenate([c64, c64, up64, up64, dn64, dn64], axis=1)


def _pack_w_in(w):
    d = w.shape[0]
    cuts = (512, 768, 1024, 1536, 1600, 1608, 2120, 2632, 3144, 3656, 4680)
    q, k, v, qi, ki, wi, qr, fr, ir, gr, ga, gb = jnp.split(w, cuts, axis=1)
    pad = jnp.zeros((d, LANES - IDX_DIM - N_IDX_HEADS), w.dtype)
    return jnp.concatenate([q, k, v, qi, ki, wi, pad, qr, fr, ir, gr, ga, gb], axis=1).astype(BF16)


def kernel(x_prompt, x_sample, cache_k, cache_v, cache_idx_k, state_rec, page_table, meta_tokens,
           w_in, norm_mix, q_norm, k_norm, lb_raw, rec_norm, w_pa, w_pb, w_o, norm_ffn, w_gu, w_down):
    batch, seq, d = x_prompt.shape
    db = x_sample.shape[0]
    depth = w_in.shape[0]
    n_pages = page_table.shape[1]
    page = cache_k.shape[2]
    past_len = n_pages * page
    t_real = seq + N_META
    seq_align = math.lcm(TOK_TILE, Q_TILE, REC_CHUNK * REC_CHUNKS_PER_STEP)
    t_pad = -(-t_real // seq_align) * seq_align
    tiles_per_seq = t_pad // TOK_TILE
    n_prompt = batch * t_pad
    top_k = min(TOPK_MAX, seq // 4)

    sm = jax.nn.softmax(lb_raw.astype(F32), axis=0)
    lower_bounds = jnp.cumsum(sm, axis=0) - sm[0:1]

    seq_pad = jnp.zeros((t_pad - t_real, d), F32)
    pieces = []
    for b in range(batch):
        pieces += [meta_tokens.astype(F32), x_prompt[b], seq_pad]
    pieces += [x_sample.reshape(db, d), jnp.zeros((TOK_TILE - db, d), F32)]
    x_all = jnp.concatenate(pieces, axis=0)

    rope_tab = jnp.concatenate([
        _rope_table(jnp.arange(t_pad, dtype=jnp.int32)),
        _rope_table(jnp.full((TOK_TILE,), past_len, jnp.int32))], axis=0)

    n_pool = cache_k.shape[1]
    cache_kt = jnp.transpose(cache_k, (0, 1, 3, 4, 2)).reshape(depth, n_pool, N_KV_HEADS * HEAD_DIM, page)
    cache_vt = jnp.transpose(cache_v, (0, 1, 3, 4, 2)).reshape(depth, n_pool, N_KV_HEADS * HEAD_DIM, page)
    cache_kit = jnp.transpose(cache_idx_k, (0, 1, 3, 2))
    srows = slice(n_prompt, n_prompt + db)
    tile2 = lambda g_: jnp.concatenate([g_, g_]).reshape(1, LANES).astype(F32)

    pk, pv, pki, ps, sk, sv, ski, ss = [], [], [], [], [], [], [], []
    for l in range(depth):
        pj = _inproj(x_all, norm_mix[l].reshape(1, d).astype(F32), _pack_w_in(w_in[l]), rope_tab,
                     tile2(q_norm[l]), tile2(k_norm[l]), lower_bounds[l].reshape(1, -1),
                     tiles_per_seq, n_prompt // TOK_TILE)

        att_p = _attn_prompt(pj, batch, t_pad, t_real, top_k)
        ro_p, st_p = _gla_prompt(pj, batch, t_pad, t_real)

        q_s = pj["q"][srows].reshape(db, N_HEADS, LANES)
        qi_s = pj["qi"][srows].reshape(db, N_IDX_HEADS, IDX_DIM)
        w_s = pj["kiwi"][srows, IDX_DIM:IDX_DIM + N_IDX_HEADS].reshape(db, N_IDX_HEADS, 1)
        ki_s = pj["kk"][srows, 0:IDX_DIM].reshape(db, 1, IDX_DIM)
        kn_s = pj["kb"][srows].reshape(db, 1, N_KV_HEADS * HEAD_DIM)
        vn_s = pj["vb"][srows].reshape(db, 1, N_KV_HEADS * HEAD_DIM)
        att_s = _attn_sample(page_table, l, cache_kit, cache_kt, cache_vt, qi_s, w_s, ki_s, q_s, kn_s, vn_s)
        ro_s, st_s = _gla_sample(pj["rq"][srows], pj["f"][srows], pj["rv"][srows], state_rec[l])

        pad_s = lambda a: jnp.pad(a, ((0, TOK_TILE - db), (0, 0)))
        x_all = _back(x_all, att_p, pad_s(att_s.astype(BF16)), ro_p, pad_s(ro_s),
                      pj["gr"], pj["ga"], pj["gb"], rec_norm[l].reshape(1, LANES).astype(F32),
                      w_pa[l].astype(BF16), w_pb[l].astype(BF16), w_o[l].astype(BF16),
                      norm_ffn[l].reshape(1, d).astype(F32), w_gu[l].astype(BF16), w_down[l].astype(BF16))

        seq_view = lambda a, wd: a[:n_prompt].reshape(batch, t_pad, wd)[:, :t_real]
        pk.append(seq_view(pj["kf"], 256).reshape(batch, t_real, N_KV_HEADS, HEAD_DIM))
        pv.append(seq_view(pj["vf"], 256).reshape(batch, t_real, N_KV_HEADS, HEAD_DIM))
        pki.append(seq_view(pj["kiwi"], LANES)[..., :IDX_DIM])
        ps.append(st_p)
        sk.append(pj["kf"][srows].reshape(db, 1, N_KV_HEADS, HEAD_DIM))
        sv.append(pj["vf"][srows].reshape(db, 1, N_KV_HEADS, HEAD_DIM))
        ski.append(pj["kiwi"][srows, :IDX_DIM].reshape(db, 1, IDX_DIM))
        ss.append(st_s)

    y_prompt = jnp.stack([x_all[b * t_pad + N_META:b * t_pad + t_real] for b in range(batch)])
    y_sample = x_all[srows].reshape(db, 1, d)
    return (y_prompt, y_sample, jnp.stack(pk), jnp.stack(pv), jnp.stack(pki), jnp.stack(ps),
            jnp.stack(sk), jnp.stack(sv), jnp.stack(ski), jnp.stack(ss))
```

```python
import functools
import math

import jax
import jax.numpy as jnp
from jax import lax
from jax.experimental import pallas as pl
from jax.experimental.pallas import tpu as pltpu

F32 = jnp.float32
BF16 = jnp.bfloat16

N_META = 16
N_HEADS = 8
HEAD_DIM = 64
N_KV_HEADS = 4
ROT_DIM = 16
ROPE_THETA = 500000.0
N_IDX_HEADS = 8
IDX_DIM = 64
IDX_W_SCALE = (N_IDX_HEADS * IDX_DIM) ** -0.5
TOPK_MAX = 256
N_REC_HEADS = 4
REC_DIM = 128
EPS = 1e-6
Q_SCALE = HEAD_DIM ** -0.5 * 1.4426950408889634

LANES = 128
SUBLANES = 8

TOK_TILE = 384
Q_TILE = 384
KEY_CHUNK = 128
KEY_GROUP = 3
SUM_ROWS = 16
REC_CHUNK = 64
REC_CHUNKS_PER_STEP = 6
MATMUL_MID_LEVELS = 3
PAGES_PER_STEP = 32
SELECT_PAGES_PER_STEP = 64
FFN_CHUNK = 256
VMEM_LIMIT = 56 * 1024 * 1024

NEG_SCORE = -3.0e38
NEG_HALF = -1.5e38
NEG_BIAS = -1.0e30
BIG = 3.0e38
BISECTIONS_PER_ROUND = 20
MAX_ROUNDS = 160

C_Q, C_K, C_V, C_QI, C_KIWI, C_RQ, C_FR, C_IR, C_GR, C_GA, C_GB, C_END = (
    0, 512, 768, 1024, 1536, 1664, 2176, 2688, 3200, 3712, 4736, 5760)

_NT = (((1,), (1,)), ((), ()))
_TN = (((0,), (0,)), ((), ()))


def _sigmoid(x):
    return 1.0 / (1.0 + jnp.exp(-x))


def _split3(x):
    hi = x.astype(BF16)
    r1 = x - hi.astype(F32)
    mid = r1.astype(BF16)
    lo = (r1 - mid.astype(F32)).astype(BF16)
    return jnp.concatenate([hi, mid, lo], axis=1)


def _join3(r):
    return r[:, 0:LANES] + r[:, LANES:2 * LANES] + r[:, 2 * LANES:3 * LANES]


def _inproj_body(x_ref, g_ref, w_ref, rope_ref, qn_ref, kn_ref, lb_ref,
                 q_ref, kf_ref, vf_ref, kb_ref, vb_ref, qi_ref, kiwi_ref, kk_ref,
                 rq_ref, f_ref, rv_ref, gr_ref, ga_ref, gb_ref):
    x = x_ref[...]
    h = x * lax.rsqrt(jnp.mean(x * x, axis=-1, keepdims=True) + EPS) * g_ref[...]
    hb = h.astype(BF16)
    rows = x.shape[0]
    lane = lax.broadcasted_iota(jnp.int32, (rows, LANES), 1)
    low = lane < HEAD_DIM
    cosv = rope_ref[:, 0:LANES]
    sin_up = rope_ref[:, LANES:2 * LANES]
    sin_dn = rope_ref[:, 2 * LANES:3 * LANES]

    def proj(c0, c1):
        return jnp.dot(hb, w_ref[:, c0:c1], preferred_element_type=F32)

    def rope(xs):
        return (xs * cosv + pltpu.roll(xs, ROT_DIM // 2, 1) * sin_up
                + pltpu.roll(xs, LANES - ROT_DIM // 2, 1) * sin_dn)

    def headnorm(xs, gain):
        sq = xs * xs
        s_lo = jnp.sum(jnp.where(low, sq, 0.0), axis=-1, keepdims=True)
        s_hi = jnp.sum(jnp.where(low, 0.0, sq), axis=-1, keepdims=True)
        ms = jnp.where(low, s_lo, s_hi) * (1.0 / HEAD_DIM)
        return xs * lax.rsqrt(ms + EPS) * gain

    qraw = proj(C_Q, C_K)
    for p in range(N_KV_HEADS):
        qs = rope(headnorm(qraw[:, p * LANES:(p + 1) * LANES], qn_ref[...])) * Q_SCALE
        qr = pltpu.roll(qs, HEAD_DIM, 1)
        if p % 2 == 0:
            h0, h1 = jnp.where(low, qs, 0.0), jnp.where(low, qr, 0.0)
        else:
            h0, h1 = jnp.where(low, 0.0, qr), jnp.where(low, 0.0, qs)
        q_ref[:, (2 * p) * LANES:(2 * p + 1) * LANES] = h0.astype(BF16)
        q_ref[:, (2 * p + 1) * LANES:(2 * p + 2) * LANES] = h1.astype(BF16)

    kraw = proj(C_K, C_V)
    for s in range(2):
        ks = rope(headnorm(kraw[:, s * LANES:(s + 1) * LANES], kn_ref[...]))
        kf_ref[:, s * LANES:(s + 1) * LANES] = ks
        kb_ref[:, s * LANES:(s + 1) * LANES] = ks.astype(BF16)

    vraw = proj(C_V, C_QI)
    vf_ref[...] = vraw
    vb_ref[...] = vraw.astype(BF16)

    qiraw = proj(C_QI, C_KIWI)
    for p in range(4):
        qi_ref[:, p * LANES:(p + 1) * LANES] = rope(qiraw[:, p * LANES:(p + 1) * LANES]).astype(BF16)

    kiwi = proj(C_KIWI, C_RQ)
    kir = rope(kiwi)
    kiwi_ref[...] = jnp.where(low, kir, kiwi * IDX_W_SCALE)
    kk_ref[...] = jnp.where(low, kir, pltpu.roll(kir, HEAD_DIM, 1)).astype(BF16)

    qr_ = proj(C_RQ, C_FR)
    rq_ref[...] = qr_ * _sigmoid(qr_)
    lb = lb_ref[...]
    f_ref[...] = lb + (1.0 - lb) * _sigmoid(proj(C_FR, C_IR))
    rv_ref[...] = proj(C_IR, C_GR)
    g_ = proj(C_GR, C_GA)
    gr_ref[...] = g_ * _sigmoid(g_)
    ga_ref[...] = _sigmoid(proj(C_GA, C_GB))
    gb_ref[...] = _sigmoid(proj(C_GB, C_END))


def _inproj(x_all, g, w, rope_tab, qn, kn, lb, tiles_per_seq, n_prompt_tiles):
    n_rows, d = x_all.shape
    n_tiles = n_rows // TOK_TILE
    row = lambda i: (i, 0)
    const = lambda i: (0, 0)
    rope_map = lambda i: (jnp.where(i < n_prompt_tiles, i % tiles_per_seq, tiles_per_seq), 0)
    widths = dict(q=(1024, BF16), kf=(256, F32), vf=(256, F32), kb=(256, BF16), vb=(256, BF16),
                  qi=(512, BF16), kiwi=(128, F32), kk=(128, BF16), rq=(512, F32), f=(512, F32),
                  rv=(512, F32), gr=(512, F32), ga=(1024, F32), gb=(1024, F32))
    out_shape = [jax.ShapeDtypeStruct((n_rows, wd), dt) for wd, dt in widths.values()]
    out_specs = [pl.BlockSpec((TOK_TILE, wd), row) for wd, _ in widths.values()]
    outs = pl.pallas_call(
        _inproj_body,
        grid=(n_tiles,),
        in_specs=[
            pl.BlockSpec((TOK_TILE, d), row),
            pl.BlockSpec((1, d), const),
            pl.BlockSpec((d, C_END), const, pipeline_mode=pl.Buffered(1)),
            pl.BlockSpec((TOK_TILE, 3 * LANES), rope_map),
            pl.BlockSpec((1, LANES), const),
            pl.BlockSpec((1, LANES), const),
            pl.BlockSpec((1, 512), const),
        ],
        out_specs=out_specs,
        out_shape=out_shape,
        compiler_params=pltpu.CompilerParams(
            dimension_semantics=("arbitrary",), vmem_limit_bytes=VMEM_LIMIT),
        name="inproj",
    )(x_all, g, w, rope_tab, qn, kn, lb)
    return dict(zip(widths.keys(), outs))


def _select_threshold(count_ge, bracket_minmax, lo0, hi0, n_adm, need, top_k):
    def halve(st):
        lo, hi, c_lo, c_hi, _ = st
        mid = 0.5 * (lo + hi)
        c_mid = count_ge(mid)
        ge = c_mid >= top_k
        lo2, c_lo2 = jnp.where(ge, mid, lo), jnp.where(ge, c_mid, c_lo)
        hi2, c_hi2 = jnp.where(ge, hi, mid), jnp.where(ge, c_hi, c_mid)
        r = top_k - c_hi2
        stop = (c_lo2 - c_hi2 == r) | (r == 1.0) | (mid <= lo) | (mid >= hi) | jnp.logical_not(need)
        return lo2, hi2, c_lo2, c_hi2, jnp.where(stop, 1.0, 0.0)

    def inner_cond(st):
        return (jnp.min(st[4]) < 0.5) & (st[5] < BISECTIONS_PER_ROUND)

    def inner_body(st):
        return halve(halve(st[:5])) + (st[5] + 2,)

    def outer_cond(st):
        return (jnp.min(st[4]) < 0.5) & (st[7] < MAX_ROUNDS)

    def outer_body(st):
        lo, hi, c_lo, c_hi, done = lax.while_loop(inner_cond, inner_body, st[:5] + (jnp.int32(0),))[:5]
        mn, mx = bracket_minmax(lo, hi)
        open_ = need & (done < 0.5)
        done = jnp.where(open_ & (mn == mx), 1.0, done)
        return jnp.where(open_, mn, lo), hi, c_lo, c_hi, done, mn, mx, st[7] + 1

    zero = jnp.zeros_like(lo0)
    lo, hi, c_lo, c_hi, _, mn, mx, _ = lax.while_loop(
        outer_cond, outer_body, (lo0, hi0, n_adm, zero, zero, zero, zero, jnp.int32(0)))
    r = top_k - c_hi
    thr = jnp.where(need, jnp.where(r == 1.0, mx, mn), NEG_SCORE)
    n_tie = jnp.where(need, jnp.where(c_lo - c_hi != r, r, BIG), 0.0)
    return thr, n_tie


def _attn_body(q_ref, qi_ref, wq_ref, kb_ref, vb_ref, kk_ref, o_ref,
               sc_ref, qia_ref, qa_ref, vt_ref, m_ref, acc_ref,
               da_ref, db_ref, sa_ref, sb_ref, *, top_k, t_real):
    j = pl.program_id(1)
    T = KEY_CHUNK
    Q = Q_TILE
    nchunks = (j + 1) * (Q // T)
    low = lax.broadcasted_iota(jnp.int32, (Q, LANES), 1) < HEAD_DIM

    @pl.when(j == 0)
    def _():
        def transpose_v(c, carry):
            blk = vb_ref[pl.ds(pl.multiple_of(c * T, T), T), :].astype(F32)
            for s in range(2):
                vt_ref[c, s, 0:LANES, :] = blk[:, s * LANES:(s + 1) * LANES].T.astype(BF16)
                vt_ref[c, s, LANES:LANES + SUM_ROWS, :] = jnp.ones((SUM_ROWS, T), BF16)
            return carry
        lax.fori_loop(0, vt_ref.shape[0], transpose_v, 0)

    wt = wq_ref[...].T
    qi = qi_ref[...]
    for h in range(N_IDX_HEADS):
        slab = qi[:, (h // 2) * LANES:(h // 2 + 1) * LANES]
        zero = jnp.zeros_like(slab)
        qia_ref[h * Q:(h + 1) * Q, :] = jnp.where(low, slab, zero) if h % 2 == 0 else jnp.where(low, zero, slab)
        qa_ref[h * Q:(h + 1) * Q, :] = q_ref[:, h * LANES:(h + 1) * LANES]

    G = KEY_GROUP
    ngroups = (nchunks + G - 1) // G
    last_chunk = vt_ref.shape[0] - 1

    def chunk_rows(ref, g, u, lanes):
        cc = jnp.minimum(g * G + u, last_chunk)
        return ref[pl.ds(pl.multiple_of(cc * T, T), T), lanes]

    def fold(x, op):
        return op(x.reshape(x.shape[0] // SUBLANES, SUBLANES, Q), axis=0)

    npairs = ngroups // 2
    odd_tail = ngroups % 2 == 1

    def idx_dots(dst, g):
        kk = jnp.concatenate([chunk_rows(kk_ref, g, u, slice(None)) for u in range(G)], axis=0)
        dst[...] = lax.dot_general(kk, qia_ref[...], _NT, preferred_element_type=F32)

    def idx_scores(src, g, carry):
        mn, mx = carry
        s = jnp.zeros((G * T, Q), F32)
        for h in range(N_IDX_HEADS):
            s = s + wt[IDX_DIM + h:IDX_DIM + h + 1, :] * jnp.maximum(src[:, h * Q:(h + 1) * Q], 0.0)
        kpos = g * (G * T) + lax.broadcasted_iota(jnp.int32, (G * T, Q), 0)
        adm = kpos <= j * Q + lax.broadcasted_iota(jnp.int32, (G * T, Q), 1)
        sc_ref[pl.ds(pl.multiple_of(g * (G * T), G * T), G * T), :] = jnp.where(adm, s, NEG_SCORE)
        return (jnp.minimum(mn, fold(jnp.where(adm, s, BIG), jnp.min)),
                jnp.maximum(mx, fold(jnp.where(adm, s, -BIG), jnp.max)))

    def phase_a(i, carry):
        idx_dots(db_ref, 2 * i + 1)
        carry = idx_scores(da_ref, 2 * i, carry)
        idx_dots(da_ref, 2 * i + 2)
        return idx_scores(db_ref, 2 * i + 1, carry)

    idx_dots(da_ref, 0)
    rng = lax.fori_loop(0, npairs, phase_a,
                        (jnp.full((SUBLANES, Q), BIG, F32), jnp.full((SUBLANES, Q), -BIG, F32)))
    mn, mx = lax.cond(odd_tail, lambda c: idx_scores(da_ref, ngroups - 1, c), lambda c: c, rng)

    def sc_group(g):
        return sc_ref[pl.ds(pl.multiple_of(g * (G * T), G * T), G * T), :]

    def count_ge(x):
        def step(g, acc):
            return acc + fold(jnp.where(sc_group(g) >= x, 1.0, 0.0), jnp.sum)
        return jnp.sum(lax.fori_loop(0, ngroups, step, jnp.zeros((SUBLANES, Q), F32)), axis=0, keepdims=True)

    def bracket_minmax(lo, hi):
        def step(g, carry):
            a, b = carry
            s = sc_group(g)
            inb = (s >= lo) & (s < hi)
            return (jnp.minimum(a, fold(jnp.where(inb, s, BIG), jnp.min)),
                    jnp.maximum(b, fold(jnp.where(inb, s, -BIG), jnp.max)))
        a, b = lax.fori_loop(0, ngroups, step,
                             (jnp.full((SUBLANES, Q), BIG, F32), jnp.full((SUBLANES, Q), -BIG, F32)))
        return jnp.min(a, axis=0, keepdims=True), jnp.max(b, axis=0, keepdims=True)

    qrow = j * Q + lax.broadcasted_iota(jnp.int32, (1, Q), 1)
    n_adm = (qrow + 1).astype(F32)
    need = (qrow + 1 > top_k) & (qrow < t_real)
    top = jnp.max(mx, axis=0, keepdims=True)
    thr, n_tie = _select_threshold(count_ge, bracket_minmax, jnp.min(mn, axis=0, keepdims=True),
                                   top + (jnp.abs(top) + 1.0), n_adm, need, float(top_k))

    m_ref[...] = jnp.full(m_ref.shape, NEG_SCORE, F32)
    acc_ref[...] = jnp.zeros(acc_ref.shape, F32)
    earlier = jnp.where(lax.broadcasted_iota(jnp.int32, (T, T), 1) < lax.broadcasted_iota(jnp.int32, (T, T), 0),
                        1.0, 0.0).astype(BF16)

    def qk_dots(dst, g):
        for sl in range(2):
            lanes = slice(sl * LANES, (sl + 1) * LANES)
            kc = jnp.concatenate([chunk_rows(kb_ref, g, u, lanes) for u in range(G)], axis=0)
            for hh in range(4):
                dst[sl, :, hh * Q:(hh + 1) * Q] = lax.dot_general(
                    kc, qa_ref[(sl * 4 + hh) * Q:(sl * 4 + hh + 1) * Q, :], _NT, preferred_element_type=F32)

    def softmax_pv(src, g, seen, issue_next):
        s = sc_group(g)
        eq = s == thr
        eqf = jnp.where(eq, 1.0, 0.0)
        eq_l = jnp.concatenate([eqf[u * T:(u + 1) * T] for u in range(G)], axis=1).astype(BF16)
        in_chunk = jnp.dot(earlier, eq_l, preferred_element_type=F32)
        ranks = []
        for u in range(G):
            ranks.append(in_chunk[:, u * Q:(u + 1) * Q] + seen)
            seen = seen + jnp.sum(eqf[u * T:(u + 1) * T], axis=0, keepdims=True)
        rank = jnp.concatenate(ranks, axis=0)
        bias = jnp.where(s > thr, 0.0, jnp.where(eq & (rank < n_tie), 0.0, NEG_BIAS))
        bias4 = jnp.concatenate([bias] * 4, axis=1)
        sts, m_news, alphas = [], [], []
        for sl in range(2):
            st = src[sl] + bias4
            m_old = m_ref[sl]
            m_new = jnp.maximum(m_old, jnp.max(st, axis=0, keepdims=True))
            m_ref[sl] = m_new
            sts.append(st)
            m_news.append(m_new)
            alphas.append(jnp.exp2(m_old - m_new))
        issue_next()
        for sl in range(2):
            p = jnp.exp2(sts[sl] - m_news[sl])
            vt =jnp.concatenate([vt_ref[jnp.minimum(g * G + u, last_chunk), sl] for u in range(G)], axis=1)
            acc_ref[sl] = alphas[sl] * acc_ref[sl] + jnp.dot(vt, p.astype(BF16), preferred_element_type=F32)
        return seen

    def phase_c(i, seen):
        seen = softmax_pv(sa_ref, 2 * i, seen, lambda: qk_dots(sb_ref, 2 * i + 1))
        return softmax_pv(sb_ref, 2 * i + 1, seen, lambda: qk_dots(sa_ref, 2 * i + 2))

    qk_dots(sa_ref, 0)
    seen = lax.fori_loop(0, npairs, phase_c, jnp.zeros((1, Q), F32))

    @pl.when(odd_tail)
    def _():
        softmax_pv(sa_ref, ngroups - 1, seen, lambda: None)

    for p in range(N_KV_HEADS):
        sl, hh = (2 * p) // 4, (2 * p) % 4
        out = acc_ref[sl, 0:LANES, :] / acc_ref[sl, LANES:LANES + 1, :]
        a0 = out[:, hh * Q:(hh + 1) * Q].T
        a1 = out[:, (hh + 1) * Q:(hh + 2) * Q].T
        if p % 2 == 0:
            slab = jnp.where(low, a0, pltpu.roll(a1, HEAD_DIM, 1))
        else:
            slab = jnp.where(low, pltpu.roll(a0, HEAD_DIM, 1), a1)
        o_ref[:, p * LANES:(p + 1) * LANES] = slab.astype(BF16)


def _attn_prompt(pj, batch, t_pad, t_real, top_k):
    nq = t_pad // Q_TILE
    nkc = t_pad // KEY_CHUNK
    grp_rows = KEY_GROUP * KEY_CHUNK
    n_grp = 2 * -(-(-(-nkc // KEY_GROUP)) // 2)
    qrow = lambda b, j: (b * nq + j, 0)
    seq = lambda b, j: (b, 0)
    return pl.pallas_call(
        functools.partial(_attn_body, top_k=top_k, t_real=t_real),
        grid=(batch, nq),
        in_specs=[
            pl.BlockSpec((Q_TILE, 1024), qrow),
            pl.BlockSpec((Q_TILE, 512), qrow),
            pl.BlockSpec((Q_TILE, LANES), qrow),
            pl.BlockSpec((t_pad, 256), seq, pipeline_mode=pl.Buffered(1)),
            pl.BlockSpec((t_pad, 256), seq, pipeline_mode=pl.Buffered(1)),
            pl.BlockSpec((t_pad, LANES), seq, pipeline_mode=pl.Buffered(1)),
        ],
        out_specs=pl.BlockSpec((Q_TILE, 512), qrow),
        out_shape=jax.ShapeDtypeStruct((batch * t_pad, 512), BF16),
        scratch_shapes=[
            pltpu.VMEM((n_grp * grp_rows, Q_TILE), F32),
            pltpu.VMEM((N_IDX_HEADS * Q_TILE, LANES), BF16),
            pltpu.VMEM((N_HEADS * Q_TILE, LANES), BF16),
            pltpu.VMEM((nkc, 2, LANES + SUM_ROWS, KEY_CHUNK), BF16),
            pltpu.VMEM((2, 1, 4 * Q_TILE), F32),
            pltpu.VMEM((2, LANES + SUM_ROWS, 4 * Q_TILE), F32),
            pltpu.VMEM((grp_rows, N_IDX_HEADS * Q_TILE), F32),
            pltpu.VMEM((grp_rows, N_IDX_HEADS * Q_TILE), F32),
            pltpu.VMEM((2, grp_rows, 4 * Q_TILE), F32),
            pltpu.VMEM((2, grp_rows, 4 * Q_TILE), F32),
        ],
        compiler_params=pltpu.CompilerParams(
            dimension_semantics=("arbitrary", "arbitrary"), vmem_limit_bytes=VMEM_LIMIT),
        name="attn_prompt",
    )(pj["q"], pj["qi"], pj["kiwi"], pj["kb"], pj["vb"], pj["kk"])


def _gla_body(rq_ref, f_ref, rv_ref, cum_ref, o_ref, sout_ref, st_ref, *, t_real):
    step = pl.program_id(1)
    n_steps = pl.num_programs(1)
    C = REC_CHUNK
    CH = o_ref.shape[0] // C
    n_lev = C.bit_length() - 1

    @pl.when(step == 0)
    def _():
        st_ref[...] = jnp.zeros(st_ref.shape, F32)

    trow = lax.broadcasted_iota(jnp.int32, (C, LANES), 0)
    ti = lax.broadcasted_iota(jnp.int32, (C, C), 0)
    si = lax.broadcasted_iota(jnp.int32, (C, C), 1)
    H = range(N_REC_HEADS)
    units = [(cc, h) for cc in range(CH) for h in H]
    valid = [(step * CH + cc) * C + trow < t_real for cc in range(CH)]

    def tile(ref, u):
        return ref[u[0] * C:(u[0] + 1) * C, u[1] * LANES:(u[1] + 1) * LANES]

    f = {u: tile(f_ref, u) for u in units}
    q = {u: tile(rq_ref, u) for u in units}
    vb = {u: tile(rv_ref, u).astype(BF16) for u in units}
    kk = {u: jnp.where(valid[u[0]], 1.0 - f[u], 0.0) for u in units}

    lf3 = jnp.concatenate([_split3(jnp.where(valid[u[0]], jnp.log(f[u]), 0.0)) for u in units], axis=1)
    cums = jnp.dot(cum_ref[...], lf3, preferred_element_type=F32)
    cum = {u: _join3(cums[:, i * 3 * LANES:(i + 1) * 3 * LANES]) for i, u in enumerate(units)}
    b = {u: cum[u][0:C] for u in units}

    def midpoint(u, lev):
        if lev < MATMUL_MID_LEVELS:
            return cum[u][(1 + lev) * C:(2 + lev) * C]
        half = 1 << lev
        return jnp.concatenate(
            [jnp.broadcast_to(b[u][s + half - 1:s + half, :], (2 * half, LANES)) for s in range(0, C, 2 * half)],
            axis=0)

    a = {u: jnp.where(ti == si, lax.dot_general(q[u].astype(BF16), kk[u].astype(BF16), _NT,
                                                preferred_element_type=F32), 0.0) for u in units}
    for lev in range(n_lev):
        half = 1 << lev
        up = (trow & half) != 0
        blk_mask = ((ti >> (lev + 1)) == (si >> (lev + 1))) & ((ti & half) != 0) & ((si & half) == 0)
        for u in units:
            bl = midpoint(u, lev)
            e = jnp.exp(jnp.where(up, b[u] - bl, bl - b[u]))
            qt = jnp.where(up, q[u] * e, 0.0).astype(BF16)
            kt = jnp.where(up, 0.0, kk[u] * e).astype(BF16)
            a[u] = a[u] + jnp.where(blk_mask, lax.dot_general(qt, kt, _NT, preferred_element_type=F32), 0.0)

    st = {h: st_ref[h] for h in H}
    for u in units:
        cc, h = u
        qe = (q[u] * jnp.exp(b[u])).astype(BF16)
        o = jnp.dot(a[u].astype(BF16), vb[u], preferred_element_type=F32)
        o_ref[cc * C:(cc + 1) * C, h * LANES:(h + 1) * LANES] = o + lax.dot_general(
            qe, st[h].astype(BF16), _NT, preferred_element_type=F32)
        b_last = b[u][C - 1:C, :]
        kd = (kk[u] * jnp.exp(b_last - b[u])).astype(BF16)
        st[h] = st[h] * jnp.exp(b_last) + lax.dot_general(vb[u], kd, _TN, preferred_element_type=F32)
    for h in H:
        st_ref[h] = st[h]

        @pl.when(step == n_steps - 1)
        def _(h=h):
            sout_ref[0, h] = st[h].T


def _gla_consts():
    C = REC_CHUNK
    t = jnp.arange(C)
    tri = t[None, :] <= t[:, None]
    blocks = [tri]
    for lev in range(MATMUL_MID_LEVELS):
        half = 1 << lev
        mid_row = (t >> (lev + 1)) * (2 * half) + half - 1
        blocks.append(t[None, :] <= mid_row[:, None])
    return jnp.concatenate(blocks, axis=0).astype(BF16)


def _gla_prompt(pj, batch, t_pad, t_real):
    rows = REC_CHUNK * REC_CHUNKS_PER_STEP
    nch = t_pad // rows
    cum = _gla_consts()
    row = lambda b, c: (b * nch + c, 0)
    const = lambda b, c: (0, 0)
    return pl.pallas_call(
        functools.partial(_gla_body, t_real=t_real),
        grid=(batch, nch),
        in_specs=[
            pl.BlockSpec((rows, 512), row),
            pl.BlockSpec((rows, 512), row),
            pl.BlockSpec((rows, 512), row),
            pl.BlockSpec(cum.shape, const),
        ],
        out_specs=[
            pl.BlockSpec((rows, 512), row),
            pl.BlockSpec((1, N_REC_HEADS, REC_DIM, REC_DIM), lambda b, c: (b, 0, 0, 0)),
        ],
        out_shape=[
            jax.ShapeDtypeStruct((batch * t_pad, 512), F32),
            jax.ShapeDtypeStruct((batch, N_REC_HEADS, REC_DIM, REC_DIM), F32),
        ],
        scratch_shapes=[pltpu.VMEM((N_REC_HEADS, REC_DIM, REC_DIM), F32)],
        compiler_params=pltpu.CompilerParams(
            dimension_semantics=("arbitrary", "arbitrary"), vmem_limit_bytes=VMEM_LIMIT),
        name="gla_prompt",
    )(pj["rq"], pj["f"], pj["rv"], cum)


def _gla_sample_body(rq_ref, f_ref, rv_ref, s0_ref, o_ref, sout_ref):
    b = pl.program_id(0)
    q = rq_ref[pl.ds(b, 1), :]
    f = f_ref[pl.ds(b, 1), :]
    v = rv_ref[pl.ds(b, 1), :]
    for h in range(N_REC_HEADS):
        sl = slice(h * LANES, (h + 1) * LANES)
        col = lambda r: jnp.broadcast_to(r[:, sl], (REC_DIM, LANES)).T
        s_new = col(f) * s0_ref[0, h] + col(1.0 - f) * v[:, sl]
        sout_ref[0, h] = s_new
        o_ref[0, :, sl] = jnp.sum(col(q) * s_new, axis=0, keepdims=True)


def _gla_sample(rq, f, rv, state):
    db = state.shape[0]
    full = lambda b: (0, 0)
    st = lambda b: (b, 0, 0, 0)
    o, s = pl.pallas_call(
        _gla_sample_body,
        grid=(db,),
        in_specs=[
            pl.BlockSpec(rq.shape, full), pl.BlockSpec(f.shape, full), pl.BlockSpec(rv.shape, full),
            pl.BlockSpec((1, N_REC_HEADS, REC_DIM, REC_DIM), st),
        ],
        out_specs=[
            pl.BlockSpec((1, 1, 512), lambda b: (b, 0, 0)),
            pl.BlockSpec((1, N_REC_HEADS, REC_DIM, REC_DIM), st),
        ],
        out_shape=[
            jax.ShapeDtypeStruct((db, 1, 512), F32),
            jax.ShapeDtypeStruct(state.shape, F32),
        ],
        compiler_params=pltpu.CompilerParams(dimension_semantics=("arbitrary",)),
        name="gla_sample",
    )(rq, f, rv, state)
    return o.reshape(db, 512), s


def _sample_select_body(pt_ref, *refs, n_pages, top_k):
    P = SELECT_PAGES_PER_STEP
    ki_refs = refs[0:P]
    qi_ref, w_ref, kin_ref, sel_ref, sc_ref = refs[P:]
    g = pl.program_id(1)
    n_groups = n_pages // P
    n_rows = sc_ref.shape[0]
    qi = qi_ref[0]
    w = w_ref[0]

    keys = jnp.concatenate([ki_refs[i][...].astype(BF16) for i in range(P)], axis=1)
    d = jnp.dot(qi, keys, preferred_element_type=F32)
    srow = jnp.sum(w * jnp.maximum(d, 0.0), axis=0, keepdims=True)
    sc_ref[pl.ds(pl.multiple_of(g * P, P), P), :] = jnp.concatenate(
        [srow[:, i * LANES:(i + 1) * LANES] for i in range(P)], axis=0)

    @pl.when(g == n_groups - 1)
    def _():
        d_new = jnp.sum(qi.astype(F32) * kin_ref[0].astype(F32), axis=1, keepdims=True)
        s_new = jnp.sum(w * jnp.maximum(d_new, 0.0), axis=0, keepdims=True)
        tail = lax.broadcasted_iota(jnp.int32, (n_rows - n_pages, LANES), 0) * LANES + \
            lax.broadcasted_iota(jnp.int32, (n_rows - n_pages, LANES), 1)
        sc_ref[n_pages:n_rows, :] = jnp.where(tail == 0, s_new, NEG_SCORE)

        s = sc_ref[...]
        adm = s > NEG_HALF
        top = jnp.max(s, keepdims=True)

        def count_ge(x):
            return jnp.sum(jnp.where(sc_ref[...] >= x, 1.0, 0.0), keepdims=True)

        def bracket_minmax(lo, hi):
            v = sc_ref[...]
            inb = (v >= lo) & (v < hi)
            return (jnp.min(jnp.where(inb, v, BIG), keepdims=True),
                    jnp.max(jnp.where(inb, v, -BIG), keepdims=True))

        n_adm = jnp.full((1, 1), float(n_pages * LANES + 1), F32)
        thr, n_tie = _select_threshold(count_ge, bracket_minmax, jnp.min(jnp.where(adm, s, BIG), keepdims=True),
                                       top + (jnp.abs(top) + 1.0), n_adm, n_adm > top_k, float(top_k))
        eqb = jnp.where(s == thr, 1.0, 0.0).astype(BF16)
        ri = lax.broadcasted_iota(jnp.int32, (LANES, LANES), 0)
        ci = lax.broadcasted_iota(jnp.int32, (LANES, LANES), 1)
        in_row = jnp.dot(eqb, jnp.where(ri < ci, 1.0, 0.0).astype(BF16), preferred_element_type=F32)
        row_tot = jnp.dot(eqb, jnp.ones((LANES, LANES), BF16), preferred_element_type=F32)
        rr = lax.broadcasted_iota(jnp.int32, (n_rows, n_rows), 0)
        rc = lax.broadcasted_iota(jnp.int32, (n_rows, n_rows), 1)
        before = jnp.dot(jnp.where(rc < rr, 1.0, 0.0).astype(BF16), row_tot.astype(BF16),
                         preferred_element_type=F32)
        rank = in_row + before
        keep = (s > thr) | ((s == thr) & (rank < n_tie))
        sel_ref[0] = jnp.where(keep & adm, 1.0, 0.0)


def _sample_attend_body(pt_ref, *refs, n_pages):
    P = PAGES_PER_STEP
    k_refs, v_refs = refs[0:P], refs[P:2 * P]
    sel_ref, q_ref, kn_ref, vn_ref, o_ref, m_ref, l_ref, acc_ref = refs[2 * P:]
    g = pl.program_id(1)
    n_groups = n_pages // P

    q_slab = q_ref[0]
    first = lax.broadcasted_iota(jnp.int32, q_slab.shape, 0) < N_HEADS // 2
    zero = jnp.zeros_like(q_slab)
    q = jnp.concatenate([jnp.where(first, q_slab, zero), jnp.where(first, zero, q_slab)], axis=1)

    @pl.when(g == 0)
    def _():
        s_own = jnp.sum(q.astype(F32) * kn_ref[0].astype(F32), axis=1, keepdims=True)
        own_sel = sel_ref[0, n_pages:n_pages + 1, 0:1]
        m_ref[...] = jnp.broadcast_to(s_own + jnp.where(own_sel > 0.5, 0.0, NEG_BIAS), m_ref.shape)
        l_ref[...] = jnp.ones(l_ref.shape, F32)
        acc_ref[...] = jnp.broadcast_to(vn_ref[0].astype(F32), acc_ref.shape)

    sel = sel_ref[0, pl.ds(pl.multiple_of(g * P, P), P), :]
    scores = []
    for i in range(P):
        s = jnp.dot(q, k_refs[i][...].astype(BF16), preferred_element_type=F32)
        scores.append(jnp.where(sel[i:i + 1, :] > 0.5, s, NEG_BIAS))
    s_all = jnp.concatenate(scores, axis=1)
    m_old = m_ref[...]
    m_new = jnp.maximum(m_old, jnp.max(s_all, axis=1, keepdims=True))
    alpha = jnp.exp2(m_old - m_new)
    p32 = jnp.exp2(s_all - m_new[:, 0:1])
    l_ref[...] = alpha * l_ref[...] + jnp.sum(p32, axis=1, keepdims=True)
    m_ref[...] = m_new
    p = p32.astype(BF16)
    pv = jnp.zeros(acc_ref.shape, F32)
    for i in range(P):
        pv = pv + lax.dot_general(p[:, i * LANES:(i + 1) * LANES], v_refs[i][...].astype(BF16), _NT,
                                  preferred_element_type=F32)
    acc_ref[...] = alpha[:, 0:1] * acc_ref[...] + pv

    @pl.when(g == n_groups - 1)
    def _():
        res = acc_ref[...] / l_ref[:, 0:1]
        for h in range(N_HEADS):
            o_ref[0, h:h + 1, :] = res[h:h + 1, (h // 2) * HEAD_DIM:(h // 2 + 1) * HEAD_DIM]


def _attn_sample(page_table, layer, cache_kit, cache_kt, cache_vt, qi, w, ki_new, q, k_new, v_new):
    db, n_pages = page_table.shape
    page = cache_kit.shape[3]
    kv_w = N_KV_HEADS * HEAD_DIM
    top_k = min(TOPK_MAX, (n_pages * page + 1) // 4)
    n_rows = -(-(n_pages + 1) // SUBLANES) * SUBLANES
    pt = page_table.reshape(-1)
    params = pltpu.CompilerParams(dimension_semantics=("arbitrary", "arbitrary"), vmem_limit_bytes=VMEM_LIMIT)

    def page_specs(rows, pages_per_step):
        def page_map(i):
            return lambda b, g, pt: (layer, pt[b * n_pages + g * pages_per_step + i], 0, 0)
        return [pl.BlockSpec((None, None, rows, page), page_map(i)) for i in range(pages_per_step)]

    per_b = lambda b, g, pt: (b, 0, 0)
    P = SELECT_PAGES_PER_STEP
    sel = pl.pallas_call(
        functools.partial(_sample_select_body, n_pages=n_pages, top_k=top_k),
        grid_spec=pltpu.PrefetchScalarGridSpec(
            num_scalar_prefetch=1,
            grid=(db, n_pages // P),
            in_specs=(page_specs(IDX_DIM, P)
                      + [pl.BlockSpec((1,) + a.shape[1:], per_b) for a in (qi, w, ki_new)]),
            out_specs=pl.BlockSpec((1, n_rows, LANES), per_b),
            scratch_shapes=[pltpu.VMEM((n_rows, LANES), F32)]),
        out_shape=jax.ShapeDtypeStruct((db, n_rows, LANES), F32),
        compiler_params=params,
        name="sample_select",
    )(pt, *([cache_kit] * P), qi, w, ki_new)

    P = PAGES_PER_STEP
    out = pl.pallas_call(
        functools.partial(_sample_attend_body, n_pages=n_pages),
        grid_spec=pltpu.PrefetchScalarGridSpec(
            num_scalar_prefetch=1,
            grid=(db, n_pages // P),
            in_specs=(page_specs(kv_w, P) + page_specs(kv_w, P)
                      + [pl.BlockSpec((1,) + a.shape[1:], per_b) for a in (sel, q, k_new, v_new)]),
            out_specs=pl.BlockSpec((1, N_HEADS, HEAD_DIM), per_b),
            scratch_shapes=[
                pltpu.VMEM((N_HEADS, LANES), F32),
                pltpu.VMEM((N_HEADS, LANES), F32),
                pltpu.VMEM((N_HEADS, kv_w), F32),
            ]),
        out_shape=jax.ShapeDtypeStruct((db, N_HEADS, HEAD_DIM), F32),
        compiler_params=params,
        name="sample_attend",
    )(pt, *([cache_kt] * P), *([cache_vt] * P), sel, q, k_new, v_new)
    return out.reshape(db, N_HEADS * HEAD_DIM)


def _back_body(x_ref, att_ref, atts_ref, ro_ref, ros_ref, gr_ref, ga_ref, gb_ref, grec_ref,
               wpa_ref, wpb_ref, wo_ref, gffn_ref, wgu_ref, wd_ref, y_ref, *, n_prompt_tiles):
    is_sample = pl.program_id(0) >= n_prompt_tiles
    ro = jnp.where(is_sample, ros_ref[...], ro_ref[...])
    att = jnp.where(is_sample, atts_ref[...], att_ref[...])
    gr = gr_ref[...]
    recs = []
    for h in range(N_REC_HEADS):
        r = ro[:, h * LANES:(h + 1) * LANES]
        r = r * lax.rsqrt(jnp.mean(r * r, axis=-1, keepdims=True) + EPS) * grec_ref[...]
        recs.append((r * gr[:, h * LANES:(h + 1) * LANES]).astype(BF16))
    rec = jnp.concatenate(recs, axis=1)
    a = jnp.dot(att, wpa_ref[...], preferred_element_type=F32)
    b = jnp.dot(rec, wpb_ref[...], preferred_element_type=F32)
    mix = (ga_ref[...] * a + gb_ref[...] * b).astype(BF16)
    y = x_ref[...] + jnp.dot(mix, wo_ref[...], preferred_element_type=F32)

    d_ff = wd_ref.shape[0]
    hb = (y * lax.rsqrt(jnp.mean(y * y, axis=-1, keepdims=True) + EPS) * gffn_ref[...]).astype(BF16)
    for c in range(d_ff // FFN_CHUNK):
        c0 = c * FFN_CHUNK
        gate = jnp.dot(hb, wgu_ref[:, c0:c0 + FFN_CHUNK], preferred_element_type=F32)
        up = jnp.dot(hb, wgu_ref[:, d_ff + c0:d_ff + c0 + FFN_CHUNK], preferred_element_type=F32)
        act = (gate * _sigmoid(gate) * up).astype(BF16)
        y = y + jnp.dot(act, wd_ref[c0:c0 + FFN_CHUNK, :], preferred_element_type=F32)
    y_ref[...] = y


def _back(x_all, att_p, att_s, ro_p, ro_s, gr, ga, gb, grec, wpa, wpb, wo, gffn, wgu, wd):
    n_rows, d = x_all.shape
    once = dict(pipeline_mode=pl.Buffered(1))
    n_prompt_tiles = att_p.shape[0] // TOK_TILE
    row = lambda i: (i, 0)
    prow = lambda i: (jnp.minimum(i, n_prompt_tiles - 1), 0)
    const = lambda i: (0, 0)
    return pl.pallas_call(
        functools.partial(_back_body, n_prompt_tiles=n_prompt_tiles),
        grid=(n_rows // TOK_TILE,),
        in_specs=[
            pl.BlockSpec((TOK_TILE, d), row),
            pl.BlockSpec((TOK_TILE, 512), prow), pl.BlockSpec((TOK_TILE, 512), const),
            pl.BlockSpec((TOK_TILE, 512), prow), pl.BlockSpec((TOK_TILE, 512), const),
            pl.BlockSpec((TOK_TILE, 512), row),
            pl.BlockSpec((TOK_TILE, d), row), pl.BlockSpec((TOK_TILE, d), row),
            pl.BlockSpec((1, LANES), const),
            pl.BlockSpec(wpa.shape, const, **once), pl.BlockSpec(wpb.shape, const, **once),
            pl.BlockSpec(wo.shape, const, **once),
            pl.BlockSpec((1, d), const),
            pl.BlockSpec(wgu.shape, const, **once), pl.BlockSpec(wd.shape, const, **once),
        ],
        out_specs=pl.BlockSpec((TOK_TILE, d), row),
        out_shape=jax.ShapeDtypeStruct((n_rows, d), F32),
        compiler_params=pltpu.CompilerParams(
            dimension_semantics=("arbitrary",), vmem_limit_bytes=VMEM_LIMIT),
        name="mixer_back_ffn",
    )(x_all, att_p, att_s, ro_p, ro_s, gr, ga, gb, grec, wpa, wpb, wo, gffn, wgu, wd)


def _rope_table(pos):
    half = ROT_DIM // 2
    inv = jnp.power(ROPE_THETA, -2.0 * jnp.arange(half, dtype=F32) / ROT_DIM)
    ang = pos.astype(F32)[:, None] * inv[None, :]
    cos, sin = jnp.cos(ang), jnp.sin(ang)
    n = pos.shape[0]
    one = jnp.ones((n, HEAD_DIM - ROT_DIM), F32)
    zero8 = jnp.zeros((n, half), F32)
    zero = jnp.zeros((n, HEAD_DIM - ROT_DIM), F32)
    c64 = jnp.concatenate([cos, cos, one], axis=1)
    up64 = jnp.concatenate([zero8, sin, zero], axis=1)
    dn64 = jnp.concatenate([-sin, zero8, zero], axis=1)
    return jnp.concatenate([c64, c64, up64, up64, dn64, dn64], axis=1)


def _pack_w_in(w):
    d = w.shape[0]
    cuts = (512, 768, 1024, 1536, 1600, 1608, 2120, 2632, 3144, 3656, 4680)
    q, k, v, qi, ki, wi, qr, fr, ir, gr, ga, gb = jnp.split(w, cuts, axis=1)
    pad = jnp.zeros((d, LANES - IDX_DIM - N_IDX_HEADS), w.dtype)
    return jnp.concatenate([q, k, v, qi, ki, wi, pad, qr, fr, ir, gr, ga, gb], axis=1).astype(BF16)


def kernel(x_prompt, x_sample, cache_k, cache_v, cache_idx_k, state_rec, page_table, meta_tokens,
           w_in, norm_mix, q_norm, k_norm, lb_raw, rec_norm, w_pa, w_pb, w_o, norm_ffn, w_gu, w_down):
    batch, seq, d = x_prompt.shape
    db = x_sample.shape[0]
    depth = w_in.shape[0]
    n_pages = page_table.shape[1]
    page = cache_k.shape[2]
    past_len = n_pages * page
    t_real = seq + N_META
    seq_align = math.lcm(TOK_TILE, Q_TILE, REC_CHUNK * REC_CHUNKS_PER_STEP)
    t_pad = -(-t_real // seq_align) * seq_align
    tiles_per_seq = t_pad // TOK_TILE
    n_prompt = batch * t_pad
    top_k = min(TOPK_MAX, seq // 4)

    sm = jax.nn.softmax(lb_raw.astype(F32), axis=0)
    lower_bounds = jnp.cumsum(sm, axis=0) - sm[0:1]

    seq_pad = jnp.zeros((t_pad - t_real, d), F32)
    pieces = []
    for b in range(batch):
        pieces += [meta_tokens.astype(F32), x_prompt[b], seq_pad]
    pieces += [x_sample.reshape(db, d), jnp.zeros((TOK_TILE - db, d), F32)]
    x_all = jnp.concatenate(pieces, axis=0)

    rope_tab = jnp.concatenate([
        _rope_table(jnp.arange(t_pad, dtype=jnp.int32)),
        _rope_table(jnp.full((TOK_TILE,), past_len, jnp.int32))], axis=0)

    n_pool = cache_k.shape[1]
    cache_kt = jnp.transpose(cache_k, (0, 1, 3, 4, 2)).reshape(depth, n_pool, N_KV_HEADS * HEAD_DIM, page)
    cache_vt = jnp.transpose(cache_v, (0, 1, 3, 4, 2)).reshape(depth, n_pool, N_KV_HEADS * HEAD_DIM, page)
    cache_kit = jnp.transpose(cache_idx_k, (0, 1, 3, 2))
    srows = slice(n_prompt, n_prompt + db)
    tile2 = lambda g_: jnp.concatenate([g_, g_]).reshape(1, LANES).astype(F32)

    pk, pv, pki, ps, sk, sv, ski, ss = [], [], [], [], [], [], [], []
    for l in range(depth):
        pj = _inproj(x_all, norm_mix[l].reshape(1, d).astype(F32), _pack_w_in(w_in[l]), rope_tab,
                     tile2(q_norm[l]), tile2(k_norm[l]), lower_bounds[l].reshape(1, -1),
                     tiles_per_seq, n_prompt // TOK_TILE)

        att_p = _attn_prompt(pj, batch, t_pad, t_real, top_k)
        ro_p, st_p = _gla_prompt(pj, batch, t_pad, t_real)

        q_s = pj["q"][srows].reshape(db, N_HEADS, LANES)
        qi_s = pj["qi"][srows].reshape(db, N_IDX_HEADS, IDX_DIM)
        w_s = pj["kiwi"][srows, IDX_DIM:IDX_DIM + N_IDX_HEADS].reshape(db, N_IDX_HEADS, 1)
        ki_s = pj["kk"][srows, 0:IDX_DIM].reshape(db, 1, IDX_DIM)
        kn_s = pj["kb"][srows].reshape(db, 1, N_KV_HEADS * HEAD_DIM)
        vn_s = pj["vb"][srows].reshape(db, 1, N_KV_HEADS * HEAD_DIM)
        att_s = _attn_sample(page_table, l, cache_kit, cache_kt, cache_vt, qi_s, w_s, ki_s, q_s, kn_s, vn_s)
        ro_s, st_s = _gla_sample(pj["rq"][srows], pj["f"][srows], pj["rv"][srows], state_rec[l])

        pad_s = lambda a: jnp.pad(a, ((0, TOK_TILE - db), (0, 0)))
        x_all = _back(x_all, att_p, pad_s(att_s.astype(BF16)), ro_p, pad_s(ro_s),
                      pj["gr"], pj["ga"], pj["gb"], rec_norm[l].reshape(1, LANES).astype(F32),
                      w_pa[l].astype(BF16), w_pb[l].astype(BF16), w_o[l].astype(BF16),
                      norm_ffn[l].reshape(1, d).astype(F32), w_gu[l].astype(BF16), w_down[l].astype(BF16))

        seq_view = lambda a, wd: a[:n_prompt].reshape(batch, t_pad, wd)[:, :t_real]
        pk.append(seq_view(pj["kf"], 256).reshape(batch, t_real, N_KV_HEADS, HEAD_DIM))
        pv.append(seq_view(pj["vf"], 256).reshape(batch, t_real, N_KV_HEADS, HEAD_DIM))
        pki.append(seq_view(pj["kiwi"], LANES)[..., :IDX_DIM])
        ps.append(st_p)
        sk.append(pj["kf"][srows].reshape(db, 1, N_KV_HEADS, HEAD_DIM))
        sv.append(pj["vf"][srows].reshape(db, 1, N_KV_HEADS, HEAD_DIM))
        ski.append(pj["kiwi"][srows, :IDX_DIM].reshape(db, 1, IDX_DIM))
        ss.append(st_s)

    y_prompt = jnp.stack([x_all[b * t_pad + N_META:b * t_pad + t_real] for b in range(batch)])
    y_sample = x_all[srows].reshape(db, 1, d)
    return (y_prompt, y_sample, jnp.stack(pk), jnp.stack(pv), jnp.stack(pki), jnp.stack(ps),
            jnp.stack(sk), jnp.stack(sv), jnp.stack(ski), jnp.stack(ss))
```

```python
import functools
import math

import jax
import jax.numpy as jnp
from jax import lax
from jax.experimental import pallas as pl
from jax.experimental.pallas import tpu as pltpu

F32 = jnp.float32
BF16 = jnp.bfloat16

N_META = 16
N_HEADS = 8
HEAD_DIM = 64
N_KV_HEADS = 4
ROT_DIM = 16
ROPE_THETA = 500000.0
N_IDX_HEADS = 8
IDX_DIM = 64
IDX_W_SCALE = (N_IDX_HEADS * IDX_DIM) ** -0.5
TOPK_MAX = 256
N_REC_HEADS = 4
REC_DIM = 128
EPS = 1e-6
Q_SCALE = HEAD_DIM ** -0.5 * 1.4426950408889634

LANES = 128
SUBLANES = 8

TOK_TILE = 384
Q_TILE = 384
KEY_CHUNK = 128
KEY_GROUP = 3
SUM_ROWS = 16
REC_CHUNK = 64
REC_CHUNKS_PER_STEP = 6
MATMUL_MID_LEVELS = 3
PAGES_PER_STEP = 32
SELECT_PAGES_PER_STEP = 64
FFN_CHUNK = 2816
VMEM_LIMIT = 56 * 1024 * 1024

NEG_SCORE = -3.0e38
NEG_HALF = -1.5e38
NEG_BIAS = -1.0e30
BIG = 3.0e38
BISECTIONS_PER_ROUND = 20
MAX_ROUNDS = 160

C_Q, C_K, C_V, C_QI, C_KIWI, C_RQ, C_FR, C_IR, C_GR, C_GA, C_GB, C_END = (
    0, 512, 768, 1024, 1536, 1664, 2176, 2688, 3200, 3712, 4736, 5760)

_NT = (((1,), (1,)), ((), ()))
_TN = (((0,), (0,)), ((), ()))


def _sigmoid(x):
    return 1.0 / (1.0 + jnp.exp(-x))


def _split3(x):
    hi = x.astype(BF16)
    r1 = x - hi.astype(F32)
    mid = r1.astype(BF16)
    lo = (r1 - mid.astype(F32)).astype(BF16)
    return jnp.concatenate([hi, mid, lo], axis=1)


def _join3(r):
    return r[:, 0:LANES] + r[:, LANES:2 * LANES] + r[:, 2 * LANES:3 * LANES]


def _inproj_body(x_ref, g_ref, w_ref, rope_ref, qn_ref, kn_ref, lb_ref,
                 q_ref, kf_ref, vf_ref, kb_ref, vb_ref, qi_ref, kiwi_ref, kk_ref,
                 rq_ref, f_ref, rv_ref, gr_ref, ga_ref, gb_ref):
    x = x_ref[...]
    h = x * lax.rsqrt(jnp.mean(x * x, axis=-1, keepdims=True) + EPS) * g_ref[...]
    hb = h.astype(BF16)
    rows = x.shape[0]
    lane = lax.broadcasted_iota(jnp.int32, (rows, LANES), 1)
    low = lane < HEAD_DIM
    cosv = rope_ref[:, 0:LANES]
    sin_up = rope_ref[:, LANES:2 * LANES]
    sin_dn = rope_ref[:, 2 * LANES:3 * LANES]

    def proj(c0, c1):
        return jnp.dot(hb, w_ref[:, c0:c1], preferred_element_type=F32)

    def rope(xs):
        return (xs * cosv + pltpu.roll(xs, ROT_DIM // 2, 1) * sin_up
                + pltpu.roll(xs, LANES - ROT_DIM // 2, 1) * sin_dn)

    def headnorm(xs, gain):
        sq = xs * xs
        s_lo = jnp.sum(jnp.where(low, sq, 0.0), axis=-1, keepdims=True)
        s_hi = jnp.sum(jnp.where(low, 0.0, sq), axis=-1, keepdims=True)
        ms = jnp.where(low, s_lo, s_hi) * (1.0 / HEAD_DIM)
        return xs * lax.rsqrt(ms + EPS) * gain

    qraw = proj(C_Q, C_K)
    for p in range(N_KV_HEADS):
        qs = rope(headnorm(qraw[:, p * LANES:(p + 1) * LANES], qn_ref[...])) * Q_SCALE
        qr = pltpu.roll(qs, HEAD_DIM, 1)
        if p % 2 == 0:
            h0, h1 = jnp.where(low, qs, 0.0), jnp.where(low, qr, 0.0)
        else:
            h0, h1 = jnp.where(low, 0.0, qr), jnp.where(low, 0.0, qs)
        q_ref[:, (2 * p) * LANES:(2 * p + 1) * LANES] = h0.astype(BF16)
        q_ref[:, (2 * p + 1) * LANES:(2 * p + 2) * LANES] = h1.astype(BF16)

    kraw = proj(C_K, C_V)
    for s in range(2):
        ks = rope(headnorm(kraw[:, s * LANES:(s + 1) * LANES], kn_ref[...]))
        kf_ref[:, s * LANES:(s + 1) * LANES] = ks
        kb_ref[:, s * LANES:(s + 1) * LANES] = ks.astype(BF16)

    vraw = proj(C_V, C_QI)
    vf_ref[...] = vraw
    vb_ref[...] = vraw.astype(BF16)

    qiraw = proj(C_QI, C_KIWI)
    for p in range(4):
        qi_ref[:, p * LANES:(p + 1) * LANES] = rope(qiraw[:, p * LANES:(p + 1) * LANES]).astype(BF16)

    kiwi = proj(C_KIWI, C_RQ)
    kir = rope(kiwi)
    kiwi_ref[...] = jnp.where(low, kir, kiwi * IDX_W_SCALE)
    kk_ref[...] = jnp.where(low, kir, pltpu.roll(kir, HEAD_DIM, 1)).astype(BF16)

    qr_ = proj(C_RQ, C_FR)
    rq_ref[...] = qr_ * _sigmoid(qr_)
    lb = lb_ref[...]
    f_ref[...] = lb + (1.0 - lb) * _sigmoid(proj(C_FR, C_IR))
    rv_ref[...] = proj(C_IR, C_GR)
    g_ = proj(C_GR, C_GA)
    gr_ref[...] = g_ * _sigmoid(g_)
    ga_ref[...] = _sigmoid(proj(C_GA, C_GB))
    gb_ref[...] = _sigmoid(proj(C_GB, C_END))


def _inproj(x_all, g, w, rope_tab, qn, kn, lb, tiles_per_seq, n_prompt_tiles):
    n_rows, d = x_all.shape
    n_tiles = n_rows // TOK_TILE
    row = lambda i: (i, 0)
    const = lambda i: (0, 0)
    rope_map = lambda i: (jnp.where(i < n_prompt_tiles, i % tiles_per_seq, tiles_per_seq), 0)
    widths = dict(q=(1024, BF16), kf=(256, F32), vf=(256, F32), kb=(256, BF16), vb=(256, BF16),
                  qi=(512, BF16), kiwi=(128, F32), kk=(128, BF16), rq=(512, F32), f=(512, F32),
                  rv=(512, F32), gr=(512, F32), ga=(1024, F32), gb=(1024, F32))
    out_shape = [jax.ShapeDtypeStruct((n_rows, wd), dt) for wd, dt in widths.values()]
    out_specs = [pl.BlockSpec((TOK_TILE, wd), row) for wd, _ in widths.values()]
    outs = pl.pallas_call(
        _inproj_body,
        grid=(n_tiles,),
        in_specs=[
            pl.BlockSpec((TOK_TILE, d), row),
            pl.BlockSpec((1, d), const),
            pl.BlockSpec((d, C_END), const, pipeline_mode=pl.Buffered(1)),
            pl.BlockSpec((TOK_TILE, 3 * LANES), rope_map),
            pl.BlockSpec((1, LANES), const),
            pl.BlockSpec((1, LANES), const),
            pl.BlockSpec((1, 512), const),
        ],
        out_specs=out_specs,
        out_shape=out_shape,
        compiler_params=pltpu.CompilerParams(
            dimension_semantics=("arbitrary",), vmem_limit_bytes=VMEM_LIMIT),
        name="inproj",
    )(x_all, g, w, rope_tab, qn, kn, lb)
    return dict(zip(widths.keys(), outs))


def _select_threshold(count_ge, bracket_minmax, lo0, hi0, n_adm, need, top_k):
    def halve(st):
        lo, hi, c_lo, c_hi, _ = st
        mid = 0.5 * (lo + hi)
        c_mid = count_ge(mid)
        ge = c_mid >= top_k
        lo2, c_lo2 = jnp.where(ge, mid, lo), jnp.where(ge, c_mid, c_lo)
        hi2, c_hi2 = jnp.where(ge, hi, mid), jnp.where(ge, c_hi, c_mid)
        r = top_k - c_hi2
        stop = (c_lo2 - c_hi2 == r) | (r == 1.0) | (mid <= lo) | (mid >= hi) | jnp.logical_not(need)
        return lo2, hi2, c_lo2, c_hi2, jnp.where(stop, 1.0, 0.0)

    def inner_cond(st):
        return (jnp.min(st[4]) < 0.5) & (st[5] < BISECTIONS_PER_ROUND)

    def inner_body(st):
        return halve(halve(st[:5])) + (st[5] + 2,)

    def outer_cond(st):
        return (jnp.min(st[4]) < 0.5) & (st[7] < MAX_ROUNDS)

    def outer_body(st):
        lo, hi, c_lo, c_hi, done = lax.while_loop(inner_cond, inner_body, st[:5] + (jnp.int32(0),))[:5]
        mn, mx = bracket_minmax(lo, hi)
        open_ = need & (done < 0.5)
        done = jnp.where(open_ & (mn == mx), 1.0, done)
        return jnp.where(open_, mn, lo), hi, c_lo, c_hi, done, mn, mx, st[7] + 1

    zero = jnp.zeros_like(lo0)
    lo, hi, c_lo, c_hi, _, mn, mx, _ = lax.while_loop(
        outer_cond, outer_body, (lo0, hi0, n_adm, zero, zero, zero, zero, jnp.int32(0)))
    r = top_k - c_hi
    thr = jnp.where(need, jnp.where(r == 1.0, mx, mn), NEG_SCORE)
    n_tie = jnp.where(need, jnp.where(c_lo - c_hi != r, r, BIG), 0.0)
    return thr, n_tie


def _attn_body(q_ref, qi_ref, wq_ref, kb_ref, vb_ref, kk_ref, o_ref,
               sc_ref, qia_ref, qa_ref, vt_ref, m_ref, acc_ref,
               da_ref, db_ref, sa_ref, sb_ref, *, top_k, t_real):
    j = pl.program_id(1)
    T = KEY_CHUNK
    Q = Q_TILE
    nchunks = (j + 1) * (Q // T)
    low = lax.broadcasted_iota(jnp.int32, (Q, LANES), 1) < HEAD_DIM

    @pl.when(j == 0)
    def _():
        def transpose_v(c, carry):
            blk = vb_ref[pl.ds(pl.multiple_of(c * T, T), T), :].astype(F32)
            for s in range(2):
                vt_ref[c, s, 0:LANES, :] = blk[:, s * LANES:(s + 1) * LANES].T.astype(BF16)
                vt_ref[c, s, LANES:LANES + SUM_ROWS, :] = jnp.ones((SUM_ROWS, T), BF16)
            return carry
        lax.fori_loop(0, vt_ref.shape[0], transpose_v, 0)

    wt = wq_ref[...].T
    qi = qi_ref[...]
    for h in range(N_IDX_HEADS):
        slab = qi[:, (h // 2) * LANES:(h // 2 + 1) * LANES]
        zero = jnp.zeros_like(slab)
        qia_ref[h * Q:(h + 1) * Q, :] = jnp.where(low, slab, zero) if h % 2 == 0 else jnp.where(low, zero, slab)
        qa_ref[h * Q:(h + 1) * Q, :] = q_ref[:, h * LANES:(h + 1) * LANES]

    G = KEY_GROUP
    ngroups = (nchunks + G - 1) // G
    last_chunk = vt_ref.shape[0] - 1

    def chunk_rows(ref, g, u, lanes):
        cc = jnp.minimum(g * G + u, last_chunk)
        return ref[pl.ds(pl.multiple_of(cc * T, T), T), lanes]

    def fold(x, op):
        return op(x.reshape(x.shape[0] // SUBLANES, SUBLANES, Q), axis=0)

    npairs = ngroups // 2
    odd_tail = ngroups % 2 == 1

    def idx_dots(dst, g):
        kk = jnp.concatenate([chunk_rows(kk_ref, g, u, slice(None)) for u in range(G)], axis=0)
        dst[...] = lax.dot_general(kk, qia_ref[...], _NT, preferred_element_type=F32)

    def idx_scores(src, g, carry):
        mn, mx = carry
        s = jnp.zeros((G * T, Q), F32)
        for h in range(N_IDX_HEADS):
            s = s + wt[IDX_DIM + h:IDX_DIM + h + 1, :] * jnp.maximum(src[:, h * Q:(h + 1) * Q], 0.0)
        kpos = g * (G * T) + lax.broadcasted_iota(jnp.int32, (G * T, Q), 0)
        adm = kpos <= j * Q + lax.broadcasted_iota(jnp.int32, (G * T, Q), 1)
        sc_ref[pl.ds(pl.multiple_of(g * (G * T), G * T), G * T), :] = jnp.where(adm, s, NEG_SCORE)
        return (jnp.minimum(mn, fold(jnp.where(adm, s, BIG), jnp.min)),
                jnp.maximum(mx, fold(jnp.where(adm, s, -BIG), jnp.max)))

    def phase_a(i, carry):
        idx_dots(db_ref, 2 * i + 1)
        carry = idx_scores(da_ref, 2 * i, carry)
        idx_dots(da_ref, 2 * i + 2)
        return idx_scores(db_ref, 2 * i + 1, carry)

    idx_dots(da_ref, 0)
    rng = lax.fori_loop(0, npairs, phase_a,
                        (jnp.full((SUBLANES, Q), BIG, F32), jnp.full((SUBLANES, Q), -BIG, F32)))
    mn, mx = lax.cond(odd_tail, lambda c: idx_scores(da_ref, ngroups - 1, c), lambda c: c, rng)

    def sc_group(g):
        return sc_ref[pl.ds(pl.multiple_of(g * (G * T), G * T), G * T), :]

    def count_ge(x):
        def step(g, acc):
            return acc + fold(jnp.where(sc_group(g) >= x, 1.0, 0.0), jnp.sum)
        return jnp.sum(lax.fori_loop(0, ngroups, step, jnp.zeros((SUBLANES, Q), F32)), axis=0, keepdims=True)

    def bracket_minmax(lo, hi):
        def step(g, carry):
            a, b = carry
            s = sc_group(g)
            inb = (s >= lo) & (s < hi)
            return (jnp.minimum(a, fold(jnp.where(inb, s, BIG), jnp.min)),
                    jnp.maximum(b, fold(jnp.where(inb, s, -BIG), jnp.max)))
        a, b = lax.fori_loop(0, ngroups, step,
                             (jnp.full((SUBLANES, Q), BIG, F32), jnp.full((SUBLANES, Q), -BIG, F32)))
        return jnp.min(a, axis=0, keepdims=True), jnp.max(b, axis=0, keepdims=True)

    qrow = j * Q + lax.broadcasted_iota(jnp.int32, (1, Q), 1)
    n_adm = (qrow + 1).astype(F32)
    need = (qrow + 1 > top_k) & (qrow < t_real)
    top = jnp.max(mx, axis=0, keepdims=True)
    thr, n_tie = _select_threshold(count_ge, bracket_minmax, jnp.min(mn, axis=0, keepdims=True),
                                   top + (jnp.abs(top) + 1.0), n_adm, need, float(top_k))

    m_ref[...] = jnp.full(m_ref.shape, NEG_SCORE, F32)
    acc_ref[...] = jnp.zeros(acc_ref.shape, F32)
    earlier = jnp.where(lax.broadcasted_iota(jnp.int32, (T, T), 1) < lax.broadcasted_iota(jnp.int32, (T, T), 0),
                        1.0, 0.0).astype(BF16)

    def qk_dots(dst, g):
        for sl in range(2):
            lanes = slice(sl * LANES, (sl + 1) * LANES)
            kc = jnp.concatenate([chunk_rows(kb_ref, g, u, lanes) for u in range(G)], axis=0)
            for hh in range(4):
                dst[sl, :, hh * Q:(hh + 1) * Q] = lax.dot_general(
                    kc, qa_ref[(sl * 4 + hh) * Q:(sl * 4 + hh + 1) * Q, :], _NT, preferred_element_type=F32)

    def softmax_pv(src, g, seen, issue_next):
        s = sc_group(g)
        eq = s == thr
        eqf = jnp.where(eq, 1.0, 0.0)
        eq_l = jnp.concatenate([eqf[u * T:(u + 1) * T] for u in range(G)], axis=1).astype(BF16)
        in_chunk = jnp.dot(earlier, eq_l, preferred_element_type=F32)
        ranks = []
        for u in range(G):
            ranks.append(in_chunk[:, u * Q:(u + 1) * Q] + seen)
            seen = seen + jnp.sum(eqf[u * T:(u + 1) * T], axis=0, keepdims=True)
        rank = jnp.concatenate(ranks, axis=0)
        bias = jnp.where(s > thr, 0.0, jnp.where(eq & (rank < n_tie), 0.0, NEG_BIAS))
        bias4 = jnp.concatenate([bias] * 4, axis=1)
        sts, m_news, alphas = [], [], []
        for sl in range(2):
            st = src[sl] + bias4
            m_old = m_ref[sl]
            m_new = jnp.maximum(m_old, jnp.max(st, axis=0, keepdims=True))
            m_ref[sl] = m_new
            sts.append(st)
            m_news.append(m_new)
            alphas.append(jnp.exp2(m_old - m_new))
        issue_next()
        for sl in range(2):
            p = jnp.exp2(sts[sl] - m_news[sl])
            vt =jnp.concatenate([vt_ref[jnp.minimum(g * G + u, last_chunk), sl] for u in range(G)], axis=1)
            acc_ref[sl] = alphas[sl] * acc_ref[sl] + jnp.dot(vt, p.astype(BF16), preferred_element_type=F32)
        return seen

    def phase_c(i, seen):
        seen = softmax_pv(sa_ref, 2 * i, seen, lambda: qk_dots(sb_ref, 2 * i + 1))
        return softmax_pv(sb_ref, 2 * i + 1, seen, lambda: qk_dots(sa_ref, 2 * i + 2))

    qk_dots(sa_ref, 0)
    seen = lax.fori_loop(0, npairs, phase_c, jnp.zeros((1, Q), F32))

    @pl.when(odd_tail)
    def _():
        softmax_pv(sa_ref, ngroups - 1, seen, lambda: None)

    for p in range(N_KV_HEADS):
        sl, hh = (2 * p) // 4, (2 * p) % 4
        out = acc_ref[sl, 0:LANES, :] / acc_ref[sl, LANES:LANES + 1, :]
        a0 = out[:, hh * Q:(hh + 1) * Q].T
        a1 = out[:, (hh + 1) * Q:(hh + 2) * Q].T
        if p % 2 == 0:
            slab = jnp.where(low, a0, pltpu.roll(a1, HEAD_DIM, 1))
        else:
            slab = jnp.where(low, pltpu.roll(a0, HEAD_DIM, 1), a1)
        o_ref[:, p * LANES:(p + 1) * LANES] = slab.astype(BF16)


def _attn_prompt(pj, batch, t_pad, t_real, top_k):
    nq = t_pad // Q_TILE
    nkc = t_pad // KEY_CHUNK
    grp_rows = KEY_GROUP * KEY_CHUNK
    n_grp = 2 * -(-(-(-nkc // KEY_GROUP)) // 2)
    qrow = lambda b, j: (b * nq + j, 0)
    seq = lambda b, j: (b, 0)
    return pl.pallas_call(
        functools.partial(_attn_body, top_k=top_k, t_real=t_real),
        grid=(batch, nq),
        in_specs=[
            pl.BlockSpec((Q_TILE, 1024), qrow),
            pl.BlockSpec((Q_TILE, 512), qrow),
            pl.BlockSpec((Q_TILE, LANES), qrow),
            pl.BlockSpec((t_pad, 256), seq, pipeline_mode=pl.Buffered(1)),
            pl.BlockSpec((t_pad, 256), seq, pipeline_mode=pl.Buffered(1)),
            pl.BlockSpec((t_pad, LANES), seq, pipeline_mode=pl.Buffered(1)),
        ],
        out_specs=pl.BlockSpec((Q_TILE, 512), qrow),
        out_shape=jax.ShapeDtypeStruct((batch * t_pad, 512), BF16),
        scratch_shapes=[
            pltpu.VMEM((n_grp * grp_rows, Q_TILE), F32),
            pltpu.VMEM((N_IDX_HEADS * Q_TILE, LANES), BF16),
            pltpu.VMEM((N_HEADS * Q_TILE, LANES), BF16),
            pltpu.VMEM((nkc, 2, LANES + SUM_ROWS, KEY_CHUNK), BF16),
            pltpu.VMEM((2, 1, 4 * Q_TILE), F32),
            pltpu.VMEM((2, LANES + SUM_ROWS, 4 * Q_TILE), F32),
            pltpu.VMEM((grp_rows, N_IDX_HEADS * Q_TILE), F32),
            pltpu.VMEM((grp_rows, N_IDX_HEADS * Q_TILE), F32),
            pltpu.VMEM((2, grp_rows, 4 * Q_TILE), F32),
            pltpu.VMEM((2, grp_rows, 4 * Q_TILE), F32),
        ],
        compiler_params=pltpu.CompilerParams(
            dimension_semantics=("arbitrary", "arbitrary"), vmem_limit_bytes=VMEM_LIMIT),
        name="attn_prompt",
    )(pj["q"], pj["qi"], pj["kiwi"], pj["kb"], pj["vb"], pj["kk"])


def _gla_body(rq_ref, f_ref, rv_ref, cum_ref, o_ref, sout_ref, st_ref, *, t_real):
    step = pl.program_id(1)
    n_steps = pl.num_programs(1)
    C = REC_CHUNK
    CH = o_ref.shape[0] // C
    n_lev = C.bit_length() - 1

    @pl.when(step == 0)
    def _():
        st_ref[...] = jnp.zeros(st_ref.shape, F32)

    trow = lax.broadcasted_iota(jnp.int32, (C, LANES), 0)
    ti = lax.broadcasted_iota(jnp.int32, (C, C), 0)
    si = lax.broadcasted_iota(jnp.int32, (C, C), 1)
    H = range(N_REC_HEADS)
    units = [(cc, h) for cc in range(CH) for h in H]
    valid = [(step * CH + cc) * C + trow < t_real for cc in range(CH)]

    def tile(ref, u):
        return ref[u[0] * C:(u[0] + 1) * C, u[1] * LANES:(u[1] + 1) * LANES]

    f = {u: tile(f_ref, u) for u in units}
    q = {u: tile(rq_ref, u) for u in units}
    vb = {u: tile(rv_ref, u).astype(BF16) for u in units}
    kk = {u: jnp.where(valid[u[0]], 1.0 - f[u], 0.0) for u in units}

    lf3 = jnp.concatenate([_split3(jnp.where(valid[u[0]], jnp.log(f[u]), 0.0)) for u in units], axis=1)
    cums = jnp.dot(cum_ref[...], lf3, preferred_element_type=F32)
    cum = {u: _join3(cums[:, i * 3 * LANES:(i + 1) * 3 * LANES]) for i, u in enumerate(units)}
    b = {u: cum[u][0:C] for u in units}

    def midpoint(u, lev):
        if lev < MATMUL_MID_LEVELS:
            return cum[u][(1 + lev) * C:(2 + lev) * C]
        half = 1 << lev
        return jnp.concatenate(
            [jnp.broadcast_to(b[u][s + half - 1:s + half, :], (2 * half, LANES)) for s in range(0, C, 2 * half)],
            axis=0)

    a = {u: jnp.where(ti == si, lax.dot_general(q[u].astype(BF16), kk[u].astype(BF16), _NT,
                                                preferred_element_type=F32), 0.0) for u in units}
    for lev in range(n_lev):
        half = 1 << lev
        up = (trow & half) != 0
        blk_mask = ((ti >> (lev + 1)) == (si >> (lev + 1))) & ((ti & half) != 0) & ((si & half) == 0)
        for u in units:
            bl = midpoint(u, lev)
            e = jnp.exp(jnp.where(up, b[u] - bl, bl - b[u]))
            qt = jnp.where(up, q[u] * e, 0.0).astype(BF16)
            kt = jnp.where(up, 0.0, kk[u] * e).astype(BF16)
            a[u] = a[u] + jnp.where(blk_mask, lax.dot_general(qt, kt, _NT, preferred_element_type=F32), 0.0)

    st = {h: st_ref[h] for h in H}
    for u in units:
        cc, h = u
        qe = (q[u] * jnp.exp(b[u])).astype(BF16)
        o = jnp.dot(a[u].astype(BF16), vb[u], preferred_element_type=F32)
        o_ref[cc * C:(cc + 1) * C, h * LANES:(h + 1) * LANES] = o + lax.dot_general(
            qe, st[h].astype(BF16), _NT, preferred_element_type=F32)
        b_last = b[u][C - 1:C, :]
        kd = (kk[u] * jnp.exp(b_last - b[u])).astype(BF16)
        st[h] = st[h] * jnp.exp(b_last) + lax.dot_general(vb[u], kd, _TN, preferred_element_type=F32)
    for h in H:
        st_ref[h] = st[h]

        @pl.when(step == n_steps - 1)
        def _(h=h):
            sout_ref[0, h] = st[h].T


def _gla_consts():
    C = REC_CHUNK
    t = jnp.arange(C)
    tri = t[None, :] <= t[:, None]
    blocks = [tri]
    for lev in range(MATMUL_MID_LEVELS):
        half = 1 << lev
        mid_row = (t >> (lev + 1)) * (2 * half) + half - 1
        blocks.append(t[None, :] <= mid_row[:, None])
    return jnp.concatenate(blocks, axis=0).astype(BF16)


def _gla_prompt(pj, batch, t_pad, t_real):
    rows = REC_CHUNK * REC_CHUNKS_PER_STEP
    nch = t_pad // rows
    cum = _gla_consts()
    row = lambda b, c: (b * nch + c, 0)
    const = lambda b, c: (0, 0)
    return pl.pallas_call(
        functools.partial(_gla_body, t_real=t_real),
        grid=(batch, nch),
        in_specs=[
            pl.BlockSpec((rows, 512), row),
            pl.BlockSpec((rows, 512), row),
            pl.BlockSpec((rows, 512), row),
            pl.BlockSpec(cum.shape, const),
        ],
        out_specs=[
            pl.BlockSpec((rows, 512), row),
            pl.BlockSpec((1, N_REC_HEADS, REC_DIM, REC_DIM), lambda b, c: (b, 0, 0, 0)),
        ],
        out_shape=[
            jax.ShapeDtypeStruct((batch * t_pad, 512), F32),
            jax.ShapeDtypeStruct((batch, N_REC_HEADS, REC_DIM, REC_DIM), F32),
        ],
        scratch_shapes=[pltpu.VMEM((N_REC_HEADS, REC_DIM, REC_DIM), F32)],
        compiler_params=pltpu.CompilerParams(
            dimension_semantics=("arbitrary", "arbitrary"), vmem_limit_bytes=VMEM_LIMIT),
        name="gla_prompt",
    )(pj["rq"], pj["f"], pj["rv"], cum)


def _gla_sample_body(rq_ref, f_ref, rv_ref, s0_ref, o_ref, sout_ref):
    b = pl.program_id(0)
    q = rq_ref[pl.ds(b, 1), :]
    f = f_ref[pl.ds(b, 1), :]
    v = rv_ref[pl.ds(b, 1), :]
    for h in range(N_REC_HEADS):
        sl = slice(h * LANES, (h + 1) * LANES)
        col = lambda r: jnp.broadcast_to(r[:, sl], (REC_DIM, LANES)).T
        s_new = col(f) * s0_ref[0, h] + col(1.0 - f) * v[:, sl]
        sout_ref[0, h] = s_new
        o_ref[0, :, sl] = jnp.sum(col(q) * s_new, axis=0, keepdims=True)


def _gla_sample(rq, f, rv, state):
    db = state.shape[0]
    full = lambda b: (0, 0)
    st = lambda b: (b, 0, 0, 0)
    o, s = pl.pallas_call(
        _gla_sample_body,
        grid=(db,),
        in_specs=[
            pl.BlockSpec(rq.shape, full), pl.BlockSpec(f.shape, full), pl.BlockSpec(rv.shape, full),
            pl.BlockSpec((1, N_REC_HEADS, REC_DIM, REC_DIM), st),
        ],
        out_specs=[
            pl.BlockSpec((1, 1, 512), lambda b: (b, 0, 0)),
            pl.BlockSpec((1, N_REC_HEADS, REC_DIM, REC_DIM), st),
        ],
        out_shape=[
            jax.ShapeDtypeStruct((db, 1, 512), F32),
            jax.ShapeDtypeStruct(state.shape, F32),
        ],
        compiler_params=pltpu.CompilerParams(dimension_semantics=("arbitrary",)),
        name="gla_sample",
    )(rq, f, rv, state)
    return o.reshape(db, 512), s


def _sample_select_body(pt_ref, *refs, n_pages, top_k):
    P = SELECT_PAGES_PER_STEP
    ki_refs = refs[0:P]
    qi_ref, w_ref, kin_ref, sel_ref, sc_ref = refs[P:]
    g = pl.program_id(1)
    n_groups = n_pages // P
    n_rows = sc_ref.shape[0]
    qi = qi_ref[0]
    w = w_ref[0]

    keys = jnp.concatenate([ki_refs[i][...].astype(BF16) for i in range(P)], axis=1)
    d = jnp.dot(qi, keys, preferred_element_type=F32)
    srow = jnp.sum(w * jnp.maximum(d, 0.0), axis=0, keepdims=True)
    sc_ref[pl.ds(pl.multiple_of(g * P, P), P), :] = jnp.concatenate(
        [srow[:, i * LANES:(i + 1) * LANES] for i in range(P)], axis=0)

    @pl.when(g == n_groups - 1)
    def _():
        d_new = jnp.sum(qi.astype(F32) * kin_ref[0].astype(F32), axis=1, keepdims=True)
        s_new = jnp.sum(w * jnp.maximum(d_new, 0.0), axis=0, keepdims=True)
        tail = lax.broadcasted_iota(jnp.int32, (n_rows - n_pages, LANES), 0) * LANES + \
            lax.broadcasted_iota(jnp.int32, (n_rows - n_pages, LANES), 1)
        sc_ref[n_pages:n_rows, :] = jnp.where(tail == 0, s_new, NEG_SCORE)

        s = sc_ref[...]
        adm = s > NEG_HALF
        top = jnp.max(s, keepdims=True)

        def count_ge(x):
            return jnp.sum(jnp.where(sc_ref[...] >= x, 1.0, 0.0), keepdims=True)

        def bracket_minmax(lo, hi):
            v = sc_ref[...]
            inb = (v >= lo) & (v < hi)
            return (jnp.min(jnp.where(inb, v, BIG), keepdims=True),
                    jnp.max(jnp.where(inb, v, -BIG), keepdims=True))

        n_adm = jnp.full((1, 1), float(n_pages * LANES + 1), F32)
        thr, n_tie = _select_threshold(count_ge, bracket_minmax, jnp.min(jnp.where(adm, s, BIG), keepdims=True),
                                       top + (jnp.abs(top) + 1.0), n_adm, n_adm > top_k, float(top_k))
        eqb = jnp.where(s == thr, 1.0, 0.0).astype(BF16)
        ri = lax.broadcasted_iota(jnp.int32, (LANES, LANES), 0)
        ci = lax.broadcasted_iota(jnp.int32, (LANES, LANES), 1)
        in_row = jnp.dot(eqb, jnp.where(ri < ci, 1.0, 0.0).astype(BF16), preferred_element_type=F32)
        row_tot = jnp.dot(eqb, jnp.ones((LANES, LANES), BF16), preferred_element_type=F32)
        rr = lax.broadcasted_iota(jnp.int32, (n_rows, n_rows), 0)
        rc = lax.broadcasted_iota(jnp.int32, (n_rows, n_rows), 1)
        before = jnp.dot(jnp.where(rc < rr, 1.0, 0.0).astype(BF16), row_tot.astype(BF16),
                         preferred_element_type=F32)
        rank = in_row + before
        keep = (s > thr) | ((s == thr) & (rank < n_tie))
        sel_ref[0] = jnp.where(keep & adm, 1.0, 0.0)


def _sample_attend_body(pt_ref, *refs, n_pages):
    P = PAGES_PER_STEP
    k_refs, v_refs = refs[0:P], refs[P:2 * P]
    sel_ref, q_ref, kn_ref, vn_ref, o_ref, m_ref, l_ref, acc_ref = refs[2 * P:]
    g = pl.program_id(1)
    n_groups = n_pages // P

    q_slab = q_ref[0]
    first = lax.broadcasted_iota(jnp.int32, q_slab.shape, 0) < N_HEADS // 2
    zero = jnp.zeros_like(q_slab)
    q = jnp.concatenate([jnp.where(first, q_slab, zero), jnp.where(first, zero, q_slab)], axis=1)

    @pl.when(g == 0)
    def _():
        s_own = jnp.sum(q.astype(F32) * kn_ref[0].astype(F32), axis=1, keepdims=True)
        own_sel = sel_ref[0, n_pages:n_pages + 1, 0:1]
        m_ref[...] = jnp.broadcast_to(s_own + jnp.where(own_sel > 0.5, 0.0, NEG_BIAS), m_ref.shape)
        l_ref[...] = jnp.ones(l_ref.shape, F32)
        acc_ref[...] = jnp.broadcast_to(vn_ref[0].astype(F32), acc_ref.shape)

    sel = sel_ref[0, pl.ds(pl.multiple_of(g * P, P), P), :]
    scores = []
    for i in range(P):
        s = jnp.dot(q, k_refs[i][...].astype(BF16), preferred_element_type=F32)
        scores.append(jnp.where(sel[i:i + 1, :] > 0.5, s, NEG_BIAS))
    s_all = jnp.concatenate(scores, axis=1)
    m_old = m_ref[...]
    m_new = jnp.maximum(m_old, jnp.max(s_all, axis=1, keepdims=True))
    alpha = jnp.exp2(m_old - m_new)
    p32 = jnp.exp2(s_all - m_new[:, 0:1])
    l_ref[...] = alpha * l_ref[...] + jnp.sum(p32, axis=1, keepdims=True)
    m_ref[...] = m_new
    p = p32.astype(BF16)
    pv = jnp.zeros(acc_ref.shape, F32)
    for i in range(P):
        pv = pv + lax.dot_general(p[:, i * LANES:(i + 1) * LANES], v_refs[i][...].astype(BF16), _NT,
                                  preferred_element_type=F32)
    acc_ref[...] = alpha[:, 0:1] * acc_ref[...] + pv

    @pl.when(g == n_groups - 1)
    def _():
        res = acc_ref[...] / l_ref[:, 0:1]
        for h in range(N_HEADS):
            o_ref[0, h:h + 1, :] = res[h:h + 1, (h // 2) * HEAD_DIM:(h // 2 + 1) * HEAD_DIM]


def _attn_sample(page_table, layer, cache_kit, cache_kt, cache_vt, qi, w, ki_new, q, k_new, v_new):
    db, n_pages = page_table.shape
    page = cache_kit.shape[3]
    kv_w = N_KV_HEADS * HEAD_DIM
    top_k = min(TOPK_MAX, (n_pages * page + 1) // 4)
    n_rows = -(-(n_pages + 1) // SUBLANES) * SUBLANES
    pt = page_table.reshape(-1)
    params = pltpu.CompilerParams(dimension_semantics=("arbitrary", "arbitrary"), vmem_limit_bytes=VMEM_LIMIT)

    def page_specs(rows, pages_per_step):
        def page_map(i):
            return lambda b, g, pt: (layer, pt[b * n_pages + g * pages_per_step + i], 0, 0)
        return [pl.BlockSpec((None, None, rows, page), page_map(i)) for i in range(pages_per_step)]

    per_b = lambda b, g, pt: (b, 0, 0)
    P = SELECT_PAGES_PER_STEP
    sel = pl.pallas_call(
        functools.partial(_sample_select_body, n_pages=n_pages, top_k=top_k),
        grid_spec=pltpu.PrefetchScalarGridSpec(
            num_scalar_prefetch=1,
            grid=(db, n_pages // P),
            in_specs=(page_specs(IDX_DIM, P)
                      + [pl.BlockSpec((1,) + a.shape[1:], per_b) for a in (qi, w, ki_new)]),
            out_specs=pl.BlockSpec((1, n_rows, LANES), per_b),
            scratch_shapes=[pltpu.VMEM((n_rows, LANES), F32)]),
        out_shape=jax.ShapeDtypeStruct((db, n_rows, LANES), F32),
        compiler_params=params,
        name="sample_select",
    )(pt, *([cache_kit] * P), qi, w, ki_new)

    P = PAGES_PER_STEP
    out = pl.pallas_call(
        functools.partial(_sample_attend_body, n_pages=n_pages),
        grid_spec=pltpu.PrefetchScalarGridSpec(
            num_scalar_prefetch=1,
            grid=(db, n_pages // P),
            in_specs=(page_specs(kv_w, P) + page_specs(kv_w, P)
                      + [pl.BlockSpec((1,) + a.shape[1:], per_b) for a in (sel, q, k_new, v_new)]),
            out_specs=pl.BlockSpec((1, N_HEADS, HEAD_DIM), per_b),
            scratch_shapes=[
                pltpu.VMEM((N_HEADS, LANES), F32),
                pltpu.VMEM((N_HEADS, LANES), F32),
                pltpu.VMEM((N_HEADS, kv_w), F32),
            ]),
        out_shape=jax.ShapeDtypeStruct((db, N_HEADS, HEAD_DIM), F32),
        compiler_params=params,
        name="sample_attend",
    )(pt, *([cache_kt] * P), *([cache_vt] * P), sel, q, k_new, v_new)
    return out.reshape(db, N_HEADS * HEAD_DIM)


def _back_body(x_ref, att_ref, atts_ref, ro_ref, ros_ref, gr_ref, ga_ref, gb_ref, grec_ref,
               wpa_ref, wpb_ref, wo_ref, gffn_ref, wgu_ref, wd_ref, y_ref, *, n_prompt_tiles):
    is_sample = pl.program_id(0) >= n_prompt_tiles
    ro = jnp.where(is_sample, ros_ref[...], ro_ref[...])
    att = jnp.where(is_sample, atts_ref[...], att_ref[...])
    gr = gr_ref[...]
    recs = []
    for h in range(N_REC_HEADS):
        r = ro[:, h * LANES:(h + 1) * LANES]
        r = r * lax.rsqrt(jnp.mean(r * r, axis=-1, keepdims=True) + EPS) * grec_ref[...]
        recs.append((r * gr[:, h * LANES:(h + 1) * LANES]).astype(BF16))
    rec = jnp.concatenate(recs, axis=1)
    a = jnp.dot(att, wpa_ref[...], preferred_element_type=F32)
    b = jnp.dot(rec, wpb_ref[...], preferred_element_type=F32)
    mix = (ga_ref[...] * a + gb_ref[...] * b).astype(BF16)
    y = x_ref[...] + jnp.dot(mix, wo_ref[...], preferred_element_type=F32)

    d_ff = wd_ref.shape[0]
    hb = (y * lax.rsqrt(jnp.mean(y * y, axis=-1, keepdims=True) + EPS) * gffn_ref[...]).astype(BF16)
    for c in range(d_ff // FFN_CHUNK):
        c0 = c * FFN_CHUNK
        gate = jnp.dot(hb, wgu_ref[:, c0:c0 + FFN_CHUNK], preferred_element_type=F32)
        up = jnp.dot(hb, wgu_ref[:, d_ff + c0:d_ff + c0 + FFN_CHUNK], preferred_element_type=F32)
        act = (gate * _sigmoid(gate) * up).astype(BF16)
        y = y + jnp.dot(act, wd_ref[c0:c0 + FFN_CHUNK, :], preferred_element_type=F32)
    y_ref[...] = y


def _back(x_all, att_p, att_s, ro_p, ro_s, gr, ga, gb, grec, wpa, wpb, wo, gffn, wgu, wd):
    n_rows, d = x_all.shape
    once = dict(pipeline_mode=pl.Buffered(1))
    n_prompt_tiles = att_p.shape[0] // TOK_TILE
    row = lambda i: (i, 0)
    prow = lambda i: (jnp.minimum(i, n_prompt_tiles - 1), 0)
    const = lambda i: (0, 0)
    return pl.pallas_call(
        functools.partial(_back_body, n_prompt_tiles=n_prompt_tiles),
        grid=(n_rows // TOK_TILE,),
        in_specs=[
            pl.BlockSpec((TOK_TILE, d), row),
            pl.BlockSpec((TOK_TILE, 512), prow), pl.BlockSpec((TOK_TILE, 512), const),
            pl.BlockSpec((TOK_TILE, 512), prow), pl.BlockSpec((TOK_TILE, 512), const),
            pl.BlockSpec((TOK_TILE, 512), row),
            pl.BlockSpec((TOK_TILE, d), row), pl.BlockSpec((TOK_TILE, d), row),
            pl.BlockSpec((1, LANES), const),
            pl.BlockSpec(wpa.shape, const, **once), pl.BlockSpec(wpb.shape, const, **once),
            pl.BlockSpec(wo.shape, const, **once),
            pl.BlockSpec((1, d), const),
            pl.BlockSpec(wgu.shape, const, **once), pl.BlockSpec(wd.shape, const, **once),
        ],
        out_specs=pl.BlockSpec((TOK_TILE, d), row),
        out_shape=jax.ShapeDtypeStruct((n_rows, d), F32),
        compiler_params=pltpu.CompilerParams(
            dimension_semantics=("arbitrary",), vmem_limit_bytes=VMEM_LIMIT),
        name="mixer_back_ffn",
    )(x_all, att_p, att_s, ro_p, ro_s, gr, ga, gb, grec, wpa, wpb, wo, gffn, wgu, wd)


def _rope_table(pos):
    half = ROT_DIM // 2
    inv = jnp.power(ROPE_THETA, -2.0 * jnp.arange(half, dtype=F32) / ROT_DIM)
    ang = pos.astype(F32)[:, None] * inv[None, :]
    cos, sin = jnp.cos(ang), jnp.sin(ang)
    n = pos.shape[0]
    one = jnp.ones((n, HEAD_DIM - ROT_DIM), F32)
    zero8 = jnp.zeros((n, half), F32)
    zero = jnp.zeros((n, HEAD_DIM - ROT_DIM), F32)
    c64 = jnp.concatenate([cos, cos, one], axis=1)
    up64 = jnp.concatenate([zero8, sin, zero], axis=1)
    dn64 = jnp.concatenate([-sin, zero8, zero], axis=1)
    return jnp.concatenate([c64, c64, up64, up64, dn64, dn64], axis=1)


def _pack_w_in(w):
    d = w.shape[0]
    cuts = (512, 768, 1024, 1536, 1600, 1608, 2120, 2632, 3144, 3656, 4680)
    q, k, v, qi, ki, wi, qr, fr, ir, gr, ga, gb = jnp.split(w, cuts, axis=1)
    pad = jnp.zeros((d, LANES - IDX_DIM - N_IDX_HEADS), w.dtype)
    return jnp.concatenate([q, k, v, qi, ki, wi, pad, qr, fr, ir, gr, ga, gb], axis=1).astype(BF16)


def kernel(x_prompt, x_sample, cache_k, cache_v, cache_idx_k, state_rec, page_table, meta_tokens,
           w_in, norm_mix, q_norm, k_norm, lb_raw, rec_norm, w_pa, w_pb, w_o, norm_ffn, w_gu, w_down):
    batch, seq, d = x_prompt.shape
    db = x_sample.shape[0]
    depth = w_in.shape[0]
    n_pages = page_table.shape[1]
    page = cache_k.shape[2]
    past_len = n_pages * page
    t_real = seq + N_META
    seq_align = math.lcm(TOK_TILE, Q_TILE, REC_CHUNK * REC_CHUNKS_PER_STEP)
    t_pad = -(-t_real // seq_align) * seq_align
    tiles_per_seq = t_pad // TOK_TILE
    n_prompt = batch * t_pad
    top_k = min(TOPK_MAX, seq // 4)

    sm = jax.nn.softmax(lb_raw.astype(F32), axis=0)
    lower_bounds = jnp.cumsum(sm, axis=0) - sm[0:1]

    seq_pad = jnp.zeros((t_pad - t_real, d), F32)
    pieces = []
    for b in range(batch):
        pieces += [meta_tokens.astype(F32), x_prompt[b], seq_pad]
    pieces += [x_sample.reshape(db, d), jnp.zeros((TOK_TILE - db, d), F32)]
    x_all = jnp.concatenate(pieces, axis=0)

    rope_tab = jnp.concatenate([
        _rope_table(jnp.arange(t_pad, dtype=jnp.int32)),
        _rope_table(jnp.full((TOK_TILE,), past_len, jnp.int32))], axis=0)

    n_pool = cache_k.shape[1]
    cache_kt = jnp.transpose(cache_k, (0, 1, 3, 4, 2)).reshape(depth, n_pool, N_KV_HEADS * HEAD_DIM, page)
    cache_vt = jnp.transpose(cache_v, (0, 1, 3, 4, 2)).reshape(depth, n_pool, N_KV_HEADS * HEAD_DIM, page)
    cache_kit = jnp.transpose(cache_idx_k, (0, 1, 3, 2))
    srows = slice(n_prompt, n_prompt + db)
    tile2 = lambda g_: jnp.concatenate([g_, g_]).reshape(1, LANES).astype(F32)

    pk, pv, pki, ps, sk, sv, ski, ss = [], [], [], [], [], [], [], []
    for l in range(depth):
        pj = _inproj(x_all, norm_mix[l].reshape(1, d).astype(F32), _pack_w_in(w_in[l]), rope_tab,
                     tile2(q_norm[l]), tile2(k_norm[l]), lower_bounds[l].reshape(1, -1),
                     tiles_per_seq, n_prompt // TOK_TILE)

        att_p = _attn_prompt(pj, batch, t_pad, t_real, top_k)
        ro_p, st_p = _gla_prompt(pj, batch, t_pad, t_real)

        q_s = pj["q"][srows].reshape(db, N_HEADS, LANES)
        qi_s = pj["qi"][srows].reshape(db, N_IDX_HEADS, IDX_DIM)
        w_s = pj["kiwi"][srows, IDX_DIM:IDX_DIM + N_IDX_HEADS].reshape(db, N_IDX_HEADS, 1)
        ki_s = pj["kk"][srows, 0:IDX_DIM].reshape(db, 1, IDX_DIM)
        kn_s = pj["kb"][srows].reshape(db, 1, N_KV_HEADS * HEAD_DIM)
        vn_s = pj["vb"][srows].reshape(db, 1, N_KV_HEADS * HEAD_DIM)
        att_s = _attn_sample(page_table, l, cache_kit, cache_kt, cache_vt, qi_s, w_s, ki_s, q_s, kn_s, vn_s)
        ro_s, st_s = _gla_sample(pj["rq"][srows], pj["f"][srows], pj["rv"][srows], state_rec[l])

        pad_s = lambda a: jnp.pad(a, ((0, TOK_TILE - db), (0, 0)))
        x_all = _back(x_all, att_p, pad_s(att_s.astype(BF16)), ro_p, pad_s(ro_s),
                      pj["gr"], pj["ga"], pj["gb"], rec_norm[l].reshape(1, LANES).astype(F32),
                      w_pa[l].astype(BF16), w_pb[l].astype(BF16), w_o[l].astype(BF16),
                      norm_ffn[l].reshape(1, d).astype(F32), w_gu[l].astype(BF16), w_down[l].astype(BF16))

        seq_view = lambda a, wd: a[:n_prompt].reshape(batch, t_pad, wd)[:, :t_real]
        pk.append(seq_view(pj["kf"], 256).reshape(batch, t_real, N_KV_HEADS, HEAD_DIM))
        pv.append(seq_view(pj["vf"], 256).reshape(batch, t_real, N_KV_HEADS, HEAD_DIM))
        pki.append(seq_view(pj["kiwi"], LANES)[..., :IDX_DIM])
        ps.append(st_p)
        sk.append(pj["kf"][srows].reshape(db, 1, N_KV_HEADS, HEAD_DIM))
        sv.append(pj["vf"][srows].reshape(db, 1, N_KV_HEADS, HEAD_DIM))
        ski.append(pj["kiwi"][srows, :IDX_DIM].reshape(db, 1, IDX_DIM))
        ss.append(st_s)

    y_prompt = jnp.stack([x_all[b * t_pad + N_META:b * t_pad + t_real] for b in range(batch)])
    y_sample = x_all[srows].reshape(db, 1, d)
    return (y_prompt, y_sample, jnp.stack(pk), jnp.stack(pv), jnp.stack(pki), jnp.stack(ps),
            jnp.stack(sk), jnp.stack(sv), jnp.stack(ski), jnp.stack(ss))
```
